```python
import jax
import jax.numpy as jnp
from jax import lax
import numpy as np

D_MODEL = 1024
BATCH = 32
SEQ = 256
DEPTH = 4
DEC_BATCH = 8
DEC_SEQ = 2048
PAST_LEN = 256

GRID_W = 64
N_HEADS_NA = 8
HEAD_DIM_NA = 64
WIN_H = 8
WIN_W = 16
N_HEADS_MLA = 8
QK_NOPE_DIM = 64
QK_ROPE_DIM = 32
V_HEAD_DIM = 64
Q_LORA_RANK = 256
KV_LORA_RANK = 128
D_NA = N_HEADS_NA * HEAD_DIM_NA
D_MLA = N_HEADS_MLA * V_HEAD_DIM
D_MIX = D_NA + D_MLA
PROJ_SPLITS = (D_NA, 2 * D_NA, 3 * D_NA, 3 * D_NA + Q_LORA_RANK, 3 * D_NA + Q_LORA_RANK + KV_LORA_RANK)
D_IN_PROJ = 3 * D_NA + Q_LORA_RANK + KV_LORA_RANK + QK_ROPE_DIM
N_EXPERTS = 16
N_EXPERT_GROUPS = 4
EXPERTS_PER_GROUP = N_EXPERTS // N_EXPERT_GROUPS
GROUP_SCORE_K = 2
TOP_K = 2
D_EXPERT = 512
ROPE_THETA = 10000.0
RMS_EPS = 1e-6
Q_BLOCK = 128
NA_SCALE = HEAD_DIM_NA ** -0.5
MLA_SCALE = (QK_NOPE_DIM + QK_ROPE_DIM) ** -0.5

kernel_name = 'natten_mla_grouped_moe_prefix_dit'


def _rms_norm(x, g):
    xf = x.astype(jnp.float32)
    y = xf * lax.rsqrt(jnp.mean(xf * xf, axis=-1, keepdims=True) + RMS_EPS)
    return (y * g.astype(jnp.float32)).astype(x.dtype)


def _modulation(cond, w_ada, b_ada):
    return jnp.split(jax.nn.silu(cond) @ w_ada + b_ada, 6, axis=-1)


def _modulate(x, g, shift, scale):
    return _rms_norm(x, g) * (1 + scale) + shift


def _axial_rope(x, rows, cols):
    half = QK_ROPE_DIM // 2
    freqs = 1.0 / (ROPE_THETA ** (jnp.arange(0, half, 2, dtype=jnp.float32) / half))

    def rot(xa, pos):
        ang = pos.astype(jnp.float32)[:, None] * freqs
        cos = jnp.cos(ang)[None, :, None, :]
        sin = jnp.sin(ang)[None, :, None, :]
        x1, x2 = jnp.split(xa.astype(jnp.float32), 2, axis=-1)
        return jnp.concatenate([x1 * cos - x2 * sin, x2 * cos + x1 * sin], axis=-1).astype(xa.dtype)

    return jnp.concatenate([rot(x[..., :half], rows), rot(x[..., half:], cols)], axis=-1)


def _block_attention(q, k, v, scale):
    b, lq, h, dk = q.shape
    qb = jnp.moveaxis(q.reshape(b, lq // Q_BLOCK, Q_BLOCK, h, dk), 1, 0)

    def one(qblk):
        s = jnp.einsum('bqhd,bkhd->bhqk', qblk, k).astype(jnp.float32) * scale
        p = jax.nn.softmax(s, axis=-1).astype(v.dtype)
        return jnp.einsum('bhqk,bkhd->bqhd', p, v)

    o = lax.map(one, qb)
    return jnp.moveaxis(o, 0, 1).reshape(b, lq, h, v.shape[-1])


def _neighbourhood_attention(q, k, v, k_ctx, v_ctx, rpb):
    b, s, h, d = q.shape
    rows = s // GRID_W
    kh = min(WIN_H, rows)
    ncb = GRID_W // WIN_W
    band = 2 * WIN_W
    qg = q.reshape(b, rows, ncb, WIN_W, h, d)
    kg = k.reshape(b, rows, GRID_W, h, d)
    vg = v.reshape(b, rows, GRID_W, h, d)
    col_start = jnp.clip(jnp.arange(ncb) * WIN_W - WIN_W // 2, 0, GRID_W - band)
    col_idx = col_start[:, None] + jnp.arange(band)
    q_col = jnp.arange(GRID_W).reshape(ncb, WIN_W)
    win_start = jnp.clip(q_col - WIN_W // 2, 0, GRID_W - WIN_W)
    key_col = col_idx[:, None, :]
    col_mask = (key_col >= win_start[..., None]) & (key_col < win_start[..., None] + WIN_W)
    col_off = jnp.clip(key_col - q_col[..., None] + WIN_W - 1, 0, 2 * WIN_W - 2)
    n_win = kh * band

    def one_row(r):
        rs = jnp.clip(r - kh // 2, 0, rows - kh)
        k_band = lax.dynamic_slice_in_dim(kg, rs, kh, axis=1)[:, :, col_idx]
        v_band = lax.dynamic_slice_in_dim(vg, rs, kh, axis=1)[:, :, col_idx]
        qr = lax.dynamic_index_in_dim(qg, r, axis=1, keepdims=False)
        row_off = rs + jnp.arange(kh) - r + WIN_H - 1
        bias = rpb[:, row_off[None, None, :, None], col_off[:, :, None, :]]
        s_win = jnp.einsum('bnqhd,bknwhd->bhnqkw', qr, k_band).astype(jnp.float32) * NA_SCALE + bias.astype(jnp.float32)
        s_win = jnp.where(col_mask[:, :, None, :], s_win, -jnp.inf)
        s_ctx = jnp.einsum('bnqhd,bchd->bhnqc', qr, k_ctx).astype(jnp.float32) * NA_SCALE
        p = jax.nn.softmax(jnp.concatenate([s_win.reshape(b, h, ncb, WIN_W, n_win), s_ctx], axis=-1), axis=-1).astype(v.dtype)
        p_win = p[..., :n_win].reshape(b, h, ncb, WIN_W, kh, band)
        return (jnp.einsum('bhnqkw,bknwhd->bnqhd', p_win, v_band)
                + jnp.einsum('bhnqc,bchd->bnqhd', p[..., n_win:], v_ctx))

    o = lax.map(one_row, jnp.arange(rows))
    return jnp.moveaxis(o, 0, 1).reshape(b, s, h, d)


def _project(h, w_in, q_norm_g, w_uq, kv_norm_g):
    b, l, _ = h.shape
    q_na, k_na, v_na, cq, ckv, k_rope = jnp.split(h @ w_in, PROJ_SPLITS, axis=-1)
    heads = lambda t: t.reshape(b, l, N_HEADS_NA, HEAD_DIM_NA)
    q_mla = (_rms_norm(cq, q_norm_g) @ w_uq).reshape(b, l, N_HEADS_MLA, QK_NOPE_DIM + QK_ROPE_DIM)
    return heads(q_na), heads(k_na), heads(v_na), q_mla, _rms_norm(ckv, kv_norm_g), k_rope


def _mla_kv(ckv, k_rope, w_ukv):
    b, l, _ = ckv.shape
    kv = (ckv @ w_ukv).reshape(b, l, N_HEADS_MLA, QK_NOPE_DIM + V_HEAD_DIM)
    k_r = jnp.broadcast_to(k_rope[:, :, None, :], (b, l, N_HEADS_MLA, QK_ROPE_DIM))
    return jnp.concatenate([kv[..., :QK_NOPE_DIM], k_r], axis=-1), kv[..., QK_NOPE_DIM:]


def _grouped_moe(h, w_router, b_router, w_gate, w_up, w_down):
    shape = h.shape
    t = h.reshape(-1, shape[-1])
    n = t.shape[0]
    aff = jax.nn.sigmoid((t @ w_router).astype(jnp.float32))
    sel = (aff + b_router.astype(jnp.float32)).reshape(n, N_EXPERT_GROUPS, EXPERTS_PER_GROUP)
    group_score = lax.top_k(sel, GROUP_SCORE_K)[0].sum(-1)
    g_idx = jnp.argmax(group_score, axis=-1)
    in_group = jnp.take_along_axis(sel, g_idx[:, None, None], axis=1)[:, 0]
    _, local = lax.top_k(in_group, TOP_K)
    e_idx = g_idx[:, None] * EXPERTS_PER_GROUP + local
    w_sel = jnp.take_along_axis(aff, e_idx, axis=1)
    w_sel = w_sel / jnp.sum(w_sel, axis=-1, keepdims=True)
    gates = jnp.sum(jax.nn.one_hot(e_idx, N_EXPERTS, dtype=jnp.float32) * w_sel[..., None], axis=1)
    hid = jax.nn.silu(jnp.einsum('nd,edf->nef', t, w_gate)) * jnp.einsum('nd,edf->nef', t, w_up)
    out = jnp.einsum('nef,efd->nd', hid * gates[..., None].astype(hid.dtype), w_down)
    return out.reshape(shape)


def _context_layer(x, mod, lw):
    n1, n2, w_in, qn, w_uq, kvn, w_ukv, w_out, w_router, b_router, w_gate, w_up, w_down = lw
    sh1, sc1, g1, sh2, sc2, g2 = mod
    b, l, _ = x.shape
    q_na, k_na, v_na, q_mla, ckv, k_rope = _project(_modulate(x, n1, sh1, sc1), w_in, qn, w_uq, kvn)
    o_na = _block_attention(q_na, k_na, v_na, NA_SCALE)
    k_mla, v_mla = _mla_kv(ckv, k_rope, w_ukv)
    o_mla = _block_attention(q_mla, k_mla, v_mla, MLA_SCALE)
    o = jnp.concatenate([o_na.reshape(b, l, D_NA), o_mla.reshape(b, l, D_MLA)], axis=-1) @ w_out
    x = x + g1 * o
    x = x + g2 * _grouped_moe(_modulate(x, n2, sh2, sc2), w_router, b_router, w_gate, w_up, w_down)
    return x, (k_na, v_na, ckv, k_rope)


def _latent_layer(x, mod, lw, rpb, ctx, rows_pos, cols_pos):
    n1, n2, w_in, qn, w_uq, kvn, w_ukv, w_out, w_router, b_router, w_gate, w_up, w_down = lw
    sh1, sc1, g1, sh2, sc2, g2 = mod
    k_ctx_na, v_ctx_na, ckv_ctx, krope_ctx = ctx
    b, s, _ = x.shape
    q_na, k_na, v_na, q_mla, ckv, k_rope = _project(_modulate(x, n1, sh1, sc1), w_in, qn, w_uq, kvn)
    o_na = _neighbourhood_attention(q_na, k_na, v_na, k_ctx_na, v_ctx_na, rpb)
    q_mla = jnp.concatenate([q_mla[..., :QK_NOPE_DIM],
                             _axial_rope(q_mla[..., QK_NOPE_DIM:], rows_pos, cols_pos)], axis=-1)
    k_rope = _axial_rope(k_rope[:, :, None, :], rows_pos, cols_pos)[:, :, 0, :]
    k_lat, v_lat = _mla_kv(ckv, k_rope, w_ukv)
    k_c, v_c = _mla_kv(ckv_ctx, krope_ctx, w_ukv)
    o_mla = _block_attention(q_mla, jnp.concatenate([k_c, k_lat], axis=1),
                             jnp.concatenate([v_c, v_lat], axis=1), MLA_SCALE)
    o = jnp.concatenate([o_na.reshape(b, s, D_NA), o_mla.reshape(b, s, D_MLA)], axis=-1) @ w_out
    x = x + g1 * o
    x = x + g2 * _grouped_moe(_modulate(x, n2, sh2, sc2), w_router, b_router, w_gate, w_up, w_down)
    return x


def setup_inputs(seed: int = 0) -> dict:
    key = jax.random.key(seed)
    ks = jax.random.split(key, 25)
    nrm = lambda k, shape, scale: jax.random.normal(k, shape, jnp.float32) * scale
    gain = lambda k, shape: 1.0 + 0.05 * jax.random.normal(k, shape, jnp.float32)
    return {
        'x_prompt': nrm(ks[0], (BATCH, SEQ, D_MODEL), 1.0),
        'x_sample': nrm(ks[1], (DEC_BATCH, DEC_SEQ, D_MODEL), 1.0),
        'cache_nat_k': nrm(ks[2], (DEC_BATCH, DEPTH, PAST_LEN, N_HEADS_NA, HEAD_DIM_NA), 1.0),
        'cache_nat_v': nrm(ks[3], (DEC_BATCH, DEPTH, PAST_LEN, N_HEADS_NA, HEAD_DIM_NA), 1.0),
        'cache_mla_ckv': nrm(ks[4], (DEC_BATCH, DEPTH, PAST_LEN, KV_LORA_RANK), 1.0),
        'cache_mla_krope': nrm(ks[5], (DEC_BATCH, DEPTH, PAST_LEN, QK_ROPE_DIM), 1.0),
        'c': nrm(ks[6], (DEC_BATCH, D_MODEL), 1.0),
        'c_ctx': nrm(ks[7], (D_MODEL,), 1.0),
        'norm1_g': gain(ks[8], (DEPTH, D_MODEL)),
        'norm2_g': gain(ks[9], (DEPTH, D_MODEL)),
        'w_ada': nrm(ks[10], (DEPTH, D_MODEL, 6 * D_MODEL), 0.2 * D_MODEL ** -0.5),
        'b_ada': nrm(ks[11], (DEPTH, 6 * D_MODEL), 0.01),
        'w_in': nrm(ks[12], (DEPTH, D_MODEL, D_IN_PROJ), D_MODEL ** -0.5),
        'q_norm_g': gain(ks[13], (DEPTH, Q_LORA_RANK)),
        'w_uq': nrm(ks[14], (DEPTH, Q_LORA_RANK, N_HEADS_MLA * (QK_NOPE_DIM + QK_ROPE_DIM)), Q_LORA_RANK ** -0.5),
        'kv_norm_g': gain(ks[15], (DEPTH, KV_LORA_RANK)),
        'w_ukv': nrm(ks[16], (DEPTH, KV_LORA_RANK, N_HEADS_MLA * (QK_NOPE_DIM + V_HEAD_DIM)), KV_LORA_RANK ** -0.5),
        'rpb': nrm(ks[17], (DEPTH, N_HEADS_NA, 2 * WIN_H - 1, 2 * WIN_W - 1), 0.02),
        'w_out': nrm(ks[18], (DEPTH, D_MIX, D_MODEL), D_MIX ** -0.5),
        'w_router': nrm(ks[19], (D_MODEL, N_EXPERTS), D_MODEL ** -0.5),
        'b_router': nrm(ks[20], (N_EXPERTS,), 0.01),
        'w_gate': nrm(ks[21], (DEPTH, N_EXPERTS, D_MODEL, D_EXPERT), D_MODEL ** -0.5),
        'w_up': nrm(ks[22], (DEPTH, N_EXPERTS, D_MODEL, D_EXPERT), D_MODEL ** -0.5),
        'w_down': nrm(ks[23], (DEPTH, N_EXPERTS, D_EXPERT, D_MODEL), D_EXPERT ** -0.5),
        'final_norm_g': gain(ks[24], (D_MODEL,)),
    }


def reference(x_prompt, x_sample, cache_nat_k, cache_nat_v, cache_mla_ckv, cache_mla_krope, c, c_ctx,
              norm1_g, norm2_g, w_ada, b_ada, w_in, q_norm_g, w_uq, kv_norm_g, w_ukv, rpb, w_out,
              w_router, b_router, w_gate, w_up, w_down, final_norm_g):
    s = x_sample.shape[1]
    t = jnp.arange(s)
    rows_pos = t // GRID_W
    cols_pos = t % GRID_W
    xc = x_prompt
    xs = x_sample
    nat_k, nat_v, mla_ckv, mla_krope = [], [], [], []
    for l in range(DEPTH):
        lw = (norm1_g[l], norm2_g[l], w_in[l], q_norm_g[l], w_uq[l], kv_norm_g[l], w_ukv[l], w_out[l],
              w_router, b_router, w_gate[l], w_up[l], w_down[l])
        mod_ctx = _modulation(c_ctx, w_ada[l], b_ada[l])
        mod_lat = [m[:, None, :] for m in _modulation(c, w_ada[l], b_ada[l])]
        xc, (k_na, v_na, ckv, k_rope) = _context_layer(xc, mod_ctx, lw)
        nat_k.append(k_na)
        nat_v.append(v_na)
        mla_ckv.append(ckv)
        mla_krope.append(k_rope)
        ctx = (cache_nat_k[:, l], cache_nat_v[:, l], cache_mla_ckv[:, l], cache_mla_krope[:, l])
        xs = _latent_layer(xs, mod_lat, lw, rpb[l], ctx, rows_pos, cols_pos)
    y_prompt = _rms_norm(xc, final_norm_g)
    y_sample = _rms_norm(xs, final_norm_g)
    new_nat_k = jnp.stack(nat_k, axis=1)
    new_nat_v = jnp.stack(nat_v, axis=1)
    new_mla_ckv = jnp.stack(mla_ckv, axis=1)
    new_mla_krope = jnp.stack(mla_krope, axis=1)
    return (y_prompt, y_sample, new_nat_k, new_nat_v, new_mla_ckv, new_mla_krope)
```

```python
import functools

import numpy as np
import jax
import jax.numpy as jnp
from jax import lax
from jax.experimental import pallas as pl
from jax.experimental.pallas import tpu as pltpu

F32 = jnp.float32
BF16 = jnp.bfloat16

D = 1024
B_CTX, S_CTX = 32, 256
B_LAT, S_LAT = 8, 2048
DEPTH = 4
PAST = 256
GRID_W, GRID_H = 64, 32
H_NA, DH_NA = 8, 64
WIN_H, WIN_W = 8, 16
H_MLA, NOPE, ROPE, VH = 8, 64, 32, 64
Q_LORA, KV_LORA = 256, 128
D_NA = H_NA * DH_NA
D_MLA = H_MLA * VH
N_EXP, N_GRP, EPG = 16, 4, 4
D_EXP = 512
ROPE_THETA = 10000.0
EPS = 1e-6
NA_SCALE = DH_NA ** -0.5
MLA_SCALE = (NOPE + ROPE) ** -0.5

N_CTX = B_CTX * S_CTX
N_LAT = B_LAT * S_LAT
N_TOK = N_CTX + N_LAT
TM = 256
CTX_TILES = N_CTX // TM
LAT_TILES = N_LAT // TM
N_TILES = CTX_TILES + LAT_TILES
LAT_TILES_PER_SEQ = S_LAT // TM
LANE = 128
HP = LANE
D_QP = H_MLA * HP
W_IN_COLS = 3 * D_NA + Q_LORA + KV_LORA + 2 * LANE
N_MOD_ROWS = 16
N_PAIR = 6
N_BUCKET = N_GRP * N_PAIR
MOE_TILES = N_TILES + N_BUCKET
NA_ROWS_PER_TILE = TM // GRID_W
NA_KEY_ROWS = 12
NA_KEYS = NA_KEY_ROWS * GRID_W
VMEM_LIMIT = 48 * 1024 * 1024

_PAIR_LO = np.array([0, 0, 0, 1, 1, 2], np.int32)
_PAIR_HI = np.array([1, 2, 3, 2, 3, 3], np.int32)


def _dot(a, b):
    return jnp.dot(a, b, preferred_element_type=F32)


def _dot_nt(a, b):
    return lax.dot_general(a, b, (((1,), (1,)), ((), ())), preferred_element_type=F32)


def _rms(x, g):
    return x * lax.rsqrt(jnp.mean(x * x, axis=-1, keepdims=True) + EPS) * g


def _mod_row(i):
    return jnp.where(i < CTX_TILES, 0, 1 + (i - CTX_TILES) // LAT_TILES_PER_SEQ)


def _pos_block(i):
    return jnp.where(i < CTX_TILES, 0, 1 + (i - CTX_TILES) % LAT_TILES_PER_SEQ)


def _ctx_block(i):
    return jnp.minimum(i, CTX_TILES - 1)


def _params(sem):
    return pltpu.CompilerParams(dimension_semantics=sem, vmem_limit_bytes=VMEM_LIMIT)


def _ada_kernel(cond_ref, w_ref, b_ref, o_ref):
    cond = cond_ref[...]
    act = cond * jax.nn.sigmoid(cond)
    o_ref[0] = jnp.dot(act, w_ref[0], preferred_element_type=F32,
                       precision=lax.Precision.HIGHEST) + b_ref[0]


def _ada(cond, w_ada, b_ada):
    nt = 1536
    return pl.pallas_call(
        _ada_kernel,
        grid=(DEPTH, 6 * D // nt),
        in_specs=[pl.BlockSpec((N_MOD_ROWS, D), lambda l, j: (0, 0)),
                  pl.BlockSpec((1, D, nt), lambda l, j: (l, 0, j)),
                  pl.BlockSpec((1, 1, nt), lambda l, j: (l, 0, j))],
        out_specs=pl.BlockSpec((1, N_MOD_ROWS, nt), lambda l, j: (l, 0, j)),
        out_shape=jax.ShapeDtypeStruct((DEPTH, N_MOD_ROWS, 6 * D), F32),
        compiler_params=_params(("arbitrary", "arbitrary")),
        name="ada",
    )(cond, w_ada, b_ada.reshape(DEPTH, 1, 6 * D))


def _ctxkv_kernel(ckv_ref, kr_ref, wuk_ref, wuv_ref, place_ref, kc_ref, vc_ref):
    ckv = ckv_ref[0, 0].astype(BF16)
    kr = _dot(kr_ref[0, 0].astype(BF16), place_ref[...])
    kc_ref[0, 0] = (_dot(ckv, wuk_ref[0]) + jnp.tile(kr, (1, H_MLA))).astype(BF16)
    vc_ref[0, 0] = _dot(ckv, wuv_ref[0]).astype(BF16)


def _ctxkv(cache_ckv, cache_krope, w_uk_p, w_uv, place):
    return pl.pallas_call(
        _ctxkv_kernel,
        grid=(DEPTH, B_LAT),
        in_specs=[pl.BlockSpec((1, 1, PAST, KV_LORA), lambda l, b: (b, l, 0, 0)),
                  pl.BlockSpec((1, 1, PAST, ROPE), lambda l, b: (b, l, 0, 0)),
                  pl.BlockSpec((1, KV_LORA, D_QP), lambda l, b: (l, 0, 0)),
                  pl.BlockSpec((1, KV_LORA, D_MLA), lambda l, b: (l, 0, 0)),
                  pl.BlockSpec((ROPE, HP), lambda l, b: (0, 0))],
        out_specs=[pl.BlockSpec((1, 1, PAST, D_QP), lambda l, b: (l, b, 0, 0)),
                   pl.BlockSpec((1, 1, PAST, D_MLA), lambda l, b: (l, b, 0, 0))],
        out_shape=[jax.ShapeDtypeStruct((DEPTH, B_LAT, PAST, D_QP), BF16),
                   jax.ShapeDtypeStruct((DEPTH, B_LAT, PAST, D_MLA), BF16)],
        compiler_params=_params(("arbitrary", "arbitrary")),
        name="ctxkv",
    )(cache_ckv, cache_krope, w_uk_p, w_uv, place)


def _pre_kernel(first, *refs):
    if first:
        x_ref, *rest = refs
    else:
        x1_ref, moe_ref, modp_ref, *rest = refs
    (mod_ref, n1_ref, win_ref, qn_ref, wuq_ref, kvn_ref, wuk_ref, wuv_ref, cos_ref, sin_ref, *outs) = rest
    if not first:
        xo_ref, *outs = outs
    (qna_ref, kna_ref, vna_ref, qp_ref, kp_ref, vm_ref, knew_ref, vnew_ref, ckvnew_ref, krnew_ref) = outs
    i = pl.program_id(0)

    if first:
        x = x_ref[...]
    else:
        x = x1_ref[...] + modp_ref[0, 5:6, :] * moe_ref[...]
        xo_ref[...] = x
    m = mod_ref[0]
    h = _rms(x, n1_ref[0]) * (1.0 + m[1:2, :]) + m[0:1, :]
    proj = _dot(h.astype(BF16), win_ref[0])
    k_na = proj[:, D_NA:2 * D_NA]
    v_na = proj[:, 2 * D_NA:3 * D_NA]
    qna_ref[...] = (proj[:, :D_NA] * NA_SCALE).astype(BF16)
    kna_ref[...] = k_na.astype(BF16)
    vna_ref[...] = v_na.astype(BF16)
    o = 3 * D_NA
    cq = proj[:, o:o + Q_LORA]
    ckv = proj[:, o + Q_LORA:o + Q_LORA + KV_LORA]
    kr = proj[:, o + Q_LORA + KV_LORA:o + Q_LORA + KV_LORA + LANE]
    kr_sw = proj[:, o + Q_LORA + KV_LORA + LANE:]
    cos = cos_ref[...]
    sin = sin_ref[...]

    cqn = _rms(cq, qn_ref[0]).astype(BF16)
    q2 = _dot(cqn, wuq_ref[0])
    qp = q2[:, :D_QP] * jnp.tile(cos, (1, H_MLA)) + q2[:, D_QP:] * jnp.tile(sin, (1, H_MLA))
    qp_ref[...] = qp.astype(BF16)

    ckvn = _rms(ckv, kvn_ref[0])
    ckvb = ckvn.astype(BF16)
    kr_rot = kr * cos + kr_sw * sin
    kp_ref[...] = (_dot(ckvb, wuk_ref[0]) + jnp.tile(kr_rot, (1, H_MLA))).astype(BF16)
    vm_ref[...] = _dot(ckvb, wuv_ref[0]).astype(BF16)

    @pl.when(i < CTX_TILES)
    def _():
        knew_ref[...] = k_na
        vnew_ref[...] = v_na
        ckvnew_ref[...] = ckvn
        krnew_ref[...] = kr[:, NOPE:NOPE + ROPE]


def _pre(l, first, xin, mods, wts, tabs):
    tok = lambda n: pl.BlockSpec((TM, n), lambda i: (i, 0))
    ctx = lambda n: pl.BlockSpec((TM, n), lambda i: (_ctx_block(i), 0))
    lay = lambda a: pl.BlockSpec((1,) + a.shape[1:], lambda i: (l,) + (0,) * (a.ndim - 1))
    in_specs = []
    args = []
    if first:
        in_specs.append(tok(D))
        args.append(xin)
    else:
        x1, moe = xin
        in_specs += [tok(D), tok(D), pl.BlockSpec((1, 6, D), lambda i: ((l - 1) * N_MOD_ROWS + _mod_row(i), 0, 0))]
        args += [x1, moe, mods]
    in_specs.append(pl.BlockSpec((1, 6, D), lambda i: (l * N_MOD_ROWS + _mod_row(i), 0, 0)))
    args.append(mods)
    for name in ("n1", "w_in", "qn", "w_uq", "kvn", "w_uk", "w_uv"):
        in_specs.append(lay(wts[name]))
        args.append(wts[name])
    for t in tabs:
        in_specs.append(pl.BlockSpec((TM, LANE), lambda i: (_pos_block(i), 0)))
        args.append(t)
    out_specs = [tok(D), tok(D_NA), tok(D_NA), tok(D_NA), tok(D_QP), tok(D_QP), tok(D_MLA),
                 ctx(D_NA), ctx(D_NA), ctx(KV_LORA), ctx(ROPE)]
    out_shape = [jax.ShapeDtypeStruct((N_TOK, D), F32),
                 jax.ShapeDtypeStruct((N_TOK, D_NA), BF16),
                 jax.ShapeDtypeStruct((N_TOK, D_NA), BF16),
                 jax.ShapeDtypeStruct((N_TOK, D_NA), BF16),
                 jax.ShapeDtypeStruct((N_TOK, D_QP), BF16),
                 jax.ShapeDtypeStruct((N_TOK, D_QP), BF16),
                 jax.ShapeDtypeStruct((N_TOK, D_MLA), BF16),
                 jax.ShapeDtypeStruct((N_CTX, D_NA), F32),
                 jax.ShapeDtypeStruct((N_CTX, D_NA), F32),
                 jax.ShapeDtypeStruct((N_CTX, KV_LORA), F32),
                 jax.ShapeDtypeStruct((N_CTX, ROPE), F32)]
    if first:
        out_specs = out_specs[1:]
        out_shape = out_shape[1:]
    return pl.pallas_call(
        functools.partial(_pre_kernel, first),
        grid=(N_TILES,),
        in_specs=in_specs,
        out_specs=out_specs,
        out_shape=out_shape,
        compiler_params=_params(("arbitrary",)),
        name="pre",
    )(*args)


def _softmax_parts(parts):
    m = parts[0].max(axis=-1, keepdims=True)
    for s in parts[1:]:
        m = jnp.maximum(m, s.max(axis=-1, keepdims=True))
    ps = [jnp.exp(s - m) for s in parts]
    den = ps[0].sum(axis=-1, keepdims=True)
    for p in ps[1:]:
        den = den + p.sum(axis=-1, keepdims=True)
    return ps, den


def _low_lanes():
    return lax.broadcasted_iota(jnp.int32, (TM, LANE), 1) < DH_NA


def _split_pair(qpair, low):
    zero = jnp.zeros_like(qpair)
    return jnp.where(low, qpair, zero), jnp.where(low, zero, qpair)


def _ctx_attn_kernel(qna_ref, kna_ref, vna_ref, qp_ref, kp_ref, vm_ref, o_ref):
    low = _low_lanes()
    outs = []
    for hp in range(H_NA // 2):
        sl = slice(hp * LANE, (hp + 1) * LANE)
        kpair = kna_ref[:, sl]
        vpair = vna_ref[:, sl]
        res = []
        for qh in _split_pair(qna_ref[:, sl], low):
            (p,), den = _softmax_parts([_dot_nt(qh, kpair)])
            res.append(_dot(p.astype(BF16), vpair) / den)
        outs.append(jnp.where(low, res[0], res[1]))
    for hp in range(H_MLA // 2):
        vpair = vm_ref[:, hp * LANE:(hp + 1) * LANE]
        res = []
        for h in (2 * hp, 2 * hp + 1):
            sl = slice(h * HP, (h + 1) * HP)
            (p,), den = _softmax_parts([_dot_nt(qp_ref[:, sl], kp_ref[:, sl]) * MLA_SCALE])
            res.append(_dot(p.astype(BF16), vpair) / den)
        outs.append(jnp.where(low, res[0], res[1]))
    o_ref[...] = jnp.concatenate(outs, axis=1).astype(BF16)


def _ctx_attn(qna, kna, vna, qp, kp, vm):
    tok = lambda n: pl.BlockSpec((S_CTX, n), lambda i: (i, 0))
    return pl.pallas_call(
        _ctx_attn_kernel,
        grid=(B_CTX,),
        in_specs=[tok(D_NA), tok(D_NA), tok(D_NA), tok(D_QP), tok(D_QP), tok(D_MLA)],
        out_specs=tok(D_NA + D_MLA),
        out_shape=jax.ShapeDtypeStruct((N_CTX, D_NA + D_MLA), BF16),
        compiler_params=_params(("arbitrary",)),
        name="ctx_attn",
    )(qna, kna, vna, qp, kp, vm)


def _na_kernel(q_ref, k_ref, v_ref, kc_ref, vc_ref, bias_ref, o_ref):
    t = pl.program_id(1)
    row0 = jnp.clip(t * NA_ROWS_PER_TILE - WIN_H // 2, 0, GRID_H - NA_KEY_ROWS)
    ks = pl.multiple_of(row0 * GRID_W, GRID_W)
    low = _low_lanes()
    outs = []
    for hp in range(H_NA // 2):
        sl = slice(hp * LANE, (hp + 1) * LANE)
        kwin = k_ref[pl.ds(ks, NA_KEYS), sl]
        vwin = v_ref[pl.ds(ks, NA_KEYS), sl]
        kctx = kc_ref[0, 0, :, sl].astype(BF16)
        vctx = vc_ref[0, 0, :, sl].astype(BF16)
        res = []
        for half, qh in enumerate(_split_pair(q_ref[:, sl], low)):
            s_win = _dot_nt(qh, kwin) + bias_ref[0, 2 * hp + half]
            s_ctx = _dot_nt(qh, kctx)
            (p_win, p_ctx), den = _softmax_parts([s_win, s_ctx])
            o = _dot(p_win.astype(BF16), vwin) + _dot(p_ctx.astype(BF16), vctx)
            res.append(o / den)
        outs.append(jnp.where(low, res[0], res[1]))
    o_ref[...] = jnp.concatenate(outs, axis=1).astype(BF16)


def _na_attn(l, qna, kna, vna, cache_k, cache_v, bias):
    nt = LAT_TILES_PER_SEQ
    seq = lambda n: pl.BlockSpec((S_LAT, n), lambda b, t: (N_CTX // S_LAT + b, 0))
    cache = pl.BlockSpec((1, 1, PAST, D_NA), lambda b, t: (b, l, 0, 0))
    kind = lambda t: jnp.where(t == 0, 0, jnp.where(t == nt - 1, 2, 1))
    return pl.pallas_call(
        _na_kernel,
        grid=(B_LAT, nt),
        in_specs=[pl.BlockSpec((TM, D_NA), lambda b, t: (CTX_TILES + b * nt + t, 0)),
                  seq(D_NA), seq(D_NA), cache, cache,
                  pl.BlockSpec((1, H_NA, TM, NA_KEYS), lambda b, t: (kind(t), 0, 0, 0))],
        out_specs=pl.BlockSpec((TM, D_NA), lambda b, t: (b * nt + t, 0)),
        out_shape=jax.ShapeDtypeStruct((N_LAT, D_NA), BF16),
        compiler_params=_params(("arbitrary", "arbitrary")),
        name="na_attn",
    )(qna, kna, vna, cache_k, cache_v, bias)


def _na_bias_index():
    qi = np.arange(TM)
    kj = np.arange(NA_KEYS)
    qr_local, qc = qi // GRID_W, qi % GRID_W
    kr_local, kc = kj // GRID_W, kj % GRID_W
    rows, cols, oks = [], [], []
    for r0, k0 in ((0, 0), (NA_ROWS_PER_TILE, 0), (GRID_H - NA_ROWS_PER_TILE, GRID_H - NA_KEY_ROWS)):
        r = (r0 + qr_local)[:, None]
        rs = np.clip(r - WIN_H // 2, 0, GRID_H - WIN_H)
        kr = (k0 + kr_local)[None, :]
        ws = np.clip(qc - WIN_W // 2, 0, GRID_W - WIN_W)[:, None]
        ok = (kr >= rs) & (kr < rs + WIN_H) & (kc[None, :] >= ws) & (kc[None, :] < ws + WIN_W)
        rows.append(np.clip(kr - r + WIN_H - 1, 0, 2 * WIN_H - 2))
        cols.append(np.clip(kc[None, :] - qc[:, None] + WIN_W - 1, 0, 2 * WIN_W - 2))
        oks.append(ok)
    return np.stack(rows), np.stack(cols), np.stack(oks)


def _mla_kernel(q_ref, k_ref, v_ref, kc_ref, vc_ref, o_ref):
    low = _low_lanes()
    outs = []
    for hp in range(H_MLA // 2):
        vsl = slice(hp * LANE, (hp + 1) * LANE)
        vlat = v_ref[:, vsl]
        vctx = vc_ref[0, 0, :, vsl]
        res = []
        for h in (2 * hp, 2 * hp + 1):
            sl = slice(h * HP, (h + 1) * HP)
            q = q_ref[:, sl]
            s_ctx = _dot_nt(q, kc_ref[0, 0, :, sl]) * MLA_SCALE
            s_lat = _dot_nt(q, k_ref[:, sl]) * MLA_SCALE
            (p_ctx, p_lat), den = _softmax_parts([s_ctx, s_lat])
            o = _dot(p_ctx.astype(BF16), vctx) + _dot(p_lat.astype(BF16), vlat)
            res.append(o / den)
        outs.append(jnp.where(low, res[0], res[1]))
    o_ref[...] = jnp.concatenate(outs, axis=1).astype(BF16)


def _mla_attn(l, qp, kp, vm, kc, vc):
    nt = LAT_TILES_PER_SEQ
    seq = lambda n: pl.BlockSpec((S_LAT, n), lambda b, t: (N_CTX // S_LAT + b, 0))
    return pl.pallas_call(
        _mla_kernel,
        grid=(B_LAT, nt),
        in_specs=[pl.BlockSpec((TM, D_QP), lambda b, t: (CTX_TILES + b * nt + t, 0)),
                  seq(D_QP), seq(D_MLA),
                  pl.BlockSpec((1, 1, PAST, D_QP), lambda b, t: (l, b, 0, 0)),
                  pl.BlockSpec((1, 1, PAST, D_MLA), lambda b, t: (l, b, 0, 0))],
        out_specs=pl.BlockSpec((TM, D_MLA), lambda b, t: (b * nt + t, 0)),
        out_shape=jax.ShapeDtypeStruct((N_LAT, D_MLA), BF16),
        compiler_params=_params(("arbitrary", "arbitrary")),
        name="mla_attn",
    )(qp, kp, vm, kc, vc)


def _route(sel, aff):
    rows = lambda a, g: [a[EPG * g + e:EPG * g + e + 1, :] for e in range(EPG)]
    best = None
    for g in range(N_GRP):
        v = rows(sel, g)
        score = v[0] + v[1]
        for a, b in ((0, 2), (0, 3), (1, 2), (1, 3), (2, 3)):
            score = jnp.maximum(score, v[a] + v[b])
        if best is None:
            best, grp = score, jnp.zeros_like(score)
        else:
            better = score > best
            best = jnp.where(better, score, best)
            grp = jnp.where(better, float(g), grp)
    zero = jnp.zeros_like(best)
    sv = [zero] * EPG
    av = [zero] * EPG
    for g in range(N_GRP):
        hit = grp == float(g)
        sg, ag = rows(sel, g), rows(aff, g)
        sv = [jnp.where(hit, sg[e], sv[e]) for e in range(EPG)]
        av = [jnp.where(hit, ag[e], av[e]) for e in range(EPG)]

    def argmax_first(vals, skip=None):
        bv, bi = None, None
        for e in range(EPG):
            v = vals[e] if skip is None else jnp.where(skip == float(e), -jnp.inf, vals[e])
            if bv is None:
                bv, bi = v, jnp.zeros_like(v)
            else:
                better = v > bv
                bv = jnp.where(better, v, bv)
                bi = jnp.where(better, float(e), bi)
        return bi

    i1 = argmax_first(sv)
    i2 = argmax_first(sv, skip=i1)
    lo = jnp.minimum(i1, i2)
    hi = jnp.maximum(i1, i2)
    pick = lambda idx: sum(jnp.where(idx == float(e), av[e], 0.0) for e in range(EPG))
    a_lo, a_hi = pick(lo), pick(hi)
    den = a_lo + a_hi
    pair = jnp.where(lo == 0.0, 0.0, jnp.where(lo == 1.0, 3.0, 5.0)) + hi - lo - 1.0
    return grp * float(N_PAIR) + pair, a_lo / den, a_hi / den


def _post_kernel(x_ref, octx_ref, ona_ref, omla_ref, mod_ref, n2_ref, wout_ref, wr_ref, br_ref,
                 x1_ref, h2_ref, route_ref):
    i = pl.program_id(0)
    is_ctx = i < CTX_TILES
    o_lat = jnp.concatenate([ona_ref[...], omla_ref[...]], axis=1)
    o = jnp.where(is_ctx, octx_ref[...], o_lat)
    m = mod_ref[0]
    x1 = x_ref[...] + m[2:3, :] * _dot(o, wout_ref[0])
    x1_ref[...] = x1
    h2 = _rms(x1, n2_ref[0]) * (1.0 + m[4:5, :]) + m[3:4, :]
    h2_ref[...] = h2
    logits = lax.dot_general(wr_ref[...], h2, (((1,), (1,)), ((), ())),
                             preferred_element_type=F32, precision=lax.Precision.HIGHEST)
    aff = jax.nn.sigmoid(logits)
    bucket, g_lo, g_hi = _route(aff + br_ref[...], aff)
    route_ref[0] = jnp.concatenate([bucket, g_lo, g_hi, jnp.zeros((5, TM), F32)], axis=0)


def _post(l, x, o_ctx, o_na, o_mla, mods, wts):
    tok = lambda n: pl.BlockSpec((TM, n), lambda i: (i, 0))
    lat = lambda n: pl.BlockSpec((TM, n), lambda i: (jnp.maximum(i - CTX_TILES, 0), 0))
    lay = lambda a: pl.BlockSpec((1,) + a.shape[1:], lambda i: (l,) + (0,) * (a.ndim - 1))
    full = lambda a: pl.BlockSpec(a.shape, lambda i: (0,) * a.ndim)
    return pl.pallas_call(
        _post_kernel,
        grid=(N_TILES,),
        in_specs=[tok(D), pl.BlockSpec((TM, D), lambda i: (_ctx_block(i), 0)), lat(D_NA), lat(D_MLA),
                  pl.BlockSpec((1, 6, D), lambda i: (l * N_MOD_ROWS + _mod_row(i), 0, 0)),
                  lay(wts["n2"]), lay(wts["w_out"]), full(wts["w_rt"]), full(wts["b_r"])],
        out_specs=[tok(D), tok(D), pl.BlockSpec((1, 8, TM), lambda i: (i, 0, 0))],
        out_shape=[jax.ShapeDtypeStruct((N_TOK, D), F32),
                   jax.ShapeDtypeStruct((N_TOK, D), F32),
                   jax.ShapeDtypeStruct((N_TILES, 8, TM), F32)],
        compiler_params=_params(("arbitrary",)),
        name="post",
    )(x, o_ctx, o_na, o_mla, mods, wts["n2"], wts["w_out"], wts["w_rt"], wts["b_r"])


def _moe_kernel(elo_ref, ehi_ref, nv_ref, src_ref,
                h2_hbm, gate_ref, wg1, wu1, wd1, wg2, wu2, wd2, out_hbm,
                xbuf, ybuf, gsem, ssem):
    del elo_ref, ehi_ref
    t = pl.program_id(0)
    nv = nv_ref[t]

    def row_copy(j, gather):
        tok = src_ref[t * TM + j]
        if gather:
            return pltpu.make_async_copy(h2_hbm.at[pl.ds(tok, 1), :], xbuf.at[pl.ds(j, 1), :], gsem)
        dst = jnp.where(j < nv, tok, N_TOK + j)
        return pltpu.make_async_copy(ybuf.at[pl.ds(j, 1), :], out_hbm.at[pl.ds(dst, 1), :], ssem)

    def start_rows(gather):
        def body(j, c):
            row_copy(j, gather).start()
            return c
        lax.fori_loop(0, TM, body, 0)

    @pl.when(t == 0)
    def _():
        ybuf[...] = jnp.zeros_like(ybuf)
        spare = pltpu.make_async_copy(ybuf, out_hbm.at[pl.ds(N_TOK, TM), :], ssem)
        spare.start()
        spare.wait()

    @pl.when(nv > 0)
    def _():
        start_rows(True)
        pltpu.make_async_copy(h2_hbm.at[pl.ds(0, TM), :], xbuf, gsem).wait()
        x = xbuf[...].astype(BF16)
        gates = gate_ref[0]
        y = None
        for k, (wg, wu, wd) in enumerate(((wg1, wu1, wd1), (wg2, wu2, wd2))):
            a = _dot(x, wg[0, 0])
            hid = a * jax.nn.sigmoid(a) * _dot(x, wu[0, 0]) * gates[:, k:k + 1]
            part = _dot(hid.astype(BF16), wd[0, 0])
            y = part if y is None else y + part
        ybuf[...] = y
        start_rows(False)
        pltpu.make_async_copy(ybuf, out_hbm.at[pl.ds(0, TM), :], ssem).wait()


def _moe(l, h2, e_lo, e_hi, n_valid, src, gates, wts):
    up = lambda which: pl.BlockSpec((1, 1, D, D_EXP), lambda t, *pf: (l, pf[which][t], 0, 0))
    down = lambda which: pl.BlockSpec((1, 1, D_EXP, D), lambda t, *pf: (l, pf[which][t], 0, 0))
    first, second = 0, 1
    return pl.pallas_call(
        _moe_kernel,
        grid_spec=pltpu.PrefetchScalarGridSpec(
            num_scalar_prefetch=4,
            grid=(MOE_TILES,),
            in_specs=[pl.BlockSpec(memory_space=pl.ANY),
                      pl.BlockSpec((1, TM, 2), lambda t, elo, ehi, nv, src: (t, 0, 0)),
                      up(first), up(first), down(first), up(second), up(second), down(second)],
            out_specs=pl.BlockSpec(memory_space=pl.ANY),
            scratch_shapes=[pltpu.VMEM((TM, D), F32), pltpu.VMEM((TM, D), F32),
                            pltpu.SemaphoreType.DMA(()), pltpu.SemaphoreType.DMA(())],
        ),
        out_shape=jax.ShapeDtypeStruct((N_TOK + TM, D), F32),
        compiler_params=_params(("arbitrary",)),
        name="moe",
    )(e_lo, e_hi, n_valid, src, h2, gates,
      wts["w_gate"], wts["w_up"], wts["w_down"], wts["w_gate"], wts["w_up"], wts["w_down"])


def _moe_schedule(route):
    bucket = route[:, 0, :].reshape(N_TOK).astype(jnp.int32)
    gates = jnp.stack([route[:, 1, :].reshape(N_TOK), route[:, 2, :].reshape(N_TOK)], axis=-1)
    onehot = (bucket[:, None] == jnp.arange(N_BUCKET, dtype=jnp.int32)[None, :]).astype(jnp.int32)
    counts = onehot.sum(axis=0)
    rank = (jnp.cumsum(onehot, axis=0) * onehot).sum(axis=1) - 1
    tiles = (counts + TM - 1) // TM
    tile_end = jnp.cumsum(tiles)
    tile_start = tile_end - tiles
    slot = tile_start[bucket] * TM + rank
    src = jnp.zeros((MOE_TILES * TM,), jnp.int32).at[slot].set(jnp.arange(N_TOK, dtype=jnp.int32))
    gates_sorted = jnp.zeros((MOE_TILES * TM, 2), F32).at[slot].set(gates)
    t = jnp.arange(MOE_TILES, dtype=jnp.int32)
    total = tile_end[-1]
    bucket_of = lambda tile: (tile_end[None, :] <= tile[:, None]).astype(jnp.int32).sum(axis=1)
    tb = bucket_of(jnp.minimum(t, total - 1))
    n_valid = jnp.where(t < total, jnp.clip(counts[tb] - (t - tile_start[tb]) * TM, 0, TM), 0).astype(jnp.int32)
    grp = tb // N_PAIR
    e_lo = grp * EPG + jnp.asarray(_PAIR_LO)[tb % N_PAIR]
    e_hi = grp * EPG + jnp.asarray(_PAIR_HI)[tb % N_PAIR]
    return e_lo, e_hi, n_valid, src, gates_sorted.reshape(MOE_TILES, TM, 2)


def _final_kernel(x1_ref, moe_ref, mod_ref, g_ref, yc_ref, yl_ref):
    i = pl.program_id(0)
    y = _rms(x1_ref[...] + mod_ref[0, 5:6, :] * moe_ref[...], g_ref[...])

    @pl.when(i < CTX_TILES)
    def _():
        yc_ref[...] = y

    @pl.when(i >= CTX_TILES)
    def _():
        yl_ref[...] = y


def _final(x1, moe, mods, g):
    tok = pl.BlockSpec((TM, D), lambda i: (i, 0))
    l = DEPTH - 1
    return pl.pallas_call(
        _final_kernel,
        grid=(N_TILES,),
        in_specs=[tok, tok,
                  pl.BlockSpec((1, 6, D), lambda i: (l * N_MOD_ROWS + _mod_row(i), 0, 0)),
                  pl.BlockSpec((1, D), lambda i: (0, 0))],
        out_specs=[pl.BlockSpec((TM, D), lambda i: (_ctx_block(i), 0)),
                   pl.BlockSpec((TM, D), lambda i: (jnp.maximum(i - CTX_TILES, 0), 0))],
        out_shape=[jax.ShapeDtypeStruct((N_CTX, D), F32), jax.ShapeDtypeStruct((N_LAT, D), F32)],
        compiler_params=_params(("arbitrary",)),
        name="final",
    )(x1, moe, mods, g)


def _swap_halves(w):
    q = ROPE // 4
    return jnp.concatenate([-w[..., q:2 * q], w[..., :q], -w[..., 3 * q:], w[..., 2 * q:3 * q]], axis=-1)


def _pad_head(nope, rope):
    z = jnp.zeros(rope.shape[:-1] + (HP - NOPE - ROPE,), rope.dtype)
    return jnp.concatenate([nope, rope, z], axis=-1)


def _prepare_weights(norm1_g, norm2_g, w_in, q_norm_g, w_uq, kv_norm_g, w_ukv, w_out, w_router, b_router,
                     w_gate, w_up, w_down):
    o = 3 * D_NA + Q_LORA + KV_LORA
    w_kr = w_in[:, :, o:]
    zeros_nope = jnp.zeros((DEPTH, D, NOPE), F32)
    w_in_p = jnp.concatenate([w_in[:, :, :o], _pad_head(zeros_nope, w_kr),
                              _pad_head(zeros_nope, _swap_halves(w_kr))], axis=-1)
    wq = w_uq.reshape(DEPTH, Q_LORA, H_MLA, NOPE + ROPE)
    plain = _pad_head(wq[..., :NOPE], wq[..., NOPE:]).reshape(DEPTH, Q_LORA, D_QP)
    swapped = _pad_head(jnp.zeros_like(wq[..., :NOPE]), _swap_halves(wq[..., NOPE:])).reshape(DEPTH, Q_LORA, D_QP)
    wkv = w_ukv.reshape(DEPTH, KV_LORA, H_MLA, NOPE + VH)
    w_uk_p = jnp.concatenate([wkv[..., :NOPE], jnp.zeros((DEPTH, KV_LORA, H_MLA, HP - NOPE), F32)],
                             axis=-1).reshape(DEPTH, KV_LORA, D_QP)
    w_uv = wkv[..., NOPE:].reshape(DEPTH, KV_LORA, D_MLA)
    return {
        "n1": norm1_g.reshape(DEPTH, 1, D),
        "n2": norm2_g.reshape(DEPTH, 1, D),
        "w_in": w_in_p.astype(BF16),
        "qn": q_norm_g.reshape(DEPTH, 1, Q_LORA),
        "w_uq": jnp.concatenate([plain, swapped], axis=-1).astype(BF16),
        "kvn": kv_norm_g.reshape(DEPTH, 1, KV_LORA),
        "w_uk": w_uk_p.astype(BF16),
        "w_uv": w_uv.astype(BF16),
        "w_out": w_out.astype(BF16),
        "w_rt": w_router.T,
        "b_r": b_router.reshape(N_EXP, 1),
        "w_gate": w_gate.astype(BF16),
        "w_up": w_up.astype(BF16),
        "w_down": w_down.astype(BF16),
    }


def _rope_tables():
    half = ROPE // 2
    freqs = 1.0 / (ROPE_THETA ** (np.arange(0, half, 2, dtype=np.float32) / half))
    pos = np.arange(S_LAT)
    ang_r = (pos // GRID_W).astype(np.float32)[:, None] * freqs
    ang_c = (pos % GRID_W).astype(np.float32)[:, None] * freqs
    ang = np.concatenate([ang_r, ang_r, ang_c, ang_c], axis=1)
    cos = np.ones((S_CTX + S_LAT, LANE), np.float32)
    sin = np.zeros((S_CTX + S_LAT, LANE), np.float32)
    cos[S_CTX:, NOPE:NOPE + ROPE] = np.cos(ang)
    sin[S_CTX:, NOPE:NOPE + ROPE] = np.sin(ang)
    return jnp.asarray(cos), jnp.asarray(sin)


def kernel(x_prompt, x_sample, cache_nat_k, cache_nat_v, cache_mla_ckv, cache_mla_krope, c, c_ctx, norm1_g, norm2_g, w_ada, b_ada, w_in, q_norm_g, w_uq, kv_norm_g, w_ukv, rpb, w_out, w_router, b_router, w_gate, w_up, w_down, final_norm_g):
    wts = _prepare_weights(norm1_g, norm2_g, w_in, q_norm_g, w_uq, kv_norm_g, w_ukv, w_out, w_router, b_router,
                           w_gate, w_up, w_down)
    tabs = _rope_tables()
    cond = jnp.concatenate([c_ctx[None, :], c, jnp.zeros((N_MOD_ROWS - 1 - B_LAT, D), F32)], axis=0)
    mods = _ada(cond, w_ada, b_ada).reshape(DEPTH * N_MOD_ROWS, 6, D)
    place = jnp.asarray(np.eye(ROPE, HP, k=NOPE, dtype=np.float32)).astype(BF16)
    kc_all, vc_all = _ctxkv(cache_mla_ckv, cache_mla_krope, wts["w_uk"], wts["w_uv"], place)
    cache_k = cache_nat_k.reshape(B_LAT, DEPTH, PAST, D_NA)
    cache_v = cache_nat_v.reshape(B_LAT, DEPTH, PAST, D_NA)
    b_rows, b_cols, b_ok = _na_bias_index()

    x = jnp.concatenate([x_prompt.reshape(N_CTX, D), x_sample.reshape(N_LAT, D)], axis=0)
    x1 = moe = None
    new_k, new_v, new_ckv, new_kr = [], [], [], []
    for l in range(DEPTH):
        if l == 0:
            outs = _pre(l, True, x, mods, wts, tabs)
        else:
            outs = _pre(l, False, (x1, moe), mods, wts, tabs)
            x = outs[0]
            outs = outs[1:]
        qna, kna, vna, qp, kp, vm, k_new, v_new, ckv_new, kr_new = outs
        new_k.append(k_new)
        new_v.append(v_new)
        new_ckv.append(ckv_new)
        new_kr.append(kr_new)
        o_ctx = _ctx_attn(qna, kna, vna, qp, kp, vm)
        bias = jnp.where(b_ok[:, None], rpb[l][:, b_rows, b_cols].transpose(1, 0, 2, 3), -jnp.inf)
        o_na = _na_attn(l, qna, kna, vna, cache_k, cache_v, bias)
        o_mla = _mla_attn(l, qp, kp, vm, kc_all, vc_all)
        x1, h2, route = _post(l, x, o_ctx, o_na, o_mla, mods, wts)
        e_lo, e_hi, n_valid, src, gates = _moe_schedule(route)
        moe = _moe(l, h2, e_lo, e_hi, n_valid, src, gates, wts)
    y_ctx, y_lat = _final(x1, moe, mods, final_norm_g.reshape(1, D))

    stack = lambda parts, tail: jnp.stack([p.reshape((B_CTX, S_CTX) + tail) for p in parts], axis=1)
    return (y_ctx.reshape(B_CTX, S_CTX, D), y_lat.reshape(B_LAT, S_LAT, D),
            stack(new_k, (H_NA, DH_NA)), stack(new_v, (H_NA, DH_NA)),
            stack(new_ckv, (KV_LORA,)), stack(new_kr, (ROPE,)))
```

```python
import functools

import numpy as np
import jax
import jax.numpy as jnp
from jax import lax
from jax.experimental import pallas as pl
from jax.experimental.pallas import tpu as pltpu

F32 = jnp.float32
BF16 = jnp.bfloat16

D = 1024
B_CTX, S_CTX = 32, 256
B_LAT, S_LAT = 8, 2048
DEPTH = 4
PAST = 256
GRID_W, GRID_H = 64, 32
H_NA, DH_NA = 8, 64
WIN_H, WIN_W = 8, 16
H_MLA, NOPE, ROPE, VH = 8, 64, 32, 64
Q_LORA, KV_LORA = 256, 128
D_NA = H_NA * DH_NA
D_MLA = H_MLA * VH
N_EXP, N_GRP, EPG = 16, 4, 4
D_EXP = 512
ROPE_THETA = 10000.0
EPS = 1e-6
NA_SCALE = DH_NA ** -0.5
MLA_SCALE = (NOPE + ROPE) ** -0.5

N_CTX = B_CTX * S_CTX
N_LAT = B_LAT * S_LAT
N_TOK = N_CTX + N_LAT
TM = 256
CTX_TILES = N_CTX // TM
LAT_TILES = N_LAT // TM
N_TILES = CTX_TILES + LAT_TILES
LAT_TILES_PER_SEQ = S_LAT // TM
LANE = 128
HP = LANE
D_QP = H_MLA * HP
W_IN_COLS = 3 * D_NA + Q_LORA + KV_LORA + 2 * LANE
N_MOD_ROWS = 16
N_PAIR = 6
N_BUCKET = N_GRP * N_PAIR
MOE_TILES = N_TILES + N_BUCKET
BUCKET_ROWS = 32
D_ROW = D + LANE
NA_ROWS_PER_TILE = TM // GRID_W
NA_KEY_ROWS = 12
NA_KEYS = NA_KEY_ROWS * GRID_W
VMEM_LIMIT = 48 * 1024 * 1024
DMA_UNROLL = 8

_PAIR_LO = np.array([0, 0, 0, 1, 1, 2], np.int32)
_PAIR_HI = np.array([1, 2, 3, 2, 3, 3], np.int32)


def _dot(a, b):
    return jnp.dot(a, b, preferred_element_type=F32)


def _dot_nt(a, b):
    return lax.dot_general(a, b, (((1,), (1,)), ((), ())), preferred_element_type=F32)


def _rms(x, g):
    return x * lax.rsqrt(jnp.mean(x * x, axis=-1, keepdims=True) + EPS) * g


def _mod_row(i):
    return jnp.where(i < CTX_TILES, 0, 1 + (i - CTX_TILES) // LAT_TILES_PER_SEQ)


def _pos_block(i):
    return jnp.where(i < CTX_TILES, 0, 1 + (i - CTX_TILES) % LAT_TILES_PER_SEQ)


def _ctx_block(i):
    return jnp.minimum(i, CTX_TILES - 1)


def _params(sem):
    return pltpu.CompilerParams(dimension_semantics=sem, vmem_limit_bytes=VMEM_LIMIT)


def _ada_kernel(cond_ref, w_ref, b_ref, o_ref):
    cond = cond_ref[...]
    act = cond * jax.nn.sigmoid(cond)
    o_ref[0] = jnp.dot(act, w_ref[0], preferred_element_type=F32,
                       precision=lax.Precision.HIGHEST) + b_ref[0]


def _ada(cond, w_ada, b_ada):
    nt = 1536
    return pl.pallas_call(
        _ada_kernel,
        grid=(DEPTH, 6 * D // nt),
        in_specs=[pl.BlockSpec((N_MOD_ROWS, D), lambda l, j: (0, 0)),
                  pl.BlockSpec((1, D, nt), lambda l, j: (l, 0, j)),
                  pl.BlockSpec((1, 1, nt), lambda l, j: (l, 0, j))],
        out_specs=pl.BlockSpec((1, N_MOD_ROWS, nt), lambda l, j: (l, 0, j)),
        out_shape=jax.ShapeDtypeStruct((DEPTH, N_MOD_ROWS, 6 * D), F32),
        compiler_params=_params(("arbitrary", "arbitrary")),
        name="ada",
    )(cond, w_ada, b_ada.reshape(DEPTH, 1, 6 * D))


def _ctxkv_kernel(ckv_ref, kr_ref, wuk_ref, wuv_ref, place_ref, kc_ref, vc_ref):
    ckv = ckv_ref[0, 0].astype(BF16)
    kr = _dot(kr_ref[0, 0].astype(BF16), place_ref[...])
    kc_ref[0, 0] = (_dot(ckv, wuk_ref[0]) + jnp.tile(kr, (1, H_MLA))).astype(BF16)
    vc_ref[0, 0] = _dot(ckv, wuv_ref[0]).astype(BF16)


def _ctxkv(cache_ckv, cache_krope, w_uk_p, w_uv, place):
    return pl.pallas_call(
        _ctxkv_kernel,
        grid=(DEPTH, B_LAT),
        in_specs=[pl.BlockSpec((1, 1, PAST, KV_LORA), lambda l, b: (b, l, 0, 0)),
                  pl.BlockSpec((1, 1, PAST, ROPE), lambda l, b: (b, l, 0, 0)),
                  pl.BlockSpec((1, KV_LORA, D_QP), lambda l, b: (l, 0, 0)),
                  pl.BlockSpec((1, KV_LORA, D_MLA), lambda l, b: (l, 0, 0)),
                  pl.BlockSpec((ROPE, HP), lambda l, b: (0, 0))],
        out_specs=[pl.BlockSpec((1, 1, PAST, D_QP), lambda l, b: (l, b, 0, 0)),
                   pl.BlockSpec((1, 1, PAST, D_MLA), lambda l, b: (l, b, 0, 0))],
        out_shape=[jax.ShapeDtypeStruct((DEPTH, B_LAT, PAST, D_QP), BF16),
                   jax.ShapeDtypeStruct((DEPTH, B_LAT, PAST, D_MLA), BF16)],
        compiler_params=_params(("arbitrary", "arbitrary")),
        name="ctxkv",
    )(cache_ckv, cache_krope, w_uk_p, w_uv, place)


def _pre_kernel(first, *refs):
    if first:
        x_ref, *rest = refs
    else:
        x1_ref, moe_ref, modp_ref, *rest = refs
    (mod_ref, n1_ref, win_ref, qn_ref, wuq_ref, kvn_ref, wuk_ref, wuv_ref, cos_ref, sin_ref, *outs) = rest
    if not first:
        xo_ref, *outs = outs
    (qna_ref, kna_ref, vna_ref, qp_ref, kp_ref, vm_ref, knew_ref, vnew_ref, ckvnew_ref, krnew_ref) = outs
    i = pl.program_id(0)

    if first:
        x = x_ref[...]
    else:
        x = x1_ref[...] + modp_ref[0, 5:6, :] * moe_ref[...]
        xo_ref[...] = x
    m = mod_ref[0]
    h = _rms(x, n1_ref[0]) * (1.0 + m[1:2, :]) + m[0:1, :]
    proj = _dot(h.astype(BF16), win_ref[0])
    k_na = proj[:, D_NA:2 * D_NA]
    v_na = proj[:, 2 * D_NA:3 * D_NA]
    qna_ref[...] = (proj[:, :D_NA] * NA_SCALE).astype(BF16)
    kna_ref[...] = k_na.astype(BF16)
    vna_ref[...] = v_na.astype(BF16)
    o = 3 * D_NA
    cq = proj[:, o:o + Q_LORA]
    ckv = proj[:, o + Q_LORA:o + Q_LORA + KV_LORA]
    kr = proj[:, o + Q_LORA + KV_LORA:o + Q_LORA + KV_LORA + LANE]
    kr_sw = proj[:, o + Q_LORA + KV_LORA + LANE:]
    cos = cos_ref[...]
    sin = sin_ref[...]

    cqn = _rms(cq, qn_ref[0]).astype(BF16)
    q2 = _dot(cqn, wuq_ref[0])
    qp = q2[:, :D_QP] * jnp.tile(cos, (1, H_MLA)) + q2[:, D_QP:] * jnp.tile(sin, (1, H_MLA))
    qp_ref[...] = qp.astype(BF16)

    ckvn = _rms(ckv, kvn_ref[0])
    ckvb = ckvn.astype(BF16)
    kr_rot = kr * cos + kr_sw * sin
    kp_ref[...] = (_dot(ckvb, wuk_ref[0]) + jnp.tile(kr_rot, (1, H_MLA))).astype(BF16)
    vm_ref[...] = _dot(ckvb, wuv_ref[0]).astype(BF16)

    @pl.when(i < CTX_TILES)
    def _():
        knew_ref[...] = k_na
        vnew_ref[...] = v_na
        ckvnew_ref[...] = ckvn
        krnew_ref[...] = kr[:, NOPE:NOPE + ROPE]


def _pre(l, first, xin, mods, wts, tabs):
    tok = lambda n: pl.BlockSpec((TM, n), lambda i: (i, 0))
    ctx = lambda n: pl.BlockSpec((TM, n), lambda i: (_ctx_block(i), 0))
    lay = lambda a: pl.BlockSpec((1,) + a.shape[1:], lambda i: (l,) + (0,) * (a.ndim - 1))
    in_specs = []
    args = []
    if first:
        in_specs.append(tok(D))
        args.append(xin)
    else:
        x1, moe = xin
        in_specs += [tok(D), tok(D), pl.BlockSpec((1, 6, D), lambda i: ((l - 1) * N_MOD_ROWS + _mod_row(i), 0, 0))]
        args += [x1, moe, mods]
    in_specs.append(pl.BlockSpec((1, 6, D), lambda i: (l * N_MOD_ROWS + _mod_row(i), 0, 0)))
    args.append(mods)
    for name in ("n1", "w_in", "qn", "w_uq", "kvn", "w_uk", "w_uv"):
        in_specs.append(lay(wts[name]))
        args.append(wts[name])
    for t in tabs:
        in_specs.append(pl.BlockSpec((TM, LANE), lambda i: (_pos_block(i), 0)))
        args.append(t)
    out_specs = [tok(D), tok(D_NA), tok(D_NA), tok(D_NA), tok(D_QP), tok(D_QP), tok(D_MLA),
                 ctx(D_NA), ctx(D_NA), ctx(KV_LORA), ctx(ROPE)]
    out_shape = [jax.ShapeDtypeStruct((N_TOK, D), F32),
                 jax.ShapeDtypeStruct((N_TOK, D_NA), BF16),
                 jax.ShapeDtypeStruct((N_TOK, D_NA), BF16),
                 jax.ShapeDtypeStruct((N_TOK, D_NA), BF16),
                 jax.ShapeDtypeStruct((N_TOK, D_QP), BF16),
                 jax.ShapeDtypeStruct((N_TOK, D_QP), BF16),
                 jax.ShapeDtypeStruct((N_TOK, D_MLA), BF16),
                 jax.ShapeDtypeStruct((N_CTX, D_NA), F32),
                 jax.ShapeDtypeStruct((N_CTX, D_NA), F32),
                 jax.ShapeDtypeStruct((N_CTX, KV_LORA), F32),
                 jax.ShapeDtypeStruct((N_CTX, ROPE), F32)]
    if first:
        out_specs = out_specs[1:]
        out_shape = out_shape[1:]
    return pl.pallas_call(
        functools.partial(_pre_kernel, first),
        grid=(N_TILES,),
        in_specs=in_specs,
        out_specs=out_specs,
        out_shape=out_shape,
        compiler_params=_params(("arbitrary",)),
        name="pre",
    )(*args)


def _softmax_parts(parts):
    m = parts[0].max(axis=-1, keepdims=True)
    for s in parts[1:]:
        m = jnp.maximum(m, s.max(axis=-1, keepdims=True))
    ps = [jnp.exp(s - m) for s in parts]
    den = ps[0].sum(axis=-1, keepdims=True)
    for p in ps[1:]:
        den = den + p.sum(axis=-1, keepdims=True)
    return ps, den


def _low_lanes():
    return lax.broadcasted_iota(jnp.int32, (TM, LANE), 1) < DH_NA


def _split_pair(qpair, low):
    zero = jnp.zeros_like(qpair)
    return jnp.where(low, qpair, zero), jnp.where(low, zero, qpair)


def _ctx_attn_kernel(qna_ref, kna_ref, vna_ref, qp_ref, kp_ref, vm_ref, o_ref):
    low = _low_lanes()
    outs = []
    for hp in range(H_NA // 2):
        sl = slice(hp * LANE, (hp + 1) * LANE)
        kpair = kna_ref[:, sl]
        vpair = vna_ref[:, sl]
        res = []
        for qh in _split_pair(qna_ref[:, sl], low):
            (p,), den = _softmax_parts([_dot_nt(qh, kpair)])
            res.append(_dot(p.astype(BF16), vpair) / den)
        outs.append(jnp.where(low, res[0], res[1]))
    for hp in range(H_MLA // 2):
        vpair = vm_ref[:, hp * LANE:(hp + 1) * LANE]
        res = []
        for h in (2 * hp, 2 * hp + 1):
            sl = slice(h * HP, (h + 1) * HP)
            (p,), den = _softmax_parts([_dot_nt(qp_ref[:, sl], kp_ref[:, sl]) * MLA_SCALE])
            res.append(_dot(p.astype(BF16), vpair) / den)
        outs.append(jnp.where(low, res[0], res[1]))
    o_ref[...] = jnp.concatenate(outs, axis=1).astype(BF16)


def _ctx_attn(qna, kna, vna, qp, kp, vm):
    tok = lambda n: pl.BlockSpec((S_CTX, n), lambda i: (i, 0))
    return pl.pallas_call(
        _ctx_attn_kernel,
        grid=(B_CTX,),
        in_specs=[tok(D_NA), tok(D_NA), tok(D_NA), tok(D_QP), tok(D_QP), tok(D_MLA)],
        out_specs=tok(D_NA + D_MLA),
        out_shape=jax.ShapeDtypeStruct((N_CTX, D_NA + D_MLA), BF16),
        compiler_params=_params(("arbitrary",)),
        name="ctx_attn",
    )(qna, kna, vna, qp, kp, vm)


def _na_kernel(q_ref, k_ref, v_ref, kc_ref, vc_ref, bias_ref, o_ref):
    t = pl.program_id(1)
    row0 = jnp.clip(t * NA_ROWS_PER_TILE - WIN_H // 2, 0, GRID_H - NA_KEY_ROWS)
    ks = pl.multiple_of(row0 * GRID_W, GRID_W)
    low = _low_lanes()
    outs = []
    for hp in range(H_NA // 2):
        sl = slice(hp * LANE, (hp + 1) * LANE)
        kwin = k_ref[pl.ds(ks, NA_KEYS), sl]
        vwin = v_ref[pl.ds(ks, NA_KEYS), sl]
        kctx = kc_ref[0, 0, :, sl].astype(BF16)
        vctx = vc_ref[0, 0, :, sl].astype(BF16)
        res = []
        for half, qh in enumerate(_split_pair(q_ref[:, sl], low)):
            s_win = _dot_nt(qh, kwin) + bias_ref[0, 0, 2 * hp + half]
            s_ctx = _dot_nt(qh, kctx)
            (p_win, p_ctx), den = _softmax_parts([s_win, s_ctx])
            o = _dot(p_win.astype(BF16), vwin) + _dot(p_ctx.astype(BF16), vctx)
            res.append(o / den)
        outs.append(jnp.where(low, res[0], res[1]))
    o_ref[...] = jnp.concatenate(outs, axis=1).astype(BF16)


def _na_attn(l, qna, kna, vna, cache_k, cache_v, bias):
    nt = LAT_TILES_PER_SEQ
    seq = lambda n: pl.BlockSpec((S_LAT, n), lambda b, t: (N_CTX // S_LAT + b, 0))
    cache = pl.BlockSpec((1, 1, PAST, D_NA), lambda b, t: (b, l, 0, 0))
    kind = lambda t: jnp.where(t == 0, 0, jnp.where(t == nt - 1, 2, 1))
    return pl.pallas_call(
        _na_kernel,
        grid=(B_LAT, nt),
        in_specs=[pl.BlockSpec((TM, D_NA), lambda b, t: (CTX_TILES + b * nt + t, 0)),
                  seq(D_NA), seq(D_NA), cache, cache,
                  pl.BlockSpec((1, 1, H_NA, TM, NA_KEYS), lambda b, t: (l, kind(t), 0, 0, 0))],
        out_specs=pl.BlockSpec((TM, D_NA), lambda b, t: (b * nt + t, 0)),
        out_shape=jax.ShapeDtypeStruct((N_LAT, D_NA), BF16),
        compiler_params=_params(("arbitrary", "arbitrary")),
        name="na_attn",
    )(qna, kna, vna, cache_k, cache_v, bias)


def _na_bias_tables(rpb):
    n_dr, n_dc = 2 * WIN_H - 1, 2 * WIN_W - 1
    qc = np.arange(GRID_W)[:, None]
    kc = np.arange(GRID_W)[None, :]
    ws = np.clip(qc - WIN_W // 2, 0, GRID_W - WIN_W)
    col_ok = (kc >= ws) & (kc < ws + WIN_W)
    col_hot = ((kc - qc + WIN_W - 1)[..., None] == np.arange(n_dc)) & col_ok[..., None]
    col_hot = col_hot.reshape(GRID_W * GRID_W, n_dc).astype(np.float32)
    kinds = ((0, 0), (NA_ROWS_PER_TILE, 0), (GRID_H - NA_ROWS_PER_TILE, GRID_H - NA_KEY_ROWS))
    row_hot = np.zeros((len(kinds), NA_ROWS_PER_TILE, NA_KEY_ROWS, n_dr), np.float32)
    row_ok = np.zeros((len(kinds), NA_ROWS_PER_TILE, NA_KEY_ROWS), bool)
    for kind, (r0, k0) in enumerate(kinds):
        for i in range(NA_ROWS_PER_TILE):
            r = r0 + i
            rs = min(max(r - WIN_H // 2, 0), GRID_H - WIN_H)
            for j in range(NA_KEY_ROWS):
                kr = k0 + j
                if rs <= kr < rs + WIN_H:
                    row_ok[kind, i, j] = True
                    row_hot[kind, i, j, kr - r + WIN_H - 1] = 1.0
    hi = lax.Precision.HIGHEST
    by_col = jnp.einsum("pb,lhab->lhap", jnp.asarray(col_hot), rpb, precision=hi)
    vals = jnp.einsum("tija,lhap->ltihjp", jnp.asarray(row_hot), by_col, precision=hi)
    vals = vals.reshape(DEPTH, len(kinds), NA_ROWS_PER_TILE, H_NA, NA_KEY_ROWS, GRID_W, GRID_W)
    ok = row_ok[None, :, :, None, :, None, None] & col_ok[None, None, None, None, None, :, :]
    bias = jnp.where(jnp.asarray(ok), vals, -jnp.inf)
    return bias.transpose(0, 1, 3, 2, 5, 4, 6).reshape(DEPTH, len(kinds), H_NA, TM, NA_KEYS)


def _mla_kernel(q_ref, k_ref, v_ref, kc_ref, vc_ref, o_ref):
    low = _low_lanes()
    outs = []
    for hp in range(H_MLA // 2):
        vsl = slice(hp * LANE, (hp + 1) * LANE)
        vlat = v_ref[:, vsl]
        vctx = vc_ref[0, 0, :, vsl]
        res = []
        for h in (2 * hp, 2 * hp + 1):
            sl = slice(h * HP, (h + 1) * HP)
            q = q_ref[:, sl]
            s_ctx = _dot_nt(q, kc_ref[0, 0, :, sl]) * MLA_SCALE
            s_lat = _dot_nt(q, k_ref[:, sl]) * MLA_SCALE
            (p_ctx, p_lat), den = _softmax_parts([s_ctx, s_lat])
            o = _dot(p_ctx.astype(BF16), vctx) + _dot(p_lat.astype(BF16), vlat)
            res.append(o / den)
        outs.append(jnp.where(low, res[0], res[1]))
    o_ref[...] = jnp.concatenate(outs, axis=1).astype(BF16)


def _mla_attn(l, qp, kp, vm, kc, vc):
    nt = LAT_TILES_PER_SEQ
    seq = lambda n: pl.BlockSpec((S_LAT, n), lambda b, t: (N_CTX // S_LAT + b, 0))
    return pl.pallas_call(
        _mla_kernel,
        grid=(B_LAT, nt),
        in_specs=[pl.BlockSpec((TM, D_QP), lambda b, t: (CTX_TILES + b * nt + t, 0)),
                  seq(D_QP), seq(D_MLA),
                  pl.BlockSpec((1, 1, PAST, D_QP), lambda b, t: (l, b, 0, 0)),
                  pl.BlockSpec((1, 1, PAST, D_MLA), lambda b, t: (l, b, 0, 0))],
        out_specs=pl.BlockSpec((TM, D_MLA), lambda b, t: (b * nt + t, 0)),
        out_shape=jax.ShapeDtypeStruct((N_LAT, D_MLA), BF16),
        compiler_params=_params(("arbitrary", "arbitrary")),
        name="mla_attn",
    )(qp, kp, vm, kc, vc)


def _route(sel, aff):
    rows = lambda a, g: [a[EPG * g + e:EPG * g + e + 1, :] for e in range(EPG)]
    best = None
    for g in range(N_GRP):
        v = rows(sel, g)
        score = v[0] + v[1]
        for a, b in ((0, 2), (0, 3), (1, 2), (1, 3), (2, 3)):
            score = jnp.maximum(score, v[a] + v[b])
        if best is None:
            best, grp = score, jnp.zeros_like(score)
        else:
            better = score > best
            best = jnp.where(better, score, best)
            grp = jnp.where(better, float(g), grp)
    zero = jnp.zeros_like(best)
    sv = [zero] * EPG
    av = [zero] * EPG
    for g in range(N_GRP):
        hit = grp == float(g)
        sg, ag = rows(sel, g), rows(aff, g)
        sv = [jnp.where(hit, sg[e], sv[e]) for e in range(EPG)]
        av = [jnp.where(hit, ag[e], av[e]) for e in range(EPG)]

    def argmax_first(vals, skip=None):
        bv, bi = None, None
        for e in range(EPG):
            v = vals[e] if skip is None else jnp.where(skip == float(e), -jnp.inf, vals[e])
            if bv is None:
                bv, bi = v, jnp.zeros_like(v)
            else:
                better = v > bv
                bv = jnp.where(better, v, bv)
                bi = jnp.where(better, float(e), bi)
        return bi

    i1 = argmax_first(sv)
    i2 = argmax_first(sv, skip=i1)
    lo = jnp.minimum(i1, i2)
    hi = jnp.maximum(i1, i2)
    pick = lambda idx: sum(jnp.where(idx == float(e), av[e], 0.0) for e in range(EPG))
    a_lo, a_hi = pick(lo), pick(hi)
    den = a_lo + a_hi
    pair = jnp.where(lo == 0.0, 0.0, jnp.where(lo == 1.0, 3.0, 5.0)) + hi - lo - 1.0
    return grp * float(N_PAIR) + pair, a_lo / den, a_hi / den


def _post_kernel(x_ref, octx_ref, ona_ref, omla_ref, mod_ref, n2_ref, wout_ref, wr_ref, br_ref,
                 x1_ref, h2_ref, route_ref, counts_ref, carry):
    i = pl.program_id(0)

    @pl.when(i == 0)
    def _():
        carry[...] = jnp.zeros_like(carry)

    is_ctx = i < CTX_TILES
    o_lat = jnp.concatenate([ona_ref[...], omla_ref[...]], axis=1)
    o = jnp.where(is_ctx, octx_ref[...], o_lat)
    m = mod_ref[0]
    x1 = x_ref[...] + m[2:3, :] * _dot(o, wout_ref[0])
    x1_ref[...] = x1
    h2 = _rms(x1, n2_ref[0]) * (1.0 + m[4:5, :]) + m[3:4, :]
    logits = lax.dot_general(wr_ref[...], h2, (((1,), (1,)), ((), ())),
                             preferred_element_type=F32, precision=lax.Precision.HIGHEST)
    aff = jax.nn.sigmoid(logits)
    bucket, g_lo, g_hi = _route(aff + br_ref[...], aff)

    hit = lax.broadcasted_iota(jnp.int32, (BUCKET_ROWS, TM), 0).astype(F32) == bucket
    earlier = lax.broadcasted_iota(jnp.int32, (TM, TM), 0) <= lax.broadcasted_iota(jnp.int32, (TM, TM), 1)
    incl = _dot(jnp.where(hit, 1.0, 0.0).astype(BF16), jnp.where(earlier, 1.0, 0.0).astype(BF16))
    base = carry[...]
    rank = jnp.sum(jnp.where(hit, incl + base[:, 0:1], 0.0), axis=0, keepdims=True) - 1.0
    carry[...] = base + incl[:, TM - 1:TM]
    counts_ref[...] = carry[...]
    route_ref[0] = jnp.concatenate([bucket, rank, jnp.zeros((6, TM), F32)], axis=0)

    gate_cols = jnp.transpose(jnp.concatenate([g_lo, g_hi, jnp.zeros((6, TM), F32)], axis=0))
    h2_ref[:, :D] = h2
    h2_ref[:, D:] = jnp.concatenate([gate_cols, jnp.zeros((TM, LANE - 8), F32)], axis=1)


def _post(l, x, o_ctx, o_na, o_mla, mods, wts):
    tok = lambda n: pl.BlockSpec((TM, n), lambda i: (i, 0))
    lat = lambda n: pl.BlockSpec((TM, n), lambda i: (jnp.maximum(i - CTX_TILES, 0), 0))
    lay = lambda a: pl.BlockSpec((1,) + a.shape[1:], lambda i: (l,) + (0,) * (a.ndim - 1))
    full = lambda a: pl.BlockSpec(a.shape, lambda i: (0,) * a.ndim)
    return pl.pallas_call(
        _post_kernel,
        grid=(N_TILES,),
        in_specs=[tok(D), pl.BlockSpec((TM, D), lambda i: (_ctx_block(i), 0)), lat(D_NA), lat(D_MLA),
                  pl.BlockSpec((1, 6, D), lambda i: (l * N_MOD_ROWS + _mod_row(i), 0, 0)),
                  lay(wts["n2"]), lay(wts["w_out"]), full(wts["w_rt"]), full(wts["b_r"])],
        out_specs=[tok(D), tok(D_ROW), pl.BlockSpec((1, 8, TM), lambda i: (i, 0, 0)),
                   pl.BlockSpec((BUCKET_ROWS, LANE), lambda i: (0, 0))],
        out_shape=[jax.ShapeDtypeStruct((N_TOK, D), F32),
                   jax.ShapeDtypeStruct((N_TOK, D_ROW), F32),
                   jax.ShapeDtypeStruct((N_TILES, 8, TM), F32),
                   jax.ShapeDtypeStruct((BUCKET_ROWS, LANE), F32)],
        scratch_shapes=[pltpu.VMEM((BUCKET_ROWS, LANE), F32)],
        compiler_params=_params(("arbitrary",)),
        name="post",
    )(x, o_ctx, o_na, o_mla, mods, wts["n2"], wts["w_out"], wts["w_rt"], wts["b_r"])


def _moe_kernel(elo_ref, ehi_ref, nv_ref, total_ref, slot_ref,
                h2_hbm, wg1, wu1, wd1, wg2, wu2, wd2, out_hbm,
                xbuf, ybuf, src, gsem, ssem):
    del elo_ref, ehi_ref
    t = pl.program_id(0)
    total = total_ref[0]
    par = t % 2

    def row_token(tile, j):
        return src[tile * TM + jnp.where(j < nv_ref[tile], j, 0)]

    def start_gather(tile, p):
        def body(j, c):
            pltpu.make_async_copy(h2_hbm.at[pl.ds(row_token(tile, j), 1), :],
                                  xbuf.at[p, pl.ds(j, 1), :], gsem.at[p]).start()
            return c
        lax.fori_loop(0, TM, body, 0, unroll=DMA_UNROLL)

    def start_scatter(tile, p):
        def body(j, c):
            dst = jnp.where(j < nv_ref[tile], row_token(tile, j), N_TOK + j)
            pltpu.make_async_copy(ybuf.at[p, pl.ds(j, 1), :],
                                  out_hbm.at[pl.ds(dst, 1), :], ssem.at[p]).start()
            return c
        lax.fori_loop(0, TM, body, 0, unroll=DMA_UNROLL)

    def wait_gather(p):
        pltpu.make_async_copy(h2_hbm.at[pl.ds(0, TM), :], xbuf.at[p], gsem.at[p]).wait()

    def wait_scatter(p):
        pltpu.make_async_copy(ybuf.at[p], out_hbm.at[pl.ds(0, TM), :], ssem.at[p]).wait()

    @pl.when(t == 0)
    def _():
        def invert(i, c):
            src[slot_ref[i]] = i
            return c
        lax.fori_loop(0, N_TOK, invert, 0, unroll=DMA_UNROLL)
        ybuf[0] = jnp.zeros((TM, D), F32)
        spare = pltpu.make_async_copy(ybuf.at[0], out_hbm.at[pl.ds(N_TOK, TM), :], ssem.at[0])
        spare.start()
        spare.wait()
        start_gather(0, 0)

    @pl.when(t + 1 < total)
    def _():
        start_gather(t + 1, 1 - par)

    @pl.when(t < total)
    def _():
        wait_gather(par)

        @pl.when(t >= 2)
        def _():
            wait_scatter(par)

        xg = xbuf[par]
        x = xg[:, :D].astype(BF16)
        gates = xg[:, D:]
        y = None
        for k, (wg, wu, wd) in enumerate(((wg1, wu1, wd1), (wg2, wu2, wd2))):
            a = _dot(x, wg[0, 0])
            hid = a * jax.nn.sigmoid(a) * _dot(x, wu[0, 0]) * gates[:, k:k + 1]
            part = _dot(hid.astype(BF16), wd[0, 0])
            y = part if y is None else y + part
        ybuf[par] = y
        start_scatter(t, par)

        @pl.when(t == total - 1)
        def _():
            @pl.when(t >= 1)
            def _():
                wait_scatter(1 - par)
            wait_scatter(par)


def _moe(l, h2, e_lo, e_hi, n_valid, total, slot, wts):
    up = lambda which: pl.BlockSpec((1, 1, D, D_EXP), lambda t, *pf: (l, pf[which][t], 0, 0))
    down = lambda which: pl.BlockSpec((1, 1, D_EXP, D), lambda t, *pf: (l, pf[which][t], 0, 0))
    first, second = 0, 1
    return pl.pallas_call(
        _moe_kernel,
        grid_spec=pltpu.PrefetchScalarGridSpec(
            num_scalar_prefetch=5,
            grid=(MOE_TILES,),
            in_specs=[pl.BlockSpec(memory_space=pl.ANY),
                      up(first), up(first), down(first), up(second), up(second), down(second)],
            out_specs=pl.BlockSpec(memory_space=pl.ANY),
            scratch_shapes=[pltpu.VMEM((2, TM, D_ROW), F32), pltpu.VMEM((2, TM, D), F32),
                            pltpu.SMEM((MOE_TILES * TM,), jnp.int32),
                            pltpu.SemaphoreType.DMA((2,)), pltpu.SemaphoreType.DMA((2,))],
        ),
        out_shape=jax.ShapeDtypeStruct((N_TOK + TM, D), F32),
        compiler_params=_params(("arbitrary",)),
        name="moe",
    )(e_lo, e_hi, n_valid, total, slot, h2,
      wts["w_gate"], wts["w_up"], wts["w_down"], wts["w_gate"], wts["w_up"], wts["w_down"])


def _moe_schedule(route, counts):
    bucket = route[:, 0, :].reshape(N_TOK).astype(jnp.int32)
    rank = route[:, 1, :].reshape(N_TOK).astype(jnp.int32)
    counts = counts[:N_BUCKET, 0].astype(jnp.int32)
    tiles = (counts + TM - 1) // TM
    tile_end = jnp.cumsum(tiles)
    tile_start = tile_end - tiles
    buckets = jnp.arange(N_BUCKET, dtype=jnp.int32)
    first_slot = jnp.where(bucket[:, None] == buckets[None, :], (tile_start * TM)[None, :], 0).sum(axis=1)
    slot = first_slot + rank
    t = jnp.arange(MOE_TILES, dtype=jnp.int32)
    total = tile_end[-1]
    bucket_of = lambda tile: (tile_end[None, :] <= tile[:, None]).astype(jnp.int32).sum(axis=1)
    tb = bucket_of(jnp.minimum(t, total - 1))
    n_valid = jnp.where(t < total, jnp.clip(counts[tb] - (t - tile_start[tb]) * TM, 0, TM), 0).astype(jnp.int32)
    grp = tb // N_PAIR
    e_lo = grp * EPG + jnp.asarray(_PAIR_LO)[tb % N_PAIR]
    e_hi = grp * EPG + jnp.asarray(_PAIR_HI)[tb % N_PAIR]
    return e_lo, e_hi, n_valid, total.reshape(1), slot


def _final_kernel(x1_ref, moe_ref, mod_ref, g_ref, yc_ref, yl_ref):
    i = pl.program_id(0)
    y = _rms(x1_ref[...] + mod_ref[0, 5:6, :] * moe_ref[...], g_ref[...])

    @pl.when(i < CTX_TILES)
    def _():
        yc_ref[...] = y

    @pl.when(i >= CTX_TILES)
    def _():
        yl_ref[...] = y


def _final(x1, moe, mods, g):
    tok = pl.BlockSpec((TM, D), lambda i: (i, 0))
    l = DEPTH - 1
    return pl.pallas_call(
        _final_kernel,
        grid=(N_TILES,),
        in_specs=[tok, tok,
                  pl.BlockSpec((1, 6, D), lambda i: (l * N_MOD_ROWS + _mod_row(i), 0, 0)),
                  pl.BlockSpec((1, D), lambda i: (0, 0))],
        out_specs=[pl.BlockSpec((TM, D), lambda i: (_ctx_block(i), 0)),
                   pl.BlockSpec((TM, D), lambda i: (jnp.maximum(i - CTX_TILES, 0), 0))],
        out_shape=[jax.ShapeDtypeStruct((N_CTX, D), F32), jax.ShapeDtypeStruct((N_LAT, D), F32)],
        compiler_params=_params(("arbitrary",)),
        name="final",
    )(x1, moe, mods, g)


def _swap_halves(w):
    q = ROPE // 4
    return jnp.concatenate([-w[..., q:2 * q], w[..., :q], -w[..., 3 * q:], w[..., 2 * q:3 * q]], axis=-1)


def _pad_head(nope, rope):
    z = jnp.zeros(rope.shape[:-1] + (HP - NOPE - ROPE,), rope.dtype)
    return jnp.concatenate([nope, rope, z], axis=-1)


def _prepare_weights(norm1_g, norm2_g, w_in, q_norm_g, w_uq, kv_norm_g, w_ukv, w_out, w_router, b_router,
                     w_gate, w_up, w_down):
    o = 3 * D_NA + Q_LORA + KV_LORA
    w_kr = w_in[:, :, o:]
    zeros_nope = jnp.zeros((DEPTH, D, NOPE), F32)
    w_in_p = jnp.concatenate([w_in[:, :, :o], _pad_head(zeros_nope, w_kr),
                              _pad_head(zeros_nope, _swap_halves(w_kr))], axis=-1)
    wq = w_uq.reshape(DEPTH, Q_LORA, H_MLA, NOPE + ROPE)
    plain = _pad_head(wq[..., :NOPE], wq[..., NOPE:]).reshape(DEPTH, Q_LORA, D_QP)
    swapped = _pad_head(jnp.zeros_like(wq[..., :NOPE]), _swap_halves(wq[..., NOPE:])).reshape(DEPTH, Q_LORA, D_QP)
    wkv = w_ukv.reshape(DEPTH, KV_LORA, H_MLA, NOPE + VH)
    w_uk_p = jnp.concatenate([wkv[..., :NOPE], jnp.zeros((DEPTH, KV_LORA, H_MLA, HP - NOPE), F32)],
                             axis=-1).reshape(DEPTH, KV_LORA, D_QP)
    w_uv = wkv[..., NOPE:].reshape(DEPTH, KV_LORA, D_MLA)
    return {
        "n1": norm1_g.reshape(DEPTH, 1, D),
        "n2": norm2_g.reshape(DEPTH, 1, D),
        "w_in": w_in_p.astype(BF16),
        "qn": q_norm_g.reshape(DEPTH, 1, Q_LORA),
        "w_uq": jnp.concatenate([plain, swapped], axis=-1).astype(BF16),
        "kvn": kv_norm_g.reshape(DEPTH, 1, KV_LORA),
        "w_uk": w_uk_p.astype(BF16),
        "w_uv": w_uv.astype(BF16),
        "w_out": w_out.astype(BF16),
        "w_rt": w_router.T,
        "b_r": b_router.reshape(N_EXP, 1),
        "w_gate": w_gate.astype(BF16),
        "w_up": w_up.astype(BF16),
        "w_down": w_down.astype(BF16),
    }


def _rope_tables():
    half = ROPE // 2
    freqs = 1.0 / (ROPE_THETA ** (np.arange(0, half, 2, dtype=np.float32) / half))
    pos = np.arange(S_LAT)
    ang_r = (pos // GRID_W).astype(np.float32)[:, None] * freqs
    ang_c = (pos % GRID_W).astype(np.float32)[:, None] * freqs
    ang = np.concatenate([ang_r, ang_r, ang_c, ang_c], axis=1)
    cos = np.ones((S_CTX + S_LAT, LANE), np.float32)
    sin = np.zeros((S_CTX + S_LAT, LANE), np.float32)
    cos[S_CTX:, NOPE:NOPE + ROPE] = np.cos(ang)
    sin[S_CTX:, NOPE:NOPE + ROPE] = np.sin(ang)
    return jnp.asarray(cos), jnp.asarray(sin)


def kernel(x_prompt, x_sample, cache_nat_k, cache_nat_v, cache_mla_ckv, cache_mla_krope, c, c_ctx, norm1_g, norm2_g, w_ada, b_ada, w_in, q_norm_g, w_uq, kv_norm_g, w_ukv, rpb, w_out, w_router, b_router, w_gate, w_up, w_down, final_norm_g):
    wts = _prepare_weights(norm1_g, norm2_g, w_in, q_norm_g, w_uq, kv_norm_g, w_ukv, w_out, w_router, b_router,
                           w_gate, w_up, w_down)
    tabs = _rope_tables()
    cond = jnp.concatenate([c_ctx[None, :], c, jnp.zeros((N_MOD_ROWS - 1 - B_LAT, D), F32)], axis=0)
    mods = _ada(cond, w_ada, b_ada).reshape(DEPTH * N_MOD_ROWS, 6, D)
    place = jnp.asarray(np.eye(ROPE, HP, k=NOPE, dtype=np.float32)).astype(BF16)
    kc_all, vc_all = _ctxkv(cache_mla_ckv, cache_mla_krope, wts["w_uk"], wts["w_uv"], place)
    cache_k = cache_nat_k.reshape(B_LAT, DEPTH, PAST, D_NA)
    cache_v = cache_nat_v.reshape(B_LAT, DEPTH, PAST, D_NA)
    bias = _na_bias_tables(rpb)

    x = jnp.concatenate([x_prompt.reshape(N_CTX, D), x_sample.reshape(N_LAT, D)], axis=0)
    x1 = moe = None
    new_k, new_v, new_ckv, new_kr = [], [], [], []
    for l in range(DEPTH):
        if l == 0:
            outs = _pre(l, True, x, mods, wts, tabs)
        else:
            outs = _pre(l, False, (x1, moe), mods, wts, tabs)
            x = outs[0]
            outs = outs[1:]
        qna, kna, vna, qp, kp, vm, k_new, v_new, ckv_new, kr_new = outs
        new_k.append(k_new)
        new_v.append(v_new)
        new_ckv.append(ckv_new)
        new_kr.append(kr_new)
        o_ctx = _ctx_attn(qna, kna, vna, qp, kp, vm)
        o_na = _na_attn(l, qna, kna, vna, cache_k, cache_v, bias)
        o_mla = _mla_attn(l, qp, kp, vm, kc_all, vc_all)
        x1, h2, route, counts = _post(l, x, o_ctx, o_na, o_mla, mods, wts)
        moe = _moe(l, h2, *_moe_schedule(route, counts), wts)
    y_ctx, y_lat = _final(x1, moe, mods, final_norm_g.reshape(1, D))

    stack = lambda parts, tail: jnp.stack([p.reshape((B_CTX, S_CTX) + tail) for p in parts], axis=1)
    return (y_ctx.reshape(B_CTX, S_CTX, D), y_lat.reshape(B_LAT, S_LAT, D),
            stack(new_k, (H_NA, DH_NA)), stack(new_v, (H_NA, DH_NA)),
            stack(new_ckv, (KV_LORA,)), stack(new_kr, (ROPE,)))
```

```python
import functools

import numpy as np
import jax
import jax.numpy as jnp
from jax import lax
from jax.experimental import pallas as pl
from jax.experimental.pallas import tpu as pltpu

F32 = jnp.float32
BF16 = jnp.bfloat16

D = 1024
B_CTX, S_CTX = 32, 256
B_LAT, S_LAT = 8, 2048
DEPTH = 4
PAST = 256
GRID_W, GRID_H = 64, 32
H_NA, DH_NA = 8, 64
WIN_H, WIN_W = 8, 16
H_MLA, NOPE, ROPE, VH = 8, 64, 32, 64
Q_LORA, KV_LORA = 256, 128
D_NA = H_NA * DH_NA
D_MLA = H_MLA * VH
N_EXP, N_GRP, EPG = 16, 4, 4
D_EXP = 512
ROPE_THETA = 10000.0
EPS = 1e-6
NA_SCALE = DH_NA ** -0.5
MLA_SCALE = (NOPE + ROPE) ** -0.5
LOG2E = 1.4426950408889634

N_CTX = B_CTX * S_CTX
N_LAT = B_LAT * S_LAT
N_TOK = N_CTX + N_LAT
TM = 256
TT = 512
CTX_TILES = N_CTX // TT
LAT_TILES = N_LAT // TT
N_TILES = CTX_TILES + LAT_TILES
LAT_TILES_PER_SEQ = S_LAT // TT
Q_TILES_PER_SEQ = S_LAT // TM
LANE = 128
HP = LANE
D_QP = H_MLA * HP
W_IN_COLS = 3 * D_NA + Q_LORA + KV_LORA + 2 * LANE
N_MOD_ROWS = 16
N_PAIR = 6
N_BUCKET = N_GRP * N_PAIR
MOE_TILES = N_TOK // TM + N_BUCKET
BUCKET_ROWS = 32
D_ROW = D + LANE
NA_ROWS_PER_TILE = TM // GRID_W
NA_KEY_ROWS = 12
NA_KEYS = NA_KEY_ROWS * GRID_W
VMEM_LIMIT = 48 * 1024 * 1024
DMA_UNROLL = 8

_PAIR_LO = np.array([0, 0, 0, 1, 1, 2], np.int32)
_PAIR_HI = np.array([1, 2, 3, 2, 3, 3], np.int32)


def _dot(a, b):
    return jnp.dot(a, b, preferred_element_type=F32)


def _dot_nt(a, b):
    return lax.dot_general(a, b, (((1,), (1,)), ((), ())), preferred_element_type=F32)


def _rms(x, g):
    return x * lax.rsqrt(jnp.mean(x * x, axis=-1, keepdims=True) + EPS) * g


def _mod_row(i):
    return jnp.where(i < CTX_TILES, 0, 1 + (i - CTX_TILES) // LAT_TILES_PER_SEQ)


def _pos_block(i):
    return jnp.where(i < CTX_TILES, 0, 1 + (i - CTX_TILES) % LAT_TILES_PER_SEQ)


def _ctx_block(i):
    return jnp.minimum(i, CTX_TILES - 1)


def _params(sem):
    return pltpu.CompilerParams(dimension_semantics=sem, vmem_limit_bytes=VMEM_LIMIT)


def _ada_kernel(cond_ref, w_ref, b_ref, o_ref):
    cond = cond_ref[...]
    act = cond * jax.nn.sigmoid(cond)
    o_ref[0] = jnp.dot(act, w_ref[0], preferred_element_type=F32,
                       precision=lax.Precision.HIGHEST) + b_ref[0]


def _ada(cond, w_ada, b_ada):
    nt = 1536
    return pl.pallas_call(
        _ada_kernel,
        grid=(DEPTH, 6 * D // nt),
        in_specs=[pl.BlockSpec((N_MOD_ROWS, D), lambda l, j: (0, 0)),
                  pl.BlockSpec((1, D, nt), lambda l, j: (l, 0, j)),
                  pl.BlockSpec((1, 1, nt), lambda l, j: (l, 0, j))],
        out_specs=pl.BlockSpec((1, N_MOD_ROWS, nt), lambda l, j: (l, 0, j)),
        out_shape=jax.ShapeDtypeStruct((DEPTH, N_MOD_ROWS, 6 * D), F32),
        compiler_params=_params(("arbitrary", "arbitrary")),
        name="ada",
    )(cond, w_ada, b_ada.reshape(DEPTH, 1, 6 * D))


def _ctxkv_kernel(ckv_ref, kr_ref, wuk_ref, wuv_ref, place_ref, kc_ref, vc_ref):
    ckv = ckv_ref[0, 0].astype(BF16)
    kr = _dot(kr_ref[0, 0].astype(BF16), place_ref[...])
    kc_ref[0, 0] = (_dot(ckv, wuk_ref[0]) + jnp.tile(kr, (1, H_MLA))).astype(BF16)
    vc_ref[0, 0] = _dot(ckv, wuv_ref[0]).astype(BF16)


def _ctxkv(cache_ckv, cache_krope, w_uk_p, w_uv, place):
    return pl.pallas_call(
        _ctxkv_kernel,
        grid=(DEPTH, B_LAT),
        in_specs=[pl.BlockSpec((1, 1, PAST, KV_LORA), lambda l, b: (b, l, 0, 0)),
                  pl.BlockSpec((1, 1, PAST, ROPE), lambda l, b: (b, l, 0, 0)),
                  pl.BlockSpec((1, KV_LORA, D_QP), lambda l, b: (l, 0, 0)),
                  pl.BlockSpec((1, KV_LORA, D_MLA), lambda l, b: (l, 0, 0)),
                  pl.BlockSpec((ROPE, HP), lambda l, b: (0, 0))],
        out_specs=[pl.BlockSpec((1, 1, PAST, D_QP), lambda l, b: (l, b, 0, 0)),
                   pl.BlockSpec((1, 1, PAST, D_MLA), lambda l, b: (l, b, 0, 0))],
        out_shape=[jax.ShapeDtypeStruct((DEPTH, B_LAT, PAST, D_QP), BF16),
                   jax.ShapeDtypeStruct((DEPTH, B_LAT, PAST, D_MLA), BF16)],
        compiler_params=_params(("arbitrary", "arbitrary")),
        name="ctxkv",
    )(cache_ckv, cache_krope, w_uk_p, w_uv, place)


def _pre_kernel(first, *refs):
    if first:
        xc_ref, xl_ref, *rest = refs
    else:
        x1_ref, moe_ref, modp_ref, *rest = refs
    (mod_ref, n1_ref, win_ref, qn_ref, wuq_ref, kvn_ref, wuk_ref, wuv_ref, cos_ref, sin_ref,
     xo_ref, qna_ref, kna_ref, vna_ref, qp_ref, kp_ref, vm_ref, knew_ref, vnew_ref, ckvnew_ref, krnew_ref) = rest
    i = pl.program_id(0)

    if first:
        x = jnp.where(i < CTX_TILES, xc_ref[...], xl_ref[...])
    else:
        x = x1_ref[...] + modp_ref[0, 5:6, :] * moe_ref[...]
    xo_ref[...] = x
    m = mod_ref[0]
    h = _rms(x, n1_ref[0]) * (1.0 + m[1:2, :]) + m[0:1, :]
    proj = _dot(h.astype(BF16), win_ref[0])
    k_na = proj[:, D_NA:2 * D_NA]
    v_na = proj[:, 2 * D_NA:3 * D_NA]
    qna_ref[...] = (proj[:, :D_NA] * NA_SCALE).astype(BF16)
    kna_ref[...] = k_na.astype(BF16)
    vna_ref[...] = v_na.astype(BF16)
    o = 3 * D_NA
    cq = proj[:, o:o + Q_LORA]
    ckv = proj[:, o + Q_LORA:o + Q_LORA + KV_LORA]
    kr = proj[:, o + Q_LORA + KV_LORA:o + Q_LORA + KV_LORA + LANE]
    kr_sw = proj[:, o + Q_LORA + KV_LORA + LANE:]
    cos = cos_ref[...]
    sin = sin_ref[...]

    cqn = _rms(cq, qn_ref[0]).astype(BF16)
    q2 = _dot(cqn, wuq_ref[0])
    qp = q2[:, :D_QP] * jnp.tile(cos, (1, H_MLA)) + q2[:, D_QP:] * jnp.tile(sin, (1, H_MLA))
    qp_ref[...] = qp.astype(BF16)

    ckvn = _rms(ckv, kvn_ref[0])
    ckvb = ckvn.astype(BF16)
    kr_rot = kr * cos + kr_sw * sin
    kp_ref[...] = (_dot(ckvb, wuk_ref[0]) + jnp.tile(kr_rot, (1, H_MLA))).astype(BF16)
    vm_ref[...] = _dot(ckvb, wuv_ref[0]).astype(BF16)

    @pl.when(i < CTX_TILES)
    def _():
        knew_ref[...] = k_na
        vnew_ref[...] = v_na
        ckvnew_ref[...] = ckvn
        krnew_ref[...] = kr[:, NOPE:NOPE + ROPE]


def _pre(l, first, xin, mods, wts, tabs):
    tok = lambda n: pl.BlockSpec((TT, n), lambda i: (i, 0))
    ctx = lambda n: pl.BlockSpec((TT, n), lambda i: (_ctx_block(i), 0))
    lat = lambda n: pl.BlockSpec((TT, n), lambda i: (jnp.maximum(i - CTX_TILES, 0), 0))
    lay = lambda a: pl.BlockSpec((1,) + a.shape[1:], lambda i: (l,) + (0,) * (a.ndim - 1))
    in_specs = []
    args = []
    if first:
        in_specs += [ctx(D), lat(D)]
        args += list(xin)
    else:
        x1, moe = xin
        in_specs += [tok(D), tok(D), pl.BlockSpec((1, 6, D), lambda i: ((l - 1) * N_MOD_ROWS + _mod_row(i), 0, 0))]
        args += [x1, moe, mods]
    in_specs.append(pl.BlockSpec((1, 6, D), lambda i: (l * N_MOD_ROWS + _mod_row(i), 0, 0)))
    args.append(mods)
    for name in ("n1", "w_in", "qn", "w_uq", "kvn", "w_uk", "w_uv"):
        in_specs.append(lay(wts[name]))
        args.append(wts[name])
    for t in tabs:
        in_specs.append(pl.BlockSpec((TT, LANE), lambda i: (_pos_block(i), 0)))
        args.append(t)
    out_specs = [tok(D), tok(D_NA), tok(D_NA), tok(D_NA), tok(D_QP), tok(D_QP), tok(D_MLA),
                 ctx(D_NA), ctx(D_NA), ctx(KV_LORA), ctx(ROPE)]
    out_shape = [jax.ShapeDtypeStruct((N_TOK, D), F32),
                 jax.ShapeDtypeStruct((N_TOK, D_NA), BF16),
                 jax.ShapeDtypeStruct((N_TOK, D_NA), BF16),
                 jax.ShapeDtypeStruct((N_TOK, D_NA), BF16),
                 jax.ShapeDtypeStruct((N_TOK, D_QP), BF16),
                 jax.ShapeDtypeStruct((N_TOK, D_QP), BF16),
                 jax.ShapeDtypeStruct((N_TOK, D_MLA), BF16),
                 jax.ShapeDtypeStruct((N_CTX, D_NA), F32),
                 jax.ShapeDtypeStruct((N_CTX, D_NA), F32),
                 jax.ShapeDtypeStruct((N_CTX, KV_LORA), F32),
                 jax.ShapeDtypeStruct((N_CTX, ROPE), F32)]
    return pl.pallas_call(
        functools.partial(_pre_kernel, first),
        grid=(N_TILES,),
        in_specs=in_specs,
        out_specs=out_specs,
        out_shape=out_shape,
        compiler_params=_params(("arbitrary",)),
        name="pre",
    )(*args)


def _softmax_parts(parts, scale=None):
    m = parts[0].max(axis=-1, keepdims=True)
    for s in parts[1:]:
        m = jnp.maximum(m, s.max(axis=-1, keepdims=True))
    if scale is None:
        ps = [jnp.exp(s - m) for s in parts]
    else:
        ps = [jnp.exp2((s - m) * (scale * LOG2E)) for s in parts]
    den = ps[0].sum(axis=-1, keepdims=True)
    for p in ps[1:]:
        den = den + p.sum(axis=-1, keepdims=True)
    return ps, den


def _low_lanes():
    return lax.broadcasted_iota(jnp.int32, (TM, LANE), 1) < DH_NA


def _split_pair(qpair, low):
    zero = jnp.zeros_like(qpair)
    return jnp.where(low, qpair, zero), jnp.where(low, zero, qpair)


def _ctx_attn_kernel(qna_ref, kna_ref, vna_ref, qp_ref, kp_ref, vm_ref, o_ref):
    low = _low_lanes()
    outs = []
    for hp in range(H_NA // 2):
        sl = slice(hp * LANE, (hp + 1) * LANE)
        kpair = kna_ref[:, sl]
        vpair = vna_ref[:, sl]
        res = []
        for qh in _split_pair(qna_ref[:, sl], low):
            (p,), den = _softmax_parts([_dot_nt(qh, kpair)])
            res.append(_dot(p.astype(BF16), vpair) / den)
        outs.append(jnp.where(low, res[0], res[1]))
    for hp in range(H_MLA // 2):
        vpair = vm_ref[:, hp * LANE:(hp + 1) * LANE]
        res = []
        for h in (2 * hp, 2 * hp + 1):
            sl = slice(h * HP, (h + 1) * HP)
            (p,), den = _softmax_parts([_dot_nt(qp_ref[:, sl], kp_ref[:, sl])], MLA_SCALE)
            res.append(_dot(p.astype(BF16), vpair) / den)
        outs.append(jnp.where(low, res[0], res[1]))
    o_ref[...] = jnp.concatenate(outs, axis=1).astype(BF16)


def _ctx_attn(qna, kna, vna, qp, kp, vm):
    tok = lambda n: pl.BlockSpec((S_CTX, n), lambda i: (i, 0))
    return pl.pallas_call(
        _ctx_attn_kernel,
        grid=(B_CTX,),
        in_specs=[tok(D_NA), tok(D_NA), tok(D_NA), tok(D_QP), tok(D_QP), tok(D_MLA)],
        out_specs=tok(D_NA + D_MLA),
        out_shape=jax.ShapeDtypeStruct((N_CTX, D_NA + D_MLA), BF16),
        compiler_params=_params(("arbitrary",)),
        name="ctx_attn",
    )(qna, kna, vna, qp, kp, vm)


def _na_kernel(q_ref, k_ref, v_ref, kc_ref, vc_ref, bias_ref, o_ref):
    t = pl.program_id(1)
    row0 = jnp.clip(t * NA_ROWS_PER_TILE - WIN_H // 2, 0, GRID_H - NA_KEY_ROWS)
    ks = pl.multiple_of(row0 * GRID_W, GRID_W)
    low = _low_lanes()
    outs = []
    for hp in range(H_NA // 2):
        sl = slice(hp * LANE, (hp + 1) * LANE)
        kwin = k_ref[pl.ds(ks, NA_KEYS), sl]
        vwin = v_ref[pl.ds(ks, NA_KEYS), sl]
        kctx = kc_ref[0, 0, :, sl].astype(BF16)
        vctx = vc_ref[0, 0, :, sl].astype(BF16)
        res = []
        for half, qh in enumerate(_split_pair(q_ref[:, sl], low)):
            s_win = _dot_nt(qh, kwin) + bias_ref[0, 0, 2 * hp + half]
            s_ctx = _dot_nt(qh, kctx)
            (p_win, p_ctx), den = _softmax_parts([s_win, s_ctx])
            o = _dot(p_win.astype(BF16), vwin) + _dot(p_ctx.astype(BF16), vctx)
            res.append(o / den)
        outs.append(jnp.where(low, res[0], res[1]))
    o_ref[...] = jnp.concatenate(outs, axis=1).astype(BF16)


def _na_attn(l, qna, kna, vna, cache_k, cache_v, bias):
    nt = Q_TILES_PER_SEQ
    seq = lambda n: pl.BlockSpec((S_LAT, n), lambda b, t: (N_CTX // S_LAT + b, 0))
    cache = pl.BlockSpec((1, 1, PAST, D_NA), lambda b, t: (b, l, 0, 0))
    kind = lambda t: jnp.where(t == 0, 0, jnp.where(t == nt - 1, 2, 1))
    return pl.pallas_call(
        _na_kernel,
        grid=(B_LAT, nt),
        in_specs=[pl.BlockSpec((TM, D_NA), lambda b, t: (N_CTX // TM + b * nt + t, 0)),
                  seq(D_NA), seq(D_NA), cache, cache,
                  pl.BlockSpec((1, 1, H_NA, TM, NA_KEYS), lambda b, t: (l, kind(t), 0, 0, 0))],
        out_specs=pl.BlockSpec((TM, D_NA), lambda b, t: (b * nt + t, 0)),
        out_shape=jax.ShapeDtypeStruct((N_LAT, D_NA), BF16),
        compiler_params=_params(("arbitrary", "arbitrary")),
        name="na_attn",
    )(qna, kna, vna, cache_k, cache_v, bias)


def _na_bias_tables(rpb):
    n_dr, n_dc = 2 * WIN_H - 1, 2 * WIN_W - 1
    qc = np.arange(GRID_W)[:, None]
    kc = np.arange(GRID_W)[None, :]
    ws = np.clip(qc - WIN_W // 2, 0, GRID_W - WIN_W)
    col_ok = (kc >= ws) & (kc < ws + WIN_W)
    col_hot = ((kc - qc + WIN_W - 1)[..., None] == np.arange(n_dc)) & col_ok[..., None]
    col_hot = col_hot.reshape(GRID_W * GRID_W, n_dc).astype(np.float32)
    by_col = jnp.einsum("pb,lhab->lhap", jnp.asarray(col_hot), rpb, precision=lax.Precision.HIGHEST)
    by_col = jnp.where(jnp.asarray(col_ok), by_col.reshape(DEPTH, H_NA, n_dr, GRID_W, GRID_W), -jnp.inf)
    pad = jnp.full((DEPTH, H_NA, _BIAS_PAD, GRID_W, GRID_W), -jnp.inf, F32)
    blocks = jnp.concatenate([pad, by_col, pad], axis=2)
    pairs = jnp.concatenate([blocks[:, :, :-1], blocks[:, :, 1:]], axis=-1)
    n_pairs = n_dr + 2 * _BIAS_PAD - 1
    return pl.pallas_call(
        _na_bias_kernel,
        grid=(DEPTH, H_NA),
        in_specs=[pl.BlockSpec((1, 1, n_pairs, GRID_W, LANE), lambda l, h: (l, h, 0, 0, 0))],
        out_specs=pl.BlockSpec((1, len(_NA_KINDS), 1, TM, NA_KEYS), lambda l, h: (l, 0, h, 0, 0)),
        out_shape=jax.ShapeDtypeStruct((DEPTH, len(_NA_KINDS), H_NA, TM, NA_KEYS), F32),
        compiler_params=_params(("arbitrary", "arbitrary")),
        name="na_bias",
    )(pairs)


_NA_KINDS = ((0, 0), (NA_ROWS_PER_TILE, 0), (GRID_H - NA_ROWS_PER_TILE, GRID_H - NA_KEY_ROWS))
_BIAS_PAD = 4


def _na_bias_kernel(pairs_ref, o_ref):
    low = lax.broadcasted_iota(jnp.int32, (GRID_W, LANE), 1) < GRID_W
    ninf = jnp.full((GRID_W, LANE), -jnp.inf, F32)
    for kind, (r0, k0) in enumerate(_NA_KINDS):
        for i in range(NA_ROWS_PER_TILE):
            r = r0 + i
            rs = min(max(r - WIN_H // 2, 0), GRID_H - WIN_H)
            for m in range(NA_KEY_ROWS // 2):
                kr = k0 + 2 * m
                ok0 = rs <= kr < rs + WIN_H
                ok1 = rs <= kr + 1 < rs + WIN_H
                blk = pairs_ref[0, 0, kr - r + WIN_H - 1 + _BIAS_PAD]
                if ok0 and not ok1:
                    blk = jnp.where(low, blk, ninf)
                elif ok1 and not ok0:
                    blk = jnp.where(low, ninf, blk)
                elif not ok0:
                    blk = ninf
                o_ref[0, kind, 0, i * GRID_W:(i + 1) * GRID_W, m * LANE:(m + 1) * LANE] = blk


def _mla_kernel(q_ref, k_ref, v_ref, kc_ref, vc_ref, o_ref):
    low = _low_lanes()
    outs = []
    for hp in range(H_MLA // 2):
        vsl = slice(hp * LANE, (hp + 1) * LANE)
        vlat = v_ref[:, vsl]
        vctx = vc_ref[0, 0, :, vsl]
        res = []
        for h in (2 * hp, 2 * hp + 1):
            sl = slice(h * HP, (h + 1) * HP)
            q = q_ref[:, sl]
            s_ctx = _dot_nt(q, kc_ref[0, 0, :, sl])
            s_lat = _dot_nt(q, k_ref[:, sl])
            (p_ctx, p_lat), den = _softmax_parts([s_ctx, s_lat], MLA_SCALE)
            o = _dot(p_ctx.astype(BF16), vctx) + _dot(p_lat.astype(BF16), vlat)
            res.append(o / den)
        outs.append(jnp.where(low, res[0], res[1]))
    o_ref[...] = jnp.concatenate(outs, axis=1).astype(BF16)


def _mla_attn(l, qp, kp, vm, kc, vc):
    nt = Q_TILES_PER_SEQ
    seq = lambda n: pl.BlockSpec((S_LAT, n), lambda b, t: (N_CTX // S_LAT + b, 0))
    return pl.pallas_call(
        _mla_kernel,
        grid=(B_LAT, nt),
        in_specs=[pl.BlockSpec((TM, D_QP), lambda b, t: (N_CTX // TM + b * nt + t, 0)),
                  seq(D_QP), seq(D_MLA),
                  pl.BlockSpec((1, 1, PAST, D_QP), lambda b, t: (l, b, 0, 0)),
                  pl.BlockSpec((1, 1, PAST, D_MLA), lambda b, t: (l, b, 0, 0))],
        out_specs=pl.BlockSpec((TM, D_MLA), lambda b, t: (b * nt + t, 0)),
        out_shape=jax.ShapeDtypeStruct((N_LAT, D_MLA), BF16),
        compiler_params=_params(("arbitrary", "arbitrary")),
        name="mla_attn",
    )(qp, kp, vm, kc, vc)


def _route(sel, aff):
    rows = lambda a, g: [a[EPG * g + e:EPG * g + e + 1, :] for e in range(EPG)]
    best = None
    for g in range(N_GRP):
        v = rows(sel, g)
        score = v[0] + v[1]
        for a, b in ((0, 2), (0, 3), (1, 2), (1, 3), (2, 3)):
            score = jnp.maximum(score, v[a] + v[b])
        if best is None:
            best, grp = score, jnp.zeros_like(score)
        else:
            better = score > best
            best = jnp.where(better, score, best)
            grp = jnp.where(better, float(g), grp)
    zero = jnp.zeros_like(best)
    sv = [zero] * EPG
    av = [zero] * EPG
    for g in range(N_GRP):
        hit = grp == float(g)
        sg, ag = rows(sel, g), rows(aff, g)
        sv = [jnp.where(hit, sg[e], sv[e]) for e in range(EPG)]
        av = [jnp.where(hit, ag[e], av[e]) for e in range(EPG)]

    def argmax_first(vals, skip=None):
        bv, bi = None, None
        for e in range(EPG):
            v = vals[e] if skip is None else jnp.where(skip == float(e), -jnp.inf, vals[e])
            if bv is None:
                bv, bi = v, jnp.zeros_like(v)
            else:
                better = v > bv
                bv = jnp.where(better, v, bv)
                bi = jnp.where(better, float(e), bi)
        return bi

    i1 = argmax_first(sv)
    i2 = argmax_first(sv, skip=i1)
    lo = jnp.minimum(i1, i2)
    hi = jnp.maximum(i1, i2)
    pick = lambda idx: sum(jnp.where(idx == float(e), av[e], 0.0) for e in range(EPG))
    a_lo, a_hi = pick(lo), pick(hi)
    den = a_lo + a_hi
    pair = jnp.where(lo == 0.0, 0.0, jnp.where(lo == 1.0, 3.0, 5.0)) + hi - lo - 1.0
    return grp * float(N_PAIR) + pair, a_lo / den, a_hi / den


def _post_kernel(x_ref, octx_ref, ona_ref, omla_ref, mod_ref, n2_ref, wout_ref, wr_ref, br_ref,
                 x1_ref, h2_ref, route_ref, counts_ref, carry):
    i = pl.program_id(0)

    @pl.when(i == 0)
    def _():
        carry[...] = jnp.zeros_like(carry)

    is_ctx = i < CTX_TILES
    o_lat = jnp.concatenate([ona_ref[...], omla_ref[...]], axis=1)
    o = jnp.where(is_ctx, octx_ref[...], o_lat)
    m = mod_ref[0]
    x1 = x_ref[...] + m[2:3, :] * _dot(o, wout_ref[0])
    x1_ref[...] = x1
    h2 = _rms(x1, n2_ref[0]) * (1.0 + m[4:5, :]) + m[3:4, :]
    logits = lax.dot_general(wr_ref[...], h2, (((1,), (1,)), ((), ())),
                             preferred_element_type=F32, precision=lax.Precision.HIGHEST)
    aff = jax.nn.sigmoid(logits)
    bucket, g_lo, g_hi = _route(aff + br_ref[...], aff)

    hit = lax.broadcasted_iota(jnp.int32, (BUCKET_ROWS, TT), 0).astype(F32) == bucket
    earlier = lax.broadcasted_iota(jnp.int32, (TT, TT), 0) <= lax.broadcasted_iota(jnp.int32, (TT, TT), 1)
    incl = _dot(jnp.where(hit, 1.0, 0.0).astype(BF16), jnp.where(earlier, 1.0, 0.0).astype(BF16))
    base = carry[...]
    rank = jnp.sum(jnp.where(hit, incl + base[:, 0:1], 0.0), axis=0, keepdims=True) - 1.0
    carry[...] = base + incl[:, TT - 1:TT]
    counts_ref[...] = carry[...]
    route_ref[0] = jnp.concatenate([bucket, rank, jnp.zeros((6, TT), F32)], axis=0)

    gate_cols = jnp.transpose(jnp.concatenate([g_lo, g_hi, jnp.zeros((6, TT), F32)], axis=0))
    h2_ref[:, :D] = h2
    h2_ref[:, D:] = jnp.concatenate([gate_cols, jnp.zeros((TT, LANE - 8), F32)], axis=1)


def _post(l, x, o_ctx, o_na, o_mla, mods, wts):
    tok = lambda n: pl.BlockSpec((TT, n), lambda i: (i, 0))
    lat = lambda n: pl.BlockSpec((TT, n), lambda i: (jnp.maximum(i - CTX_TILES, 0), 0))
    lay = lambda a: pl.BlockSpec((1,) + a.shape[1:], lambda i: (l,) + (0,) * (a.ndim - 1))
    full = lambda a: pl.BlockSpec(a.shape, lambda i: (0,) * a.ndim)
    return pl.pallas_call(
        _post_kernel,
        grid=(N_TILES,),
        in_specs=[tok(D), pl.BlockSpec((TT, D), lambda i: (_ctx_block(i), 0)), lat(D_NA), lat(D_MLA),
                  pl.BlockSpec((1, 6, D), lambda i: (l * N_MOD_ROWS + _mod_row(i), 0, 0)),
                  lay(wts["n2"]), lay(wts["w_out"]), full(wts["w_rt"]), full(wts["b_r"])],
        out_specs=[tok(D), tok(D_ROW), pl.BlockSpec((1, 8, TT), lambda i: (i, 0, 0)),
                   pl.BlockSpec((BUCKET_ROWS, LANE), lambda i: (0, 0))],
        out_shape=[jax.ShapeDtypeStruct((N_TOK, D), F32),
                   jax.ShapeDtypeStruct((N_TOK, D_ROW), F32),
                   jax.ShapeDtypeStruct((N_TILES, 8, TT), F32),
                   jax.ShapeDtypeStruct((BUCKET_ROWS, LANE), F32)],
        scratch_shapes=[pltpu.VMEM((BUCKET_ROWS, LANE), F32)],
        compiler_params=_params(("arbitrary",)),
        name="post",
    )(x, o_ctx, o_na, o_mla, mods, wts["n2"], wts["w_out"], wts["w_rt"], wts["b_r"])


def _moe_kernel(elo_ref, ehi_ref, nv_ref, total_ref, slot_ref,
                h2_hbm, wg1, wu1, wd1, wg2, wu2, wd2, out_hbm,
                xbuf, ybuf, src, gsem, ssem):
    del elo_ref, ehi_ref
    t = pl.program_id(0)
    total = total_ref[0]
    par = t % 2

    def row_token(tile, j):
        return src[tile * TM + jnp.where(j < nv_ref[tile], j, 0)]

    def start_gather(tile, p):
        def body(j, c):
            pltpu.make_async_copy(h2_hbm.at[pl.ds(row_token(tile, j), 1), :],
                                  xbuf.at[p, pl.ds(j, 1), :], gsem.at[p]).start()
            return c
        lax.fori_loop(0, TM, body, 0, unroll=DMA_UNROLL)

    def start_scatter(tile, p):
        def body(j, c):
            dst = jnp.where(j < nv_ref[tile], row_token(tile, j), N_TOK + j)
            pltpu.make_async_copy(ybuf.at[p, pl.ds(j, 1), :],
                                  out_hbm.at[pl.ds(dst, 1), :], ssem.at[p]).start()
            return c
        lax.fori_loop(0, TM, body, 0, unroll=DMA_UNROLL)

    def wait_gather(p):
        pltpu.make_async_copy(h2_hbm.at[pl.ds(0, TM), :], xbuf.at[p], gsem.at[p]).wait()

    def wait_scatter(p):
        pltpu.make_async_copy(ybuf.at[p], out_hbm.at[pl.ds(0, TM), :], ssem.at[p]).wait()

    @pl.when(t == 0)
    def _():
        def invert(i, c):
            src[slot_ref[i]] = i
            return c
        lax.fori_loop(0, N_TOK, invert, 0, unroll=DMA_UNROLL)
        ybuf[0] = jnp.zeros((TM, D), F32)
        spare = pltpu.make_async_copy(ybuf.at[0], out_hbm.at[pl.ds(N_TOK, TM), :], ssem.at[0])
        spare.start()
        spare.wait()
        start_gather(0, 0)

    @pl.when(t + 1 < total)
    def _():
        start_gather(t + 1, 1 - par)

    @pl.when(t < total)
    def _():
        wait_gather(par)

        @pl.when(t >= 2)
        def _():
            wait_scatter(par)

        xg = xbuf[par]
        x = xg[:, :D].astype(BF16)
        gates = xg[:, D:]
        y = None
        for k, (wg, wu, wd) in enumerate(((wg1, wu1, wd1), (wg2, wu2, wd2))):
            a = _dot(x, wg[0, 0])
            hid = a * jax.nn.sigmoid(a) * _dot(x, wu[0, 0]) * gates[:, k:k + 1]
            part = _dot(hid.astype(BF16), wd[0, 0])
            y = part if y is None else y + part
        ybuf[par] = y
        start_scatter(t, par)

        @pl.when(t == total - 1)
        def _():
            @pl.when(t >= 1)
            def _():
                wait_scatter(1 - par)
            wait_scatter(par)


def _moe(l, h2, e_lo, e_hi, n_valid, total, slot, wts):
    up = lambda which: pl.BlockSpec((1, 1, D, D_EXP), lambda t, *pf: (l, pf[which][t], 0, 0))
    down = lambda which: pl.BlockSpec((1, 1, D_EXP, D), lambda t, *pf: (l, pf[which][t], 0, 0))
    first, second = 0, 1
    return pl.pallas_call(
        _moe_kernel,
        grid_spec=pltpu.PrefetchScalarGridSpec(
            num_scalar_prefetch=5,
            grid=(MOE_TILES,),
            in_specs=[pl.BlockSpec(memory_space=pl.ANY),
                      up(first), up(first), down(first), up(second), up(second), down(second)],
            out_specs=pl.BlockSpec(memory_space=pl.ANY),
            scratch_shapes=[pltpu.VMEM((2, TM, D_ROW), F32), pltpu.VMEM((2, TM, D), F32),
                            pltpu.SMEM((MOE_TILES * TM,), jnp.int32),
                            pltpu.SemaphoreType.DMA((2,)), pltpu.SemaphoreType.DMA((2,))],
        ),
        out_shape=jax.ShapeDtypeStruct((N_TOK + TM, D), F32),
        compiler_params=_params(("arbitrary",)),
        name="moe",
    )(e_lo, e_hi, n_valid, total, slot, h2,
      wts["w_gate"], wts["w_up"], wts["w_down"], wts["w_gate"], wts["w_up"], wts["w_down"])


def _moe_schedule(route, counts):
    bucket = route[:, 0, :].reshape(N_TOK).astype(jnp.int32)
    rank = route[:, 1, :].reshape(N_TOK).astype(jnp.int32)
    counts = counts[:N_BUCKET, 0].astype(jnp.int32)
    tiles = (counts + TM - 1) // TM
    tile_end = jnp.cumsum(tiles)
    tile_start = tile_end - tiles
    buckets = jnp.arange(N_BUCKET, dtype=jnp.int32)
    first_slot = jnp.where(bucket[:, None] == buckets[None, :], (tile_start * TM)[None, :], 0).sum(axis=1)
    slot = first_slot + rank
    t = jnp.arange(MOE_TILES, dtype=jnp.int32)
    total = tile_end[-1]
    bucket_of = lambda tile: (tile_end[None, :] <= tile[:, None]).astype(jnp.int32).sum(axis=1)
    tb = bucket_of(jnp.minimum(t, total - 1))
    n_valid = jnp.where(t < total, jnp.clip(counts[tb] - (t - tile_start[tb]) * TM, 0, TM), 0).astype(jnp.int32)
    grp = tb // N_PAIR
    e_lo = grp * EPG + jnp.asarray(_PAIR_LO)[tb % N_PAIR]
    e_hi = grp * EPG + jnp.asarray(_PAIR_HI)[tb % N_PAIR]
    return e_lo, e_hi, n_valid, total.reshape(1), slot


def _final_kernel(x1_ref, moe_ref, mod_ref, g_ref, yc_ref, yl_ref):
    i = pl.program_id(0)
    y = _rms(x1_ref[...] + mod_ref[0, 5:6, :] * moe_ref[...], g_ref[...])

    @pl.when(i < CTX_TILES)
    def _():
        yc_ref[...] = y

    @pl.when(i >= CTX_TILES)
    def _():
        yl_ref[...] = y


def _final(x1, moe, mods, g):
    tok = pl.BlockSpec((TT, D), lambda i: (i, 0))
    l = DEPTH - 1
    return pl.pallas_call(
        _final_kernel,
        grid=(N_TILES,),
        in_specs=[tok, tok,
                  pl.BlockSpec((1, 6, D), lambda i: (l * N_MOD_ROWS + _mod_row(i), 0, 0)),
                  pl.BlockSpec((1, D), lambda i: (0, 0))],
        out_specs=[pl.BlockSpec((TT, D), lambda i: (_ctx_block(i), 0)),
                   pl.BlockSpec((TT, D), lambda i: (jnp.maximum(i - CTX_TILES, 0), 0))],
        out_shape=[jax.ShapeDtypeStruct((N_CTX, D), F32), jax.ShapeDtypeStruct((N_LAT, D), F32)],
        compiler_params=_params(("arbitrary",)),
        name="final",
    )(x1, moe, mods, g)


def _swap_halves(w):
    q = ROPE // 4
    return jnp.concatenate([-w[..., q:2 * q], w[..., :q], -w[..., 3 * q:], w[..., 2 * q:3 * q]], axis=-1)


def _pad_head(nope, rope):
    z = jnp.zeros(rope.shape[:-1] + (HP - NOPE - ROPE,), rope.dtype)
    return jnp.concatenate([nope, rope, z], axis=-1)


def _prepare_weights(norm1_g, norm2_g, w_in, q_norm_g, w_uq, kv_norm_g, w_ukv, w_out, w_router, b_router,
                     w_gate, w_up, w_down):
    o = 3 * D_NA + Q_LORA + KV_LORA
    w_kr = w_in[:, :, o:]
    zeros_nope = jnp.zeros((DEPTH, D, NOPE), F32)
    w_in_p = jnp.concatenate([w_in[:, :, :o], _pad_head(zeros_nope, w_kr),
                              _pad_head(zeros_nope, _swap_halves(w_kr))], axis=-1)
    wq = w_uq.reshape(DEPTH, Q_LORA, H_MLA, NOPE + ROPE)
    plain = _pad_head(wq[..., :NOPE], wq[..., NOPE:]).reshape(DEPTH, Q_LORA, D_QP)
    swapped = _pad_head(jnp.zeros_like(wq[..., :NOPE]), _swap_halves(wq[..., NOPE:])).reshape(DEPTH, Q_LORA, D_QP)
    wkv = w_ukv.reshape(DEPTH, KV_LORA, H_MLA, NOPE + VH)
    w_uk_p = jnp.concatenate([wkv[..., :NOPE], jnp.zeros((DEPTH, KV_LORA, H_MLA, HP - NOPE), F32)],
                             axis=-1).reshape(DEPTH, KV_LORA, D_QP)
    w_uv = wkv[..., NOPE:].reshape(DEPTH, KV_LORA, D_MLA)
    return {
        "n1": norm1_g.reshape(DEPTH, 1, D),
        "n2": norm2_g.reshape(DEPTH, 1, D),
        "w_in": w_in_p.astype(BF16),
        "qn": q_norm_g.reshape(DEPTH, 1, Q_LORA),
        "w_uq": jnp.concatenate([plain, swapped], axis=-1).astype(BF16),
        "kvn": kv_norm_g.reshape(DEPTH, 1, KV_LORA),
        "w_uk": w_uk_p.astype(BF16),
        "w_uv": w_uv.astype(BF16),
        "w_out": w_out.astype(BF16),
        "w_rt": w_router.T,
        "b_r": b_router.reshape(N_EXP, 1),
        "w_gate": w_gate.astype(BF16),
        "w_up": w_up.astype(BF16),
        "w_down": w_down.astype(BF16),
    }


def _rope_tables():
    half = ROPE // 2
    freqs = 1.0 / (ROPE_THETA ** (np.arange(0, half, 2, dtype=np.float32) / half))
    pos = np.arange(S_LAT)
    ang_r = (pos // GRID_W).astype(np.float32)[:, None] * freqs
    ang_c = (pos % GRID_W).astype(np.float32)[:, None] * freqs
    ang = np.concatenate([ang_r, ang_r, ang_c, ang_c], axis=1)
    cos = np.ones((TT + S_LAT, LANE), np.float32)
    sin = np.zeros((TT + S_LAT, LANE), np.float32)
    cos[TT:, NOPE:NOPE + ROPE] = np.cos(ang)
    sin[TT:, NOPE:NOPE + ROPE] = np.sin(ang)
    return jnp.asarray(cos), jnp.asarray(sin)


def kernel(x_prompt, x_sample, cache_nat_k, cache_nat_v, cache_mla_ckv, cache_mla_krope, c, c_ctx, norm1_g, norm2_g, w_ada, b_ada, w_in, q_norm_g, w_uq, kv_norm_g, w_ukv, rpb, w_out, w_router, b_router, w_gate, w_up, w_down, final_norm_g):
    wts = _prepare_weights(norm1_g, norm2_g, w_in, q_norm_g, w_uq, kv_norm_g, w_ukv, w_out, w_router, b_router,
                           w_gate, w_up, w_down)
    tabs = _rope_tables()
    cond = jnp.concatenate([c_ctx[None, :], c, jnp.zeros((N_MOD_ROWS - 1 - B_LAT, D), F32)], axis=0)
    mods = _ada(cond, w_ada, b_ada).reshape(DEPTH * N_MOD_ROWS, 6, D)
    place = jnp.asarray(np.eye(ROPE, HP, k=NOPE, dtype=np.float32)).astype(BF16)
    kc_all, vc_all = _ctxkv(cache_mla_ckv, cache_mla_krope, wts["w_uk"], wts["w_uv"], place)
    cache_k = cache_nat_k.reshape(B_LAT, DEPTH, PAST, D_NA)
    cache_v = cache_nat_v.reshape(B_LAT, DEPTH, PAST, D_NA)
    bias = _na_bias_tables(rpb)

    x1 = moe = None
    new_k, new_v, new_ckv, new_kr = [], [], [], []
    for l in range(DEPTH):
        xin = (x_prompt.reshape(N_CTX, D), x_sample.reshape(N_LAT, D)) if l == 0 else (x1, moe)
        x, qna, kna, vna, qp, kp, vm, k_new, v_new, ckv_new, kr_new = _pre(l, l == 0, xin, mods, wts, tabs)
        new_k.append(k_new)
        new_v.append(v_new)
        new_ckv.append(ckv_new)
        new_kr.append(kr_new)
        o_ctx = _ctx_attn(qna, kna, vna, qp, kp, vm)
        o_na = _na_attn(l, qna, kna, vna, cache_k, cache_v, bias)
        o_mla = _mla_attn(l, qp, kp, vm, kc_all, vc_all)
        x1, h2, route, counts = _post(l, x, o_ctx, o_na, o_mla, mods, wts)
        moe = _moe(l, h2, *_moe_schedule(route, counts), wts)
    y_ctx, y_lat = _final(x1, moe, mods, final_norm_g.reshape(1, D))

    stack = lambda parts, tail: jnp.stack([p.reshape((B_CTX, S_CTX) + tail) for p in parts], axis=1)
    return (y_ctx.reshape(B_CTX, S_CTX, D), y_lat.reshape(B_LAT, S_LAT, D),
            stack(new_k, (H_NA, DH_NA)), stack(new_v, (H_NA, DH_NA)),
            stack(new_ckv, (KV_LORA,)), stack(new_kr, (ROPE,)))
```

```python
import functools

import numpy as np
import jax
import jax.numpy as jnp
from jax import lax
from jax.experimental import pallas as pl
from jax.experimental.pallas import tpu as pltpu

F32 = jnp.float32
BF16 = jnp.bfloat16

D = 1024
B_CTX, S_CTX = 32, 256
B_LAT, S_LAT = 8, 2048
DEPTH = 4
PAST = 256
GRID_W, GRID_H = 64, 32
H_NA, DH_NA = 8, 64
WIN_H, WIN_W = 8, 16
H_MLA, NOPE, ROPE, VH = 8, 64, 32, 64
Q_LORA, KV_LORA = 256, 128
D_NA = H_NA * DH_NA
D_MLA = H_MLA * VH
N_EXP, N_GRP, EPG = 16, 4, 4
D_EXP = 512
ROPE_THETA = 10000.0
EPS = 1e-6
NA_SCALE = DH_NA ** -0.5
MLA_SCALE = (NOPE + ROPE) ** -0.5
LOG2E = 1.4426950408889634

N_CTX = B_CTX * S_CTX
N_LAT = B_LAT * S_LAT
N_TOK = N_CTX + N_LAT
TM = 256
TT = 512
CTX_TILES = N_CTX // TT
LAT_TILES = N_LAT // TT
N_TILES = CTX_TILES + LAT_TILES
LAT_TILES_PER_SEQ = S_LAT // TT
Q_TILES_PER_SEQ = S_LAT // TM
LANE = 128
HP = LANE
D_QP = H_MLA * HP
W_IN_COLS = 3 * D_NA + Q_LORA + KV_LORA + 2 * LANE
N_MOD_ROWS = 16
N_PAIR = 6
N_BUCKET = N_GRP * N_PAIR
MOE_TILES = N_TOK // TM + N_BUCKET
BUCKET_ROWS = 32
Y_CHUNKS = D // LANE
X_CHUNKS = 2 * Y_CHUNKS
GATE_CHUNK = Y_CHUNKS


def _to_rows(ref, value, chunks, lead=()):
    t = value.shape[0]
    for c in range(value.shape[1] // LANE):
        ref[lead + (pl.ds(c, t, stride=chunks), slice(None))] = value[:, c * LANE:(c + 1) * LANE]


def _from_rows(ref, t, chunks, first, n, lead=()):
    parts = [ref[lead + (pl.ds(first + c, t, stride=chunks), slice(None))] for c in range(n)]
    return parts[0] if n == 1 else jnp.concatenate(parts, axis=1)
NA_ROWS_PER_TILE = TM // GRID_W
NA_KEY_ROWS = 12
NA_KEYS = NA_KEY_ROWS * GRID_W
VMEM_LIMIT = 48 * 1024 * 1024
DMA_UNROLL = 8

_PAIR_LO = np.array([0, 0, 0, 1, 1, 2], np.int32)
_PAIR_HI = np.array([1, 2, 3, 2, 3, 3], np.int32)


def _dot(a, b):
    return jnp.dot(a, b, preferred_element_type=F32)


def _dot_nt(a, b):
    return lax.dot_general(a, b, (((1,), (1,)), ((), ())), preferred_element_type=F32)


def _rms(x, g):
    return x * lax.rsqrt(jnp.mean(x * x, axis=-1, keepdims=True) + EPS) * g


def _mod_row(i):
    return jnp.where(i < CTX_TILES, 0, 1 + (i - CTX_TILES) // LAT_TILES_PER_SEQ)


def _pos_block(i):
    return jnp.where(i < CTX_TILES, 0, 1 + (i - CTX_TILES) % LAT_TILES_PER_SEQ)


def _ctx_block(i):
    return jnp.minimum(i, CTX_TILES - 1)


def _params(sem):
    return pltpu.CompilerParams(dimension_semantics=sem, vmem_limit_bytes=VMEM_LIMIT)


def _ada_kernel(cond_ref, w_ref, b_ref, o_ref):
    cond = cond_ref[...]
    act = cond * jax.nn.sigmoid(cond)
    o_ref[0] = jnp.dot(act, w_ref[0], preferred_element_type=F32,
                       precision=lax.Precision.HIGHEST) + b_ref[0]


def _ada(cond, w_ada, b_ada):
    nt = 1536
    return pl.pallas_call(
        _ada_kernel,
        grid=(DEPTH, 6 * D // nt),
        in_specs=[pl.BlockSpec((N_MOD_ROWS, D), lambda l, j: (0, 0)),
                  pl.BlockSpec((1, D, nt), lambda l, j: (l, 0, j)),
                  pl.BlockSpec((1, 1, nt), lambda l, j: (l, 0, j))],
        out_specs=pl.BlockSpec((1, N_MOD_ROWS, nt), lambda l, j: (l, 0, j)),
        out_shape=jax.ShapeDtypeStruct((DEPTH, N_MOD_ROWS, 6 * D), F32),
        compiler_params=_params(("arbitrary", "arbitrary")),
        name="ada",
    )(cond, w_ada, b_ada.reshape(DEPTH, 1, 6 * D))


def _ctxkv_kernel(ckv_ref, kr_ref, wuk_ref, wuv_ref, place_ref, kc_ref, vc_ref):
    ckv = ckv_ref[0, 0].astype(BF16)
    kr = _dot(kr_ref[0, 0].astype(BF16), place_ref[...])
    kc_ref[0, 0] = (_dot(ckv, wuk_ref[0]) + jnp.tile(kr, (1, H_MLA))).astype(BF16)
    vc_ref[0, 0] = _dot(ckv, wuv_ref[0]).astype(BF16)


def _ctxkv(cache_ckv, cache_krope, w_uk_p, w_uv, place):
    return pl.pallas_call(
        _ctxkv_kernel,
        grid=(DEPTH, B_LAT),
        in_specs=[pl.BlockSpec((1, 1, PAST, KV_LORA), lambda l, b: (b, l, 0, 0)),
                  pl.BlockSpec((1, 1, PAST, ROPE), lambda l, b: (b, l, 0, 0)),
                  pl.BlockSpec((1, KV_LORA, D_QP), lambda l, b: (l, 0, 0)),
                  pl.BlockSpec((1, KV_LORA, D_MLA), lambda l, b: (l, 0, 0)),
                  pl.BlockSpec((ROPE, HP), lambda l, b: (0, 0))],
        out_specs=[pl.BlockSpec((1, 1, PAST, D_QP), lambda l, b: (l, b, 0, 0)),
                   pl.BlockSpec((1, 1, PAST, D_MLA), lambda l, b: (l, b, 0, 0))],
        out_shape=[jax.ShapeDtypeStruct((DEPTH, B_LAT, PAST, D_QP), BF16),
                   jax.ShapeDtypeStruct((DEPTH, B_LAT, PAST, D_MLA), BF16)],
        compiler_params=_params(("arbitrary", "arbitrary")),
        name="ctxkv",
    )(cache_ckv, cache_krope, w_uk_p, w_uv, place)


def _pre_kernel(first, *refs):
    if first:
        xc_ref, xl_ref, *rest = refs
    else:
        x1_ref, moe_ref, modp_ref, *rest = refs
    (mod_ref, n1_ref, win_ref, qn_ref, wuq_ref, kvn_ref, wuk_ref, wuv_ref, cos_ref, sin_ref,
     xo_ref, qna_ref, kna_ref, vna_ref, qp_ref, kp_ref, vm_ref, knew_ref, vnew_ref, ckvnew_ref, krnew_ref) = rest
    i = pl.program_id(0)

    if first:
        x = jnp.where(i < CTX_TILES, xc_ref[...], xl_ref[...])
    else:
        x = x1_ref[...] + modp_ref[0, 5:6, :] * _from_rows(moe_ref, TT, Y_CHUNKS, 0, Y_CHUNKS)
    xo_ref[...] = x
    m = mod_ref[0]
    h = _rms(x, n1_ref[0]) * (1.0 + m[1:2, :]) + m[0:1, :]
    proj = _dot(h.astype(BF16), win_ref[0])
    k_na = proj[:, D_NA:2 * D_NA]
    v_na = proj[:, 2 * D_NA:3 * D_NA]
    qna_ref[...] = (proj[:, :D_NA] * NA_SCALE).astype(BF16)
    kna_ref[...] = k_na.astype(BF16)
    vna_ref[...] = v_na.astype(BF16)
    o = 3 * D_NA
    cq = proj[:, o:o + Q_LORA]
    ckv = proj[:, o + Q_LORA:o + Q_LORA + KV_LORA]
    kr = proj[:, o + Q_LORA + KV_LORA:o + Q_LORA + KV_LORA + LANE]
    kr_sw = proj[:, o + Q_LORA + KV_LORA + LANE:]
    cos = cos_ref[...]
    sin = sin_ref[...]

    cqn = _rms(cq, qn_ref[0]).astype(BF16)
    q2 = _dot(cqn, wuq_ref[0])
    qp = q2[:, :D_QP] * jnp.tile(cos, (1, H_MLA)) + q2[:, D_QP:] * jnp.tile(sin, (1, H_MLA))
    qp_ref[...] = qp.astype(BF16)

    ckvn = _rms(ckv, kvn_ref[0])
    ckvb = ckvn.astype(BF16)
    kr_rot = kr * cos + kr_sw * sin
    kp_ref[...] = (_dot(ckvb, wuk_ref[0]) + jnp.tile(kr_rot, (1, H_MLA))).astype(BF16)
    vm_ref[...] = _dot(ckvb, wuv_ref[0]).astype(BF16)

    @pl.when(i < CTX_TILES)
    def _():
        knew_ref[...] = k_na
        vnew_ref[...] = v_na
        ckvnew_ref[...] = ckvn
        krnew_ref[...] = kr[:, NOPE:NOPE + ROPE]


def _pre(l, first, xin, mods, wts, tabs):
    tok = lambda n: pl.BlockSpec((TT, n), lambda i: (i, 0))
    ctx = lambda n: pl.BlockSpec((TT, n), lambda i: (_ctx_block(i), 0))
    lat = lambda n: pl.BlockSpec((TT, n), lambda i: (jnp.maximum(i - CTX_TILES, 0), 0))
    lay = lambda a: pl.BlockSpec((1,) + a.shape[1:], lambda i: (l,) + (0,) * (a.ndim - 1))
    in_specs = []
    args = []
    if first:
        in_specs += [ctx(D), lat(D)]
        args += list(xin)
    else:
        x1, moe = xin
        in_specs += [tok(D), pl.BlockSpec((TT * Y_CHUNKS, LANE), lambda i: (i, 0)),
                     pl.BlockSpec((1, 6, D), lambda i: ((l - 1) * N_MOD_ROWS + _mod_row(i), 0, 0))]
        args += [x1, moe, mods]
    in_specs.append(pl.BlockSpec((1, 6, D), lambda i: (l * N_MOD_ROWS + _mod_row(i), 0, 0)))
    args.append(mods)
    for name in ("n1", "w_in", "qn", "w_uq", "kvn", "w_uk", "w_uv"):
        in_specs.append(lay(wts[name]))
        args.append(wts[name])
    for t in tabs:
        in_specs.append(pl.BlockSpec((TT, LANE), lambda i: (_pos_block(i), 0)))
        args.append(t)
    out_specs = [tok(D), tok(D_NA), tok(D_NA), tok(D_NA), tok(D_QP), tok(D_QP), tok(D_MLA),
                 ctx(D_NA), ctx(D_NA), ctx(KV_LORA), ctx(ROPE)]
    out_shape = [jax.ShapeDtypeStruct((N_TOK, D), F32),
                 jax.ShapeDtypeStruct((N_TOK, D_NA), BF16),
                 jax.ShapeDtypeStruct((N_TOK, D_NA), BF16),
                 jax.ShapeDtypeStruct((N_TOK, D_NA), BF16),
                 jax.ShapeDtypeStruct((N_TOK, D_QP), BF16),
                 jax.ShapeDtypeStruct((N_TOK, D_QP), BF16),
                 jax.ShapeDtypeStruct((N_TOK, D_MLA), BF16),
                 jax.ShapeDtypeStruct((N_CTX, D_NA), F32),
                 jax.ShapeDtypeStruct((N_CTX, D_NA), F32),
                 jax.ShapeDtypeStruct((N_CTX, KV_LORA), F32),
                 jax.ShapeDtypeStruct((N_CTX, ROPE), F32)]
    return pl.pallas_call(
        functools.partial(_pre_kernel, first),
        grid=(N_TILES,),
        in_specs=in_specs,
        out_specs=out_specs,
        out_shape=out_shape,
        compiler_params=_params(("arbitrary",)),
        name="pre",
    )(*args)


def _softmax_parts(parts, scale=None):
    m = parts[0].max(axis=-1, keepdims=True)
    for s in parts[1:]:
        m = jnp.maximum(m, s.max(axis=-1, keepdims=True))
    if scale is None:
        ps = [jnp.exp(s - m) for s in parts]
    else:
        ps = [jnp.exp2((s - m) * (scale * LOG2E)) for s in parts]
    den = ps[0].sum(axis=-1, keepdims=True)
    for p in ps[1:]:
        den = den + p.sum(axis=-1, keepdims=True)
    return ps, den


def _low_lanes():
    return lax.broadcasted_iota(jnp.int32, (TM, LANE), 1) < DH_NA


def _split_pair(qpair, low):
    zero = jnp.zeros_like(qpair)
    return jnp.where(low, qpair, zero), jnp.where(low, zero, qpair)


def _ctx_attn_kernel(qna_ref, kna_ref, vna_ref, qp_ref, kp_ref, vm_ref, o_ref):
    low = _low_lanes()
    outs = []
    for hp in range(H_NA // 2):
        sl = slice(hp * LANE, (hp + 1) * LANE)
        kpair = kna_ref[:, sl]
        vpair = vna_ref[:, sl]
        res = []
        for qh in _split_pair(qna_ref[:, sl], low):
            (p,), den = _softmax_parts([_dot_nt(qh, kpair)])
            res.append(_dot(p.astype(BF16), vpair) / den)
        outs.append(jnp.where(low, res[0], res[1]))
    for hp in range(H_MLA // 2):
        vpair = vm_ref[:, hp * LANE:(hp + 1) * LANE]
        res = []
        for h in (2 * hp, 2 * hp + 1):
            sl = slice(h * HP, (h + 1) * HP)
            (p,), den = _softmax_parts([_dot_nt(qp_ref[:, sl], kp_ref[:, sl])], MLA_SCALE)
            res.append(_dot(p.astype(BF16), vpair) / den)
        outs.append(jnp.where(low, res[0], res[1]))
    o_ref[...] = jnp.concatenate(outs, axis=1).astype(BF16)


def _ctx_attn(qna, kna, vna, qp, kp, vm):
    tok = lambda n: pl.BlockSpec((S_CTX, n), lambda i: (i, 0))
    return pl.pallas_call(
        _ctx_attn_kernel,
        grid=(B_CTX,),
        in_specs=[tok(D_NA), tok(D_NA), tok(D_NA), tok(D_QP), tok(D_QP), tok(D_MLA)],
        out_specs=tok(D_NA + D_MLA),
        out_shape=jax.ShapeDtypeStruct((N_CTX, D_NA + D_MLA), BF16),
        compiler_params=_params(("arbitrary",)),
        name="ctx_attn",
    )(qna, kna, vna, qp, kp, vm)


def _na_kernel(q_ref, k_ref, v_ref, kc_ref, vc_ref, bias_ref, o_ref):
    t = pl.program_id(1)
    row0 = jnp.clip(t * NA_ROWS_PER_TILE - WIN_H // 2, 0, GRID_H - NA_KEY_ROWS)
    ks = pl.multiple_of(row0 * GRID_W, GRID_W)
    low = _low_lanes()
    outs = []
    for hp in range(H_NA // 2):
        sl = slice(hp * LANE, (hp + 1) * LANE)
        kwin = k_ref[pl.ds(ks, NA_KEYS), sl]
        vwin = v_ref[pl.ds(ks, NA_KEYS), sl]
        kctx = kc_ref[0, 0, :, sl].astype(BF16)
        vctx = vc_ref[0, 0, :, sl].astype(BF16)
        res = []
        for half, qh in enumerate(_split_pair(q_ref[:, sl], low)):
            s_win = _dot_nt(qh, kwin) + bias_ref[0, 0, 2 * hp + half]
            s_ctx = _dot_nt(qh, kctx)
            (p_win, p_ctx), den = _softmax_parts([s_win, s_ctx])
            o = _dot(p_win.astype(BF16), vwin) + _dot(p_ctx.astype(BF16), vctx)
            res.append(o / den)
        outs.append(jnp.where(low, res[0], res[1]))
    o_ref[...] = jnp.concatenate(outs, axis=1).astype(BF16)


def _na_attn(l, qna, kna, vna, cache_k, cache_v, bias):
    nt = Q_TILES_PER_SEQ
    seq = lambda n: pl.BlockSpec((S_LAT, n), lambda b, t: (N_CTX // S_LAT + b, 0))
    cache = pl.BlockSpec((1, 1, PAST, D_NA), lambda b, t: (b, l, 0, 0))
    kind = lambda t: jnp.where(t == 0, 0, jnp.where(t == nt - 1, 2, 1))
    return pl.pallas_call(
        _na_kernel,
        grid=(B_LAT, nt),
        in_specs=[pl.BlockSpec((TM, D_NA), lambda b, t: (N_CTX // TM + b * nt + t, 0)),
                  seq(D_NA), seq(D_NA), cache, cache,
                  pl.BlockSpec((1, 1, H_NA, TM, NA_KEYS), lambda b, t: (l, kind(t), 0, 0, 0))],
        out_specs=pl.BlockSpec((TM, D_NA), lambda b, t: (b * nt + t, 0)),
        out_shape=jax.ShapeDtypeStruct((N_LAT, D_NA), BF16),
        compiler_params=_params(("arbitrary", "arbitrary")),
        name="na_attn",
    )(qna, kna, vna, cache_k, cache_v, bias)


def _na_bias_tables(rpb):
    n_dr, n_dc = 2 * WIN_H - 1, 2 * WIN_W - 1
    qc = np.arange(GRID_W)[:, None]
    kc = np.arange(GRID_W)[None, :]
    ws = np.clip(qc - WIN_W // 2, 0, GRID_W - WIN_W)
    col_ok = (kc >= ws) & (kc < ws + WIN_W)
    col_hot = ((kc - qc + WIN_W - 1)[..., None] == np.arange(n_dc)) & col_ok[..., None]
    col_hot = col_hot.reshape(GRID_W * GRID_W, n_dc).astype(np.float32)
    by_col = jnp.einsum("pb,lhab->lhap", jnp.asarray(col_hot), rpb, precision=lax.Precision.HIGHEST)
    by_col = jnp.where(jnp.asarray(col_ok), by_col.reshape(DEPTH, H_NA, n_dr, GRID_W, GRID_W), -jnp.inf)
    pad = jnp.full((DEPTH, H_NA, _BIAS_PAD, GRID_W, GRID_W), -jnp.inf, F32)
    blocks = jnp.concatenate([pad, by_col, pad], axis=2)
    pairs = jnp.concatenate([blocks[:, :, :-1], blocks[:, :, 1:]], axis=-1)
    n_pairs = n_dr + 2 * _BIAS_PAD - 1
    return pl.pallas_call(
        _na_bias_kernel,
        grid=(DEPTH, H_NA),
        in_specs=[pl.BlockSpec((1, 1, n_pairs, GRID_W, LANE), lambda l, h: (l, h, 0, 0, 0))],
        out_specs=pl.BlockSpec((1, len(_NA_KINDS), 1, TM, NA_KEYS), lambda l, h: (l, 0, h, 0, 0)),
        out_shape=jax.ShapeDtypeStruct((DEPTH, len(_NA_KINDS), H_NA, TM, NA_KEYS), F32),
        compiler_params=_params(("arbitrary", "arbitrary")),
        name="na_bias",
    )(pairs)


_NA_KINDS = ((0, 0), (NA_ROWS_PER_TILE, 0), (GRID_H - NA_ROWS_PER_TILE, GRID_H - NA_KEY_ROWS))
_BIAS_PAD = 4


def _na_bias_kernel(pairs_ref, o_ref):
    low = lax.broadcasted_iota(jnp.int32, (GRID_W, LANE), 1) < GRID_W
    ninf = jnp.full((GRID_W, LANE), -jnp.inf, F32)
    for kind, (r0, k0) in enumerate(_NA_KINDS):
        for i in range(NA_ROWS_PER_TILE):
            r = r0 + i
            rs = min(max(r - WIN_H // 2, 0), GRID_H - WIN_H)
            for m in range(NA_KEY_ROWS // 2):
                kr = k0 + 2 * m
                ok0 = rs <= kr < rs + WIN_H
                ok1 = rs <= kr + 1 < rs + WIN_H
                blk = pairs_ref[0, 0, kr - r + WIN_H - 1 + _BIAS_PAD]
                if ok0 and not ok1:
                    blk = jnp.where(low, blk, ninf)
                elif ok1 and not ok0:
                    blk = jnp.where(low, ninf, blk)
                elif not ok0:
                    blk = ninf
                o_ref[0, kind, 0, i * GRID_W:(i + 1) * GRID_W, m * LANE:(m + 1) * LANE] = blk


def _mla_kernel(q_ref, k_ref, v_ref, kc_ref, vc_ref, o_ref):
    low = _low_lanes()
    outs = []
    for hp in range(H_MLA // 2):
        vsl = slice(hp * LANE, (hp + 1) * LANE)
        vlat = v_ref[:, vsl]
        vctx = vc_ref[0, 0, :, vsl]
        res = []
        for h in (2 * hp, 2 * hp + 1):
            sl = slice(h * HP, (h + 1) * HP)
            q = q_ref[:, sl]
            s_ctx = _dot_nt(q, kc_ref[0, 0, :, sl])
            s_lat = _dot_nt(q, k_ref[:, sl])
            (p_ctx, p_lat), den = _softmax_parts([s_ctx, s_lat], MLA_SCALE)
            o = _dot(p_ctx.astype(BF16), vctx) + _dot(p_lat.astype(BF16), vlat)
            res.append(o / den)
        outs.append(jnp.where(low, res[0], res[1]))
    o_ref[...] = jnp.concatenate(outs, axis=1).astype(BF16)


def _mla_attn(l, qp, kp, vm, kc, vc):
    nt = Q_TILES_PER_SEQ
    seq = lambda n: pl.BlockSpec((S_LAT, n), lambda b, t: (N_CTX // S_LAT + b, 0))
    return pl.pallas_call(
        _mla_kernel,
        grid=(B_LAT, nt),
        in_specs=[pl.BlockSpec((TM, D_QP), lambda b, t: (N_CTX // TM + b * nt + t, 0)),
                  seq(D_QP), seq(D_MLA),
                  pl.BlockSpec((1, 1, PAST, D_QP), lambda b, t: (l, b, 0, 0)),
                  pl.BlockSpec((1, 1, PAST, D_MLA), lambda b, t: (l, b, 0, 0))],
        out_specs=pl.BlockSpec((TM, D_MLA), lambda b, t: (b * nt + t, 0)),
        out_shape=jax.ShapeDtypeStruct((N_LAT, D_MLA), BF16),
        compiler_params=_params(("arbitrary", "arbitrary")),
        name="mla_attn",
    )(qp, kp, vm, kc, vc)


def _route(sel, aff):
    rows = lambda a, g: [a[EPG * g + e:EPG * g + e + 1, :] for e in range(EPG)]
    best = None
    for g in range(N_GRP):
        v = rows(sel, g)
        score = v[0] + v[1]
        for a, b in ((0, 2), (0, 3), (1, 2), (1, 3), (2, 3)):
            score = jnp.maximum(score, v[a] + v[b])
        if best is None:
            best, grp = score, jnp.zeros_like(score)
        else:
            better = score > best
            best = jnp.where(better, score, best)
            grp = jnp.where(better, float(g), grp)
    zero = jnp.zeros_like(best)
    sv = [zero] * EPG
    av = [zero] * EPG
    for g in range(N_GRP):
        hit = grp == float(g)
        sg, ag = rows(sel, g), rows(aff, g)
        sv = [jnp.where(hit, sg[e], sv[e]) for e in range(EPG)]
        av = [jnp.where(hit, ag[e], av[e]) for e in range(EPG)]

    def argmax_first(vals, skip=None):
        bv, bi = None, None
        for e in range(EPG):
            v = vals[e] if skip is None else jnp.where(skip == float(e), -jnp.inf, vals[e])
            if bv is None:
                bv, bi = v, jnp.zeros_like(v)
            else:
                better = v > bv
                bv = jnp.where(better, v, bv)
                bi = jnp.where(better, float(e), bi)
        return bi

    i1 = argmax_first(sv)
    i2 = argmax_first(sv, skip=i1)
    lo = jnp.minimum(i1, i2)
    hi = jnp.maximum(i1, i2)
    pick = lambda idx: sum(jnp.where(idx == float(e), av[e], 0.0) for e in range(EPG))
    a_lo, a_hi = pick(lo), pick(hi)
    den = a_lo + a_hi
    pair = jnp.where(lo == 0.0, 0.0, jnp.where(lo == 1.0, 3.0, 5.0)) + hi - lo - 1.0
    return grp * float(N_PAIR) + pair, a_lo / den, a_hi / den


def _post_kernel(x_ref, octx_ref, ona_ref, omla_ref, mod_ref, n2_ref, wout_ref, wr_ref, br_ref,
                 x1_ref, h2_ref, route_ref, counts_ref, carry):
    i = pl.program_id(0)

    @pl.when(i == 0)
    def _():
        carry[...] = jnp.zeros_like(carry)

    is_ctx = i < CTX_TILES
    o_lat = jnp.concatenate([ona_ref[...], omla_ref[...]], axis=1)
    o = jnp.where(is_ctx, octx_ref[...], o_lat)
    m = mod_ref[0]
    x1 = x_ref[...] + m[2:3, :] * _dot(o, wout_ref[0])
    x1_ref[...] = x1
    h2 = _rms(x1, n2_ref[0]) * (1.0 + m[4:5, :]) + m[3:4, :]
    logits = lax.dot_general(wr_ref[...], h2, (((1,), (1,)), ((), ())),
                             preferred_element_type=F32, precision=lax.Precision.HIGHEST)
    aff = jax.nn.sigmoid(logits)
    bucket, g_lo, g_hi = _route(aff + br_ref[...], aff)

    hit = lax.broadcasted_iota(jnp.int32, (BUCKET_ROWS, TT), 0).astype(F32) == bucket
    earlier = lax.broadcasted_iota(jnp.int32, (TT, TT), 0) <= lax.broadcasted_iota(jnp.int32, (TT, TT), 1)
    incl = _dot(jnp.where(hit, 1.0, 0.0).astype(BF16), jnp.where(earlier, 1.0, 0.0).astype(BF16))
    base = carry[...]
    rank = jnp.sum(jnp.where(hit, incl + base[:, 0:1], 0.0), axis=0, keepdims=True) - 1.0
    carry[...] = base + incl[:, TT - 1:TT]
    counts_ref[...] = carry[...]
    route_ref[0] = jnp.concatenate([bucket, rank, jnp.zeros((6, TT), F32)], axis=0)

    gate_cols = jnp.transpose(jnp.concatenate([g_lo, g_hi, jnp.zeros((6, TT), F32)], axis=0))
    tail = jnp.zeros((TT, (X_CHUNKS - GATE_CHUNK) * LANE - 8), F32)
    _to_rows(h2_ref, jnp.concatenate([h2, gate_cols, tail], axis=1), X_CHUNKS)


def _post(l, x, o_ctx, o_na, o_mla, mods, wts):
    tok = lambda n: pl.BlockSpec((TT, n), lambda i: (i, 0))
    lat = lambda n: pl.BlockSpec((TT, n), lambda i: (jnp.maximum(i - CTX_TILES, 0), 0))
    lay = lambda a: pl.BlockSpec((1,) + a.shape[1:], lambda i: (l,) + (0,) * (a.ndim - 1))
    full = lambda a: pl.BlockSpec(a.shape, lambda i: (0,) * a.ndim)
    return pl.pallas_call(
        _post_kernel,
        grid=(N_TILES,),
        in_specs=[tok(D), pl.BlockSpec((TT, D), lambda i: (_ctx_block(i), 0)), lat(D_NA), lat(D_MLA),
                  pl.BlockSpec((1, 6, D), lambda i: (l * N_MOD_ROWS + _mod_row(i), 0, 0)),
                  lay(wts["n2"]), lay(wts["w_out"]), full(wts["w_rt"]), full(wts["b_r"])],
        out_specs=[tok(D), pl.BlockSpec((TT * X_CHUNKS, LANE), lambda i: (i, 0)),
                   pl.BlockSpec((1, 8, TT), lambda i: (i, 0, 0)),
                   pl.BlockSpec((BUCKET_ROWS, LANE), lambda i: (0, 0))],
        out_shape=[jax.ShapeDtypeStruct((N_TOK, D), F32),
                   jax.ShapeDtypeStruct((N_TOK * X_CHUNKS, LANE), F32),
                   jax.ShapeDtypeStruct((N_TILES, 8, TT), F32),
                   jax.ShapeDtypeStruct((BUCKET_ROWS, LANE), F32)],
        scratch_shapes=[pltpu.VMEM((BUCKET_ROWS, LANE), F32)],
        compiler_params=_params(("arbitrary",)),
        name="post",
    )(x, o_ctx, o_na, o_mla, mods, wts["n2"], wts["w_out"], wts["w_rt"], wts["b_r"])


def _moe_kernel(elo_ref, ehi_ref, nv_ref, total_ref, slot_ref,
                h2_hbm, wg1, wu1, wd1, wg2, wu2, wd2, out_hbm,
                xbuf, ybuf, src, dst, gsem, ssem):
    del elo_ref, ehi_ref
    t = pl.program_id(0)
    total = total_ref[0]
    par = t % 2

    def start_gather(tile, p):
        def body(j, c):
            row = pl.multiple_of(src[tile * TM + j] * X_CHUNKS, X_CHUNKS)
            pltpu.make_async_copy(h2_hbm.at[pl.ds(row, X_CHUNKS), :],
                                  xbuf.at[p, pl.ds(j * X_CHUNKS, X_CHUNKS), :], gsem.at[p]).start()
            return c
        lax.fori_loop(0, TM, body, 0, unroll=DMA_UNROLL)

    def start_scatter(tile, p):
        def body(j, c):
            row = pl.multiple_of(dst[tile * TM + j] * Y_CHUNKS, Y_CHUNKS)
            pltpu.make_async_copy(ybuf.at[p, pl.ds(j * Y_CHUNKS, Y_CHUNKS), :],
                                  out_hbm.at[pl.ds(row, Y_CHUNKS), :], ssem.at[p]).start()
            return c
        lax.fori_loop(0, TM, body, 0, unroll=DMA_UNROLL)

    def wait_gather(p):
        pltpu.make_async_copy(h2_hbm.at[pl.ds(0, TM * X_CHUNKS), :], xbuf.at[p], gsem.at[p]).wait()

    def wait_scatter(p):
        pltpu.make_async_copy(ybuf.at[p], out_hbm.at[pl.ds(0, TM * Y_CHUNKS), :], ssem.at[p]).wait()

    @pl.when(t == 0)
    def _():
        def invert(i, c):
            s = slot_ref[i]
            src[s] = i
            dst[s] = i
            return c
        lax.fori_loop(0, N_TOK, invert, 0, unroll=DMA_UNROLL)

        def pad_tile(tile, c):
            def pad_slot(j, c2):
                src[tile * TM + j] = src[tile * TM]
                dst[tile * TM + j] = N_TOK + j
                return c2
            return lax.fori_loop(nv_ref[tile], TM, pad_slot, c)
        lax.fori_loop(0, total, pad_tile, 0)
        ybuf[0] = jnp.zeros((TM * Y_CHUNKS, LANE), F32)
        spare = pltpu.make_async_copy(ybuf.at[0], out_hbm.at[pl.ds(N_TOK * Y_CHUNKS, TM * Y_CHUNKS), :], ssem.at[0])
        spare.start()
        spare.wait()
        start_gather(0, 0)

    @pl.when(t + 1 < total)
    def _():
        start_gather(t + 1, 1 - par)

    @pl.when(t < total)
    def _():
        wait_gather(par)

        @pl.when(t >= 2)
        def _():
            wait_scatter(par)

        x = _from_rows(xbuf, TM, X_CHUNKS, 0, Y_CHUNKS, lead=(par,)).astype(BF16)
        gates = _from_rows(xbuf, TM, X_CHUNKS, GATE_CHUNK, 1, lead=(par,))
        y = None
        for k, (wg, wu, wd) in enumerate(((wg1, wu1, wd1), (wg2, wu2, wd2))):
            a = _dot(x, wg[0, 0])
            hid = a * jax.nn.sigmoid(a) * _dot(x, wu[0, 0]) * gates[:, k:k + 1]
            part = _dot(hid.astype(BF16), wd[0, 0])
            y = part if y is None else y + part
        _to_rows(ybuf, y, Y_CHUNKS, lead=(par,))
        start_scatter(t, par)

        @pl.when(t == total - 1)
        def _():
            @pl.when(t >= 1)
            def _():
                wait_scatter(1 - par)
            wait_scatter(par)


def _moe(l, h2, e_lo, e_hi, n_valid, total, slot, wts):
    up = lambda which: pl.BlockSpec((1, 1, D, D_EXP), lambda t, *pf: (l, pf[which][t], 0, 0))
    down = lambda which: pl.BlockSpec((1, 1, D_EXP, D), lambda t, *pf: (l, pf[which][t], 0, 0))
    first, second = 0, 1
    return pl.pallas_call(
        _moe_kernel,
        grid_spec=pltpu.PrefetchScalarGridSpec(
            num_scalar_prefetch=5,
            grid=(MOE_TILES,),
            in_specs=[pl.BlockSpec(memory_space=pl.ANY),
                      up(first), up(first), down(first), up(second), up(second), down(second)],
            out_specs=pl.BlockSpec(memory_space=pl.ANY),
            scratch_shapes=[pltpu.VMEM((2, TM * X_CHUNKS, LANE), F32), pltpu.VMEM((2, TM * Y_CHUNKS, LANE), F32),
                            pltpu.SMEM((MOE_TILES * TM,), jnp.int32), pltpu.SMEM((MOE_TILES * TM,), jnp.int32),
                            pltpu.SemaphoreType.DMA((2,)), pltpu.SemaphoreType.DMA((2,))],
        ),
        out_shape=jax.ShapeDtypeStruct(((N_TOK + TM) * Y_CHUNKS, LANE), F32),
        compiler_params=_params(("arbitrary",)),
        name="moe",
    )(e_lo, e_hi, n_valid, total, slot, h2,
      wts["w_gate"], wts["w_up"], wts["w_down"], wts["w_gate"], wts["w_up"], wts["w_down"])


def _moe_schedule(route, counts):
    bucket = route[:, 0, :].reshape(N_TOK).astype(jnp.int32)
    rank = route[:, 1, :].reshape(N_TOK).astype(jnp.int32)
    counts = counts[:N_BUCKET, 0].astype(jnp.int32)
    tiles = (counts + TM - 1) // TM
    tile_end = jnp.cumsum(tiles)
    tile_start = tile_end - tiles
    buckets = jnp.arange(N_BUCKET, dtype=jnp.int32)
    first_slot = jnp.where(bucket[:, None] == buckets[None, :], (tile_start * TM)[None, :], 0).sum(axis=1)
    slot = first_slot + rank
    t = jnp.arange(MOE_TILES, dtype=jnp.int32)
    total = tile_end[-1]
    bucket_of = lambda tile: (tile_end[None, :] <= tile[:, None]).astype(jnp.int32).sum(axis=1)
    tb = bucket_of(jnp.minimum(t, total - 1))
    n_valid = jnp.where(t < total, jnp.clip(counts[tb] - (t - tile_start[tb]) * TM, 0, TM), 0).astype(jnp.int32)
    grp = tb // N_PAIR
    e_lo = grp * EPG + jnp.asarray(_PAIR_LO)[tb % N_PAIR]
    e_hi = grp * EPG + jnp.asarray(_PAIR_HI)[tb % N_PAIR]
    return e_lo, e_hi, n_valid, total.reshape(1), slot


def _final_kernel(x1_ref, moe_ref, mod_ref, g_ref, yc_ref, yl_ref):
    i = pl.program_id(0)
    moe = _from_rows(moe_ref, TT, Y_CHUNKS, 0, Y_CHUNKS)
    y = _rms(x1_ref[...] + mod_ref[0, 5:6, :] * moe, g_ref[...])

    @pl.when(i < CTX_TILES)
    def _():
        yc_ref[...] = y

    @pl.when(i >= CTX_TILES)
    def _():
        yl_ref[...] = y


def _final(x1, moe, mods, g):
    tok = pl.BlockSpec((TT, D), lambda i: (i, 0))
    l = DEPTH - 1
    return pl.pallas_call(
        _final_kernel,
        grid=(N_TILES,),
        in_specs=[tok, pl.BlockSpec((TT * Y_CHUNKS, LANE), lambda i: (i, 0)),
                  pl.BlockSpec((1, 6, D), lambda i: (l * N_MOD_ROWS + _mod_row(i), 0, 0)),
                  pl.BlockSpec((1, D), lambda i: (0, 0))],
        out_specs=[pl.BlockSpec((TT, D), lambda i: (_ctx_block(i), 0)),
                   pl.BlockSpec((TT, D), lambda i: (jnp.maximum(i - CTX_TILES, 0), 0))],
        out_shape=[jax.ShapeDtypeStruct((N_CTX, D), F32), jax.ShapeDtypeStruct((N_LAT, D), F32)],
        compiler_params=_params(("arbitrary",)),
        name="final",
    )(x1, moe, mods, g)


def _swap_halves(w):
    q = ROPE // 4
    return jnp.concatenate([-w[..., q:2 * q], w[..., :q], -w[..., 3 * q:], w[..., 2 * q:3 * q]], axis=-1)


def _pad_head(nope, rope):
    z = jnp.zeros(rope.shape[:-1] + (HP - NOPE - ROPE,), rope.dtype)
    return jnp.concatenate([nope, rope, z], axis=-1)


def _prepare_weights(norm1_g, norm2_g, w_in, q_norm_g, w_uq, kv_norm_g, w_ukv, w_out, w_router, b_router,
                     w_gate, w_up, w_down):
    o = 3 * D_NA + Q_LORA + KV_LORA
    w_kr = w_in[:, :, o:]
    zeros_nope = jnp.zeros((DEPTH, D, NOPE), F32)
    w_in_p = jnp.concatenate([w_in[:, :, :o], _pad_head(zeros_nope, w_kr),
                              _pad_head(zeros_nope, _swap_halves(w_kr))], axis=-1)
    wq = w_uq.reshape(DEPTH, Q_LORA, H_MLA, NOPE + ROPE)
    plain = _pad_head(wq[..., :NOPE], wq[..., NOPE:]).reshape(DEPTH, Q_LORA, D_QP)
    swapped = _pad_head(jnp.zeros_like(wq[..., :NOPE]), _swap_halves(wq[..., NOPE:])).reshape(DEPTH, Q_LORA, D_QP)
    wkv = w_ukv.reshape(DEPTH, KV_LORA, H_MLA, NOPE + VH)
    w_uk_p = jnp.concatenate([wkv[..., :NOPE], jnp.zeros((DEPTH, KV_LORA, H_MLA, HP - NOPE), F32)],
                             axis=-1).reshape(DEPTH, KV_LORA, D_QP)
    w_uv = wkv[..., NOPE:].reshape(DEPTH, KV_LORA, D_MLA)
    return {
        "n1": norm1_g.reshape(DEPTH, 1, D),
        "n2": norm2_g.reshape(DEPTH, 1, D),
        "w_in": w_in_p.astype(BF16),
        "qn": q_norm_g.reshape(DEPTH, 1, Q_LORA),
        "w_uq": jnp.concatenate([plain, swapped], axis=-1).astype(BF16),
        "kvn": kv_norm_g.reshape(DEPTH, 1, KV_LORA),
        "w_uk": w_uk_p.astype(BF16),
        "w_uv": w_uv.astype(BF16),
        "w_out": w_out.astype(BF16),
        "w_rt": w_router.T,
        "b_r": b_router.reshape(N_EXP, 1),
        "w_gate": w_gate.astype(BF16),
        "w_up": w_up.astype(BF16),
        "w_down": w_down.astype(BF16),
    }


def _rope_tables():
    half = ROPE // 2
    freqs = 1.0 / (ROPE_THETA ** (np.arange(0, half, 2, dtype=np.float32) / half))
    pos = np.arange(S_LAT)
    ang_r = (pos // GRID_W).astype(np.float32)[:, None] * freqs
    ang_c = (pos % GRID_W).astype(np.float32)[:, None] * freqs
    ang = np.concatenate([ang_r, ang_r, ang_c, ang_c], axis=1)
    cos = np.ones((TT + S_LAT, LANE), np.float32)
    sin = np.zeros((TT + S_LAT, LANE), np.float32)
    cos[TT:, NOPE:NOPE + ROPE] = np.cos(ang)
    sin[TT:, NOPE:NOPE + ROPE] = np.sin(ang)
    return jnp.asarray(cos), jnp.asarray(sin)


def kernel(x_prompt, x_sample, cache_nat_k, cache_nat_v, cache_mla_ckv, cache_mla_krope, c, c_ctx, norm1_g, norm2_g, w_ada, b_ada, w_in, q_norm_g, w_uq, kv_norm_g, w_ukv, rpb, w_out, w_router, b_router, w_gate, w_up, w_down, final_norm_g):
    wts = _prepare_weights(norm1_g, norm2_g, w_in, q_norm_g, w_uq, kv_norm_g, w_ukv, w_out, w_router, b_router,
                           w_gate, w_up, w_down)
    tabs = _rope_tables()
    cond = jnp.concatenate([c_ctx[None, :], c, jnp.zeros((N_MOD_ROWS - 1 - B_LAT, D), F32)], axis=0)
    mods = _ada(cond, w_ada, b_ada).reshape(DEPTH * N_MOD_ROWS, 6, D)
    place = jnp.asarray(np.eye(ROPE, HP, k=NOPE, dtype=np.float32)).astype(BF16)
    kc_all, vc_all = _ctxkv(cache_mla_ckv, cache_mla_krope, wts["w_uk"], wts["w_uv"], place)
    cache_k = cache_nat_k.reshape(B_LAT, DEPTH, PAST, D_NA)
    cache_v = cache_nat_v.reshape(B_LAT, DEPTH, PAST, D_NA)
    bias = _na_bias_tables(rpb)

    x1 = moe = None
    new_k, new_v, new_ckv, new_kr = [], [], [], []
    for l in range(DEPTH):
        xin = (x_prompt.reshape(N_CTX, D), x_sample.reshape(N_LAT, D)) if l == 0 else (x1, moe)
        x, qna, kna, vna, qp, kp, vm, k_new, v_new, ckv_new, kr_new = _pre(l, l == 0, xin, mods, wts, tabs)
        new_k.append(k_new)
        new_v.append(v_new)
        new_ckv.append(ckv_new)
        new_kr.append(kr_new)
        o_ctx = _ctx_attn(qna, kna, vna, qp, kp, vm)
        o_na = _na_attn(l, qna, kna, vna, cache_k, cache_v, bias)
        o_mla = _mla_attn(l, qp, kp, vm, kc_all, vc_all)
        x1, h2, route, counts = _post(l, x, o_ctx, o_na, o_mla, mods, wts)
        moe = _moe(l, h2, *_moe_schedule(route, counts), wts)
    y_ctx, y_lat = _final(x1, moe, mods, final_norm_g.reshape(1, D))

    stack = lambda parts, tail: jnp.stack([p.reshape((B_CTX, S_CTX) + tail) for p in parts], axis=1)
    return (y_ctx.reshape(B_CTX, S_CTX, D), y_lat.reshape(B_LAT, S_LAT, D),
            stack(new_k, (H_NA, DH_NA)), stack(new_v, (H_NA, DH_NA)),
            stack(new_ckv, (KV_LORA,)), stack(new_kr, (ROPE,)))
```

```python
import functools

import numpy as np
import jax
import jax.numpy as jnp
from jax import lax
from jax.experimental import pallas as pl
from jax.experimental.pallas import tpu as pltpu

F32 = jnp.float32
BF16 = jnp.bfloat16

D = 1024
B_CTX, S_CTX = 32, 256
B_LAT, S_LAT = 8, 2048
DEPTH = 4
PAST = 256
GRID_W, GRID_H = 64, 32
H_NA, DH_NA = 8, 64
WIN_H, WIN_W = 8, 16
H_MLA, NOPE, ROPE, VH = 8, 64, 32, 64
Q_LORA, KV_LORA = 256, 128
D_NA = H_NA * DH_NA
D_MLA = H_MLA * VH
N_EXP, N_GRP, EPG = 16, 4, 4
D_EXP = 512
ROPE_THETA = 10000.0
EPS = 1e-6
NA_SCALE = DH_NA ** -0.5
MLA_SCALE = (NOPE + ROPE) ** -0.5
LOG2E = 1.4426950408889634

N_CTX = B_CTX * S_CTX
N_LAT = B_LAT * S_LAT
N_TOK = N_CTX + N_LAT
TM = 256
TT = 512
CTX_TILES = N_CTX // TT
LAT_TILES = N_LAT // TT
N_TILES = CTX_TILES + LAT_TILES
LAT_TILES_PER_SEQ = S_LAT // TT
Q_TILES_PER_SEQ = S_LAT // TM
LANE = 128
HP = LANE
D_QP = H_MLA * HP
W_IN_COLS = 3 * D_NA + Q_LORA + KV_LORA + 2 * LANE
N_MOD_ROWS = 16
N_PAIR = 6
N_BUCKET = N_GRP * N_PAIR
MOE_TILES = N_TOK // TM + N_BUCKET
BUCKET_ROWS = 32
Y_CHUNKS = D // LANE
X_CHUNKS = Y_CHUNKS
GATE_CHUNK = D // (2 * LANE)
U32 = jnp.uint32


def _pack_bf16_pairs(x):
    half = x.shape[1] // 2
    bits = lambda v: lax.bitcast_convert_type(v.astype(BF16).astype(F32), U32)
    return bits(x[:, :half]) | (bits(x[:, half:]) >> 16)


def _unpack_bf16_pairs(w):
    hi = lax.bitcast_convert_type(w & jnp.uint32(0xFFFF0000), F32)
    lo = lax.bitcast_convert_type(w << 16, F32)
    return jnp.concatenate([hi, lo], axis=1).astype(BF16)


def _to_rows(ref, value, chunks, lead=()):
    t = value.shape[0]
    for c in range(value.shape[1] // LANE):
        ref[lead + (pl.ds(c, t, stride=chunks), slice(None))] = value[:, c * LANE:(c + 1) * LANE]


def _from_rows(ref, t, chunks, first, n, lead=()):
    parts = [ref[lead + (pl.ds(first + c, t, stride=chunks), slice(None))] for c in range(n)]
    return parts[0] if n == 1 else jnp.concatenate(parts, axis=1)
NA_ROWS_PER_TILE = TM // GRID_W
NA_KEY_ROWS = 12
NA_KEYS = NA_KEY_ROWS * GRID_W
VMEM_LIMIT = 48 * 1024 * 1024
DMA_UNROLL = 8

_PAIR_LO = np.array([0, 0, 0, 1, 1, 2], np.int32)
_PAIR_HI = np.array([1, 2, 3, 2, 3, 3], np.int32)


def _dot(a, b):
    return jnp.dot(a, b, preferred_element_type=F32)


def _dot_nt(a, b):
    return lax.dot_general(a, b, (((1,), (1,)), ((), ())), preferred_element_type=F32)


def _rms(x, g):
    return x * lax.rsqrt(jnp.mean(x * x, axis=-1, keepdims=True) + EPS) * g


def _mod_row(i):
    return jnp.where(i < CTX_TILES, 0, 1 + (i - CTX_TILES) // LAT_TILES_PER_SEQ)


def _pos_block(i):
    return jnp.where(i < CTX_TILES, 0, 1 + (i - CTX_TILES) % LAT_TILES_PER_SEQ)


def _ctx_block(i):
    return jnp.minimum(i, CTX_TILES - 1)


def _params(sem):
    return pltpu.CompilerParams(dimension_semantics=sem, vmem_limit_bytes=VMEM_LIMIT)


def _ada_kernel(cond_ref, w_ref, b_ref, o_ref):
    cond = cond_ref[...]
    act = cond * jax.nn.sigmoid(cond)
    o_ref[0] = jnp.dot(act, w_ref[0], preferred_element_type=F32,
                       precision=lax.Precision.HIGHEST) + b_ref[0]


def _ada(cond, w_ada, b_ada):
    nt = 1536
    return pl.pallas_call(
        _ada_kernel,
        grid=(DEPTH, 6 * D // nt),
        in_specs=[pl.BlockSpec((N_MOD_ROWS, D), lambda l, j: (0, 0)),
                  pl.BlockSpec((1, D, nt), lambda l, j: (l, 0, j)),
                  pl.BlockSpec((1, 1, nt), lambda l, j: (l, 0, j))],
        out_specs=pl.BlockSpec((1, N_MOD_ROWS, nt), lambda l, j: (l, 0, j)),
        out_shape=jax.ShapeDtypeStruct((DEPTH, N_MOD_ROWS, 6 * D), F32),
        compiler_params=_params(("arbitrary", "arbitrary")),
        name="ada",
    )(cond, w_ada, b_ada.reshape(DEPTH, 1, 6 * D))


def _ctxkv_kernel(ckv_ref, kr_ref, wuk_ref, wuv_ref, place_ref, kc_ref, vc_ref):
    ckv = ckv_ref[0, 0].astype(BF16)
    kr = _dot(kr_ref[0, 0].astype(BF16), place_ref[...])
    kc_ref[0, 0] = (_dot(ckv, wuk_ref[0]) + jnp.tile(kr, (1, H_MLA))).astype(BF16)
    vc_ref[0, 0] = _dot(ckv, wuv_ref[0]).astype(BF16)


def _ctxkv(cache_ckv, cache_krope, w_uk_p, w_uv, place):
    return pl.pallas_call(
        _ctxkv_kernel,
        grid=(DEPTH, B_LAT),
        in_specs=[pl.BlockSpec((1, 1, PAST, KV_LORA), lambda l, b: (b, l, 0, 0)),
                  pl.BlockSpec((1, 1, PAST, ROPE), lambda l, b: (b, l, 0, 0)),
                  pl.BlockSpec((1, KV_LORA, D_QP), lambda l, b: (l, 0, 0)),
                  pl.BlockSpec((1, KV_LORA, D_MLA), lambda l, b: (l, 0, 0)),
                  pl.BlockSpec((ROPE, HP), lambda l, b: (0, 0))],
        out_specs=[pl.BlockSpec((1, 1, PAST, D_QP), lambda l, b: (l, b, 0, 0)),
                   pl.BlockSpec((1, 1, PAST, D_MLA), lambda l, b: (l, b, 0, 0))],
        out_shape=[jax.ShapeDtypeStruct((DEPTH, B_LAT, PAST, D_QP), BF16),
                   jax.ShapeDtypeStruct((DEPTH, B_LAT, PAST, D_MLA), BF16)],
        compiler_params=_params(("arbitrary", "arbitrary")),
        name="ctxkv",
    )(cache_ckv, cache_krope, w_uk_p, w_uv, place)


def _pre_kernel(first, *refs):
    if first:
        xc_ref, xl_ref, *rest = refs
    else:
        x1_ref, moe_ref, modp_ref, *rest = refs
    (mod_ref, n1_ref, win_ref, qn_ref, wuq_ref, kvn_ref, wuk_ref, wuv_ref, cos_ref, sin_ref,
     xo_ref, qna_ref, kna_ref, vna_ref, qp_ref, kp_ref, vm_ref, knew_ref, vnew_ref, ckvnew_ref, krnew_ref) = rest
    i = pl.program_id(0)

    if first:
        x = jnp.where(i < CTX_TILES, xc_ref[...], xl_ref[...])
    else:
        x = x1_ref[...] + modp_ref[0, 5:6, :] * _from_rows(moe_ref, TT, Y_CHUNKS, 0, Y_CHUNKS)
    xo_ref[...] = x
    m = mod_ref[0]
    h = _rms(x, n1_ref[0]) * (1.0 + m[1:2, :]) + m[0:1, :]
    proj = _dot(h.astype(BF16), win_ref[0])
    k_na = proj[:, D_NA:2 * D_NA]
    v_na = proj[:, 2 * D_NA:3 * D_NA]
    qna_ref[...] = (proj[:, :D_NA] * NA_SCALE).astype(BF16)
    kna_ref[...] = k_na.astype(BF16)
    vna_ref[...] = v_na.astype(BF16)
    o = 3 * D_NA
    cq = proj[:, o:o + Q_LORA]
    ckv = proj[:, o + Q_LORA:o + Q_LORA + KV_LORA]
    kr = proj[:, o + Q_LORA + KV_LORA:o + Q_LORA + KV_LORA + LANE]
    kr_sw = proj[:, o + Q_LORA + KV_LORA + LANE:]
    cos = cos_ref[...]
    sin = sin_ref[...]

    cqn = _rms(cq, qn_ref[0]).astype(BF16)
    q2 = _dot(cqn, wuq_ref[0])
    qp = q2[:, :D_QP] * jnp.tile(cos, (1, H_MLA)) + q2[:, D_QP:] * jnp.tile(sin, (1, H_MLA))
    qp_ref[...] = qp.astype(BF16)

    ckvn = _rms(ckv, kvn_ref[0])
    ckvb = ckvn.astype(BF16)
    kr_rot = kr * cos + kr_sw * sin
    kp_ref[...] = (_dot(ckvb, wuk_ref[0]) + jnp.tile(kr_rot, (1, H_MLA))).astype(BF16)
    vm_ref[...] = _dot(ckvb, wuv_ref[0]).astype(BF16)

    @pl.when(i < CTX_TILES)
    def _():
        knew_ref[...] = k_na
        vnew_ref[...] = v_na
        ckvnew_ref[...] = ckvn
        krnew_ref[...] = kr[:, NOPE:NOPE + ROPE]


def _pre(l, first, xin, mods, wts, tabs):
    tok = lambda n: pl.BlockSpec((TT, n), lambda i: (i, 0))
    ctx = lambda n: pl.BlockSpec((TT, n), lambda i: (_ctx_block(i), 0))
    lat = lambda n: pl.BlockSpec((TT, n), lambda i: (jnp.maximum(i - CTX_TILES, 0), 0))
    lay = lambda a: pl.BlockSpec((1,) + a.shape[1:], lambda i: (l,) + (0,) * (a.ndim - 1))
    in_specs = []
    args = []
    if first:
        in_specs += [ctx(D), lat(D)]
        args += list(xin)
    else:
        x1, moe = xin
        in_specs += [tok(D), pl.BlockSpec((TT * Y_CHUNKS, LANE), lambda i: (i, 0)),
                     pl.BlockSpec((1, 6, D), lambda i: ((l - 1) * N_MOD_ROWS + _mod_row(i), 0, 0))]
        args += [x1, moe, mods]
    in_specs.append(pl.BlockSpec((1, 6, D), lambda i: (l * N_MOD_ROWS + _mod_row(i), 0, 0)))
    args.append(mods)
    for name in ("n1", "w_in", "qn", "w_uq", "kvn", "w_uk", "w_uv"):
        in_specs.append(lay(wts[name]))
        args.append(wts[name])
    for t in tabs:
        in_specs.append(pl.BlockSpec((TT, LANE), lambda i: (_pos_block(i), 0)))
        args.append(t)
    out_specs = [tok(D), tok(D_NA), tok(D_NA), tok(D_NA), tok(D_QP), tok(D_QP), tok(D_MLA),
                 ctx(D_NA), ctx(D_NA), ctx(KV_LORA), ctx(ROPE)]
    out_shape = [jax.ShapeDtypeStruct((N_TOK, D), F32),
                 jax.ShapeDtypeStruct((N_TOK, D_NA), BF16),
                 jax.ShapeDtypeStruct((N_TOK, D_NA), BF16),
                 jax.ShapeDtypeStruct((N_TOK, D_NA), BF16),
                 jax.ShapeDtypeStruct((N_TOK, D_QP), BF16),
                 jax.ShapeDtypeStruct((N_TOK, D_QP), BF16),
                 jax.ShapeDtypeStruct((N_TOK, D_MLA), BF16),
                 jax.ShapeDtypeStruct((N_CTX, D_NA), F32),
                 jax.ShapeDtypeStruct((N_CTX, D_NA), F32),
                 jax.ShapeDtypeStruct((N_CTX, KV_LORA), F32),
                 jax.ShapeDtypeStruct((N_CTX, ROPE), F32)]
    return pl.pallas_call(
        functools.partial(_pre_kernel, first),
        grid=(N_TILES,),
        in_specs=in_specs,
        out_specs=out_specs,
        out_shape=out_shape,
        compiler_params=_params(("arbitrary",)),
        name="pre",
    )(*args)


def _softmax_parts(parts, scale=None):
    m = parts[0].max(axis=-1, keepdims=True)
    for s in parts[1:]:
        m = jnp.maximum(m, s.max(axis=-1, keepdims=True))
    if scale is None:
        ps = [jnp.exp(s - m) for s in parts]
    else:
        ps = [jnp.exp2((s - m) * (scale * LOG2E)) for s in parts]
    den = ps[0].sum(axis=-1, keepdims=True)
    for p in ps[1:]:
        den = den + p.sum(axis=-1, keepdims=True)
    return ps, den


def _low_lanes():
    return lax.broadcasted_iota(jnp.int32, (TM, LANE), 1) < DH_NA


def _split_pair(qpair, low):
    zero = jnp.zeros_like(qpair)
    return jnp.where(low, qpair, zero), jnp.where(low, zero, qpair)


def _ctx_attn_kernel(qna_ref, kna_ref, vna_ref, qp_ref, kp_ref, vm_ref, o_ref):
    low = _low_lanes()
    outs = []
    for hp in range(H_NA // 2):
        sl = slice(hp * LANE, (hp + 1) * LANE)
        kpair = kna_ref[:, sl]
        vpair = vna_ref[:, sl]
        res = []
        for qh in _split_pair(qna_ref[:, sl], low):
            (p,), den = _softmax_parts([_dot_nt(qh, kpair)])
            res.append(_dot(p.astype(BF16), vpair) / den)
        outs.append(jnp.where(low, res[0], res[1]))
    for hp in range(H_MLA // 2):
        vpair = vm_ref[:, hp * LANE:(hp + 1) * LANE]
        res = []
        for h in (2 * hp, 2 * hp + 1):
            sl = slice(h * HP, (h + 1) * HP)
            (p,), den = _softmax_parts([_dot_nt(qp_ref[:, sl], kp_ref[:, sl])], MLA_SCALE)
            res.append(_dot(p.astype(BF16), vpair) / den)
        outs.append(jnp.where(low, res[0], res[1]))
    o_ref[...] = jnp.concatenate(outs, axis=1).astype(BF16)


def _ctx_attn(qna, kna, vna, qp, kp, vm):
    tok = lambda n: pl.BlockSpec((S_CTX, n), lambda i: (i, 0))
    return pl.pallas_call(
        _ctx_attn_kernel,
        grid=(B_CTX,),
        in_specs=[tok(D_NA), tok(D_NA), tok(D_NA), tok(D_QP), tok(D_QP), tok(D_MLA)],
        out_specs=tok(D_NA + D_MLA),
        out_shape=jax.ShapeDtypeStruct((N_CTX, D_NA + D_MLA), BF16),
        compiler_params=_params(("arbitrary",)),
        name="ctx_attn",
    )(qna, kna, vna, qp, kp, vm)


def _na_kernel(q_ref, k_ref, v_ref, kc_ref, vc_ref, bias_ref, o_ref):
    t = pl.program_id(1)
    row0 = jnp.clip(t * NA_ROWS_PER_TILE - WIN_H // 2, 0, GRID_H - NA_KEY_ROWS)
    ks = pl.multiple_of(row0 * GRID_W, GRID_W)
    low = _low_lanes()
    outs = []
    for hp in range(H_NA // 2):
        sl = slice(hp * LANE, (hp + 1) * LANE)
        kwin = k_ref[pl.ds(ks, NA_KEYS), sl]
        vwin = v_ref[pl.ds(ks, NA_KEYS), sl]
        kctx = kc_ref[0, 0, :, sl].astype(BF16)
        vctx = vc_ref[0, 0, :, sl].astype(BF16)
        res = []
        for half, qh in enumerate(_split_pair(q_ref[:, sl], low)):
            s_win = _dot_nt(qh, kwin) + bias_ref[0, 0, 2 * hp + half]
            s_ctx = _dot_nt(qh, kctx)
            (p_win, p_ctx), den = _softmax_parts([s_win, s_ctx])
            o = _dot(p_win.astype(BF16), vwin) + _dot(p_ctx.astype(BF16), vctx)
            res.append(o / den)
        outs.append(jnp.where(low, res[0], res[1]))
    o_ref[...] = jnp.concatenate(outs, axis=1).astype(BF16)


def _na_attn(l, qna, kna, vna, cache_k, cache_v, bias):
    nt = Q_TILES_PER_SEQ
    seq = lambda n: pl.BlockSpec((S_LAT, n), lambda b, t: (N_CTX // S_LAT + b, 0))
    cache = pl.BlockSpec((1, 1, PAST, D_NA), lambda b, t: (b, l, 0, 0))
    kind = lambda t: jnp.where(t == 0, 0, jnp.where(t == nt - 1, 2, 1))
    return pl.pallas_call(
        _na_kernel,
        grid=(B_LAT, nt),
        in_specs=[pl.BlockSpec((TM, D_NA), lambda b, t: (N_CTX // TM + b * nt + t, 0)),
                  seq(D_NA), seq(D_NA), cache, cache,
                  pl.BlockSpec((1, 1, H_NA, TM, NA_KEYS), lambda b, t: (l, kind(t), 0, 0, 0))],
        out_specs=pl.BlockSpec((TM, D_NA), lambda b, t: (b * nt + t, 0)),
        out_shape=jax.ShapeDtypeStruct((N_LAT, D_NA), BF16),
        compiler_params=_params(("arbitrary", "arbitrary")),
        name="na_attn",
    )(qna, kna, vna, cache_k, cache_v, bias)


def _na_bias_tables(rpb):
    n_dr, n_dc = 2 * WIN_H - 1, 2 * WIN_W - 1
    qc = np.arange(GRID_W)[:, None]
    kc = np.arange(GRID_W)[None, :]
    ws = np.clip(qc - WIN_W // 2, 0, GRID_W - WIN_W)
    col_ok = (kc >= ws) & (kc < ws + WIN_W)
    col_hot = ((kc - qc + WIN_W - 1)[..., None] == np.arange(n_dc)) & col_ok[..., None]
    col_hot = col_hot.reshape(GRID_W * GRID_W, n_dc).astype(np.float32)
    by_col = jnp.einsum("pb,lhab->lhap", jnp.asarray(col_hot), rpb, precision=lax.Precision.HIGHEST)
    by_col = jnp.where(jnp.asarray(col_ok), by_col.reshape(DEPTH, H_NA, n_dr, GRID_W, GRID_W), -jnp.inf)
    pad = jnp.full((DEPTH, H_NA, _BIAS_PAD, GRID_W, GRID_W), -jnp.inf, F32)
    blocks = jnp.concatenate([pad, by_col, pad], axis=2)
    pairs = jnp.concatenate([blocks[:, :, :-1], blocks[:, :, 1:]], axis=-1)
    n_pairs = n_dr + 2 * _BIAS_PAD - 1
    return pl.pallas_call(
        _na_bias_kernel,
        grid=(DEPTH, H_NA),
        in_specs=[pl.BlockSpec((1, 1, n_pairs, GRID_W, LANE), lambda l, h: (l, h, 0, 0, 0))],
        out_specs=pl.BlockSpec((1, len(_NA_KINDS), 1, TM, NA_KEYS), lambda l, h: (l, 0, h, 0, 0)),
        out_shape=jax.ShapeDtypeStruct((DEPTH, len(_NA_KINDS), H_NA, TM, NA_KEYS), F32),
        compiler_params=_params(("arbitrary", "arbitrary")),
        name="na_bias",
    )(pairs)


_NA_KINDS = ((0, 0), (NA_ROWS_PER_TILE, 0), (GRID_H - NA_ROWS_PER_TILE, GRID_H - NA_KEY_ROWS))
_BIAS_PAD = 4


def _na_bias_kernel(pairs_ref, o_ref):
    low = lax.broadcasted_iota(jnp.int32, (GRID_W, LANE), 1) < GRID_W
    ninf = jnp.full((GRID_W, LANE), -jnp.inf, F32)
    for kind, (r0, k0) in enumerate(_NA_KINDS):
        for i in range(NA_ROWS_PER_TILE):
            r = r0 + i
            rs = min(max(r - WIN_H // 2, 0), GRID_H - WIN_H)
            for m in range(NA_KEY_ROWS // 2):
                kr = k0 + 2 * m
                ok0 = rs <= kr < rs + WIN_H
                ok1 = rs <= kr + 1 < rs + WIN_H
                blk = pairs_ref[0, 0, kr - r + WIN_H - 1 + _BIAS_PAD]
                if ok0 and not ok1:
                    blk = jnp.where(low, blk, ninf)
                elif ok1 and not ok0:
                    blk = jnp.where(low, ninf, blk)
                elif not ok0:
                    blk = ninf
                o_ref[0, kind, 0, i * GRID_W:(i + 1) * GRID_W, m * LANE:(m + 1) * LANE] = blk


def _mla_kernel(q_ref, k_ref, v_ref, kc_ref, vc_ref, o_ref):
    low = _low_lanes()
    outs = []
    for hp in range(H_MLA // 2):
        vsl = slice(hp * LANE, (hp + 1) * LANE)
        vlat = v_ref[:, vsl]
        vctx = vc_ref[0, 0, :, vsl]
        res = []
        for h in (2 * hp, 2 * hp + 1):
            sl = slice(h * HP, (h + 1) * HP)
            q = q_ref[:, sl]
            s_ctx = _dot_nt(q, kc_ref[0, 0, :, sl])
            s_lat = _dot_nt(q, k_ref[:, sl])
            (p_ctx, p_lat), den = _softmax_parts([s_ctx, s_lat], MLA_SCALE)
            o = _dot(p_ctx.astype(BF16), vctx) + _dot(p_lat.astype(BF16), vlat)
            res.append(o / den)
        outs.append(jnp.where(low, res[0], res[1]))
    o_ref[...] = jnp.concatenate(outs, axis=1).astype(BF16)


def _mla_attn(l, qp, kp, vm, kc, vc):
    nt = Q_TILES_PER_SEQ
    seq = lambda n: pl.BlockSpec((S_LAT, n), lambda b, t: (N_CTX // S_LAT + b, 0))
    return pl.pallas_call(
        _mla_kernel,
        grid=(B_LAT, nt),
        in_specs=[pl.BlockSpec((TM, D_QP), lambda b, t: (N_CTX // TM + b * nt + t, 0)),
                  seq(D_QP), seq(D_MLA),
                  pl.BlockSpec((1, 1, PAST, D_QP), lambda b, t: (l, b, 0, 0)),
                  pl.BlockSpec((1, 1, PAST, D_MLA), lambda b, t: (l, b, 0, 0))],
        out_specs=pl.BlockSpec((TM, D_MLA), lambda b, t: (b * nt + t, 0)),
        out_shape=jax.ShapeDtypeStruct((N_LAT, D_MLA), BF16),
        compiler_params=_params(("arbitrary", "arbitrary")),
        name="mla_attn",
    )(qp, kp, vm, kc, vc)


def _route(sel, aff):
    rows = lambda a, g: [a[EPG * g + e:EPG * g + e + 1, :] for e in range(EPG)]
    best = None
    for g in range(N_GRP):
        v = rows(sel, g)
        score = v[0] + v[1]
        for a, b in ((0, 2), (0, 3), (1, 2), (1, 3), (2, 3)):
            score = jnp.maximum(score, v[a] + v[b])
        if best is None:
            best, grp = score, jnp.zeros_like(score)
        else:
            better = score > best
            best = jnp.where(better, score, best)
            grp = jnp.where(better, float(g), grp)
    zero = jnp.zeros_like(best)
    sv = [zero] * EPG
    av = [zero] * EPG
    for g in range(N_GRP):
        hit = grp == float(g)
        sg, ag = rows(sel, g), rows(aff, g)
        sv = [jnp.where(hit, sg[e], sv[e]) for e in range(EPG)]
        av = [jnp.where(hit, ag[e], av[e]) for e in range(EPG)]

    def argmax_first(vals, skip=None):
        bv, bi = None, None
        for e in range(EPG):
            v = vals[e] if skip is None else jnp.where(skip == float(e), -jnp.inf, vals[e])
            if bv is None:
                bv, bi = v, jnp.zeros_like(v)
            else:
                better = v > bv
                bv = jnp.where(better, v, bv)
                bi = jnp.where(better, float(e), bi)
        return bi

    i1 = argmax_first(sv)
    i2 = argmax_first(sv, skip=i1)
    lo = jnp.minimum(i1, i2)
    hi = jnp.maximum(i1, i2)
    pick = lambda idx: sum(jnp.where(idx == float(e), av[e], 0.0) for e in range(EPG))
    a_lo, a_hi = pick(lo), pick(hi)
    den = a_lo + a_hi
    pair = jnp.where(lo == 0.0, 0.0, jnp.where(lo == 1.0, 3.0, 5.0)) + hi - lo - 1.0
    return grp * float(N_PAIR) + pair, a_lo / den, a_hi / den


def _post_kernel(x_ref, octx_ref, ona_ref, omla_ref, mod_ref, n2_ref, wout_ref, wr_ref, br_ref,
                 x1_ref, h2_ref, route_ref, counts_ref, carry):
    i = pl.program_id(0)

    @pl.when(i == 0)
    def _():
        carry[...] = jnp.zeros_like(carry)

    is_ctx = i < CTX_TILES
    o_lat = jnp.concatenate([ona_ref[...], omla_ref[...]], axis=1)
    o = jnp.where(is_ctx, octx_ref[...], o_lat)
    m = mod_ref[0]
    x1 = x_ref[...] + m[2:3, :] * _dot(o, wout_ref[0])
    x1_ref[...] = x1
    h2 = _rms(x1, n2_ref[0]) * (1.0 + m[4:5, :]) + m[3:4, :]
    logits = lax.dot_general(wr_ref[...], h2, (((1,), (1,)), ((), ())),
                             preferred_element_type=F32, precision=lax.Precision.HIGHEST)
    aff = jax.nn.sigmoid(logits)
    bucket, g_lo, g_hi = _route(aff + br_ref[...], aff)

    hit = lax.broadcasted_iota(jnp.int32, (BUCKET_ROWS, TT), 0).astype(F32) == bucket
    earlier = lax.broadcasted_iota(jnp.int32, (TT, TT), 0) <= lax.broadcasted_iota(jnp.int32, (TT, TT), 1)
    incl = _dot(jnp.where(hit, 1.0, 0.0).astype(BF16), jnp.where(earlier, 1.0, 0.0).astype(BF16))
    base = carry[...]
    rank = jnp.sum(jnp.where(hit, incl + base[:, 0:1], 0.0), axis=0, keepdims=True) - 1.0
    carry[...] = base + incl[:, TT - 1:TT]
    counts_ref[...] = carry[...]
    route_ref[0] = jnp.concatenate([bucket, rank, jnp.zeros((6, TT), F32)], axis=0)

    gate_cols = jnp.transpose(jnp.concatenate([g_lo, g_hi, jnp.zeros((6, TT), F32)], axis=0))
    tail = jnp.zeros((TT, (X_CHUNKS - GATE_CHUNK) * LANE - 8), F32)
    gate_words = lax.bitcast_convert_type(jnp.concatenate([gate_cols, tail], axis=1), U32)
    _to_rows(h2_ref, jnp.concatenate([_pack_bf16_pairs(h2), gate_words], axis=1), X_CHUNKS)


def _post(l, x, o_ctx, o_na, o_mla, mods, wts):
    tok = lambda n: pl.BlockSpec((TT, n), lambda i: (i, 0))
    lat = lambda n: pl.BlockSpec((TT, n), lambda i: (jnp.maximum(i - CTX_TILES, 0), 0))
    lay = lambda a: pl.BlockSpec((1,) + a.shape[1:], lambda i: (l,) + (0,) * (a.ndim - 1))
    full = lambda a: pl.BlockSpec(a.shape, lambda i: (0,) * a.ndim)
    return pl.pallas_call(
        _post_kernel,
        grid=(N_TILES,),
        in_specs=[tok(D), pl.BlockSpec((TT, D), lambda i: (_ctx_block(i), 0)), lat(D_NA), lat(D_MLA),
                  pl.BlockSpec((1, 6, D), lambda i: (l * N_MOD_ROWS + _mod_row(i), 0, 0)),
                  lay(wts["n2"]), lay(wts["w_out"]), full(wts["w_rt"]), full(wts["b_r"])],
        out_specs=[tok(D), pl.BlockSpec((TT * X_CHUNKS, LANE), lambda i: (i, 0)),
                   pl.BlockSpec((1, 8, TT), lambda i: (i, 0, 0)),
                   pl.BlockSpec((BUCKET_ROWS, LANE), lambda i: (0, 0))],
        out_shape=[jax.ShapeDtypeStruct((N_TOK, D), F32),
                   jax.ShapeDtypeStruct((N_TOK * X_CHUNKS, LANE), U32),
                   jax.ShapeDtypeStruct((N_TILES, 8, TT), F32),
                   jax.ShapeDtypeStruct((BUCKET_ROWS, LANE), F32)],
        scratch_shapes=[pltpu.VMEM((BUCKET_ROWS, LANE), F32)],
        compiler_params=_params(("arbitrary",)),
        name="post",
    )(x, o_ctx, o_na, o_mla, mods, wts["n2"], wts["w_out"], wts["w_rt"], wts["b_r"])


def _moe_kernel(elo_ref, ehi_ref, nv_ref, total_ref, slot_ref,
                h2_hbm, wg1, wu1, wd1, wg2, wu2, wd2, out_hbm,
                xbuf, ybuf, src, dst, gsem, ssem):
    del elo_ref, ehi_ref
    t = pl.program_id(0)
    total = total_ref[0]
    par = t % 2

    def start_gather(tile, p):
        def body(j, c):
            row = pl.multiple_of(src[tile * TM + j] * X_CHUNKS, X_CHUNKS)
            pltpu.make_async_copy(h2_hbm.at[pl.ds(row, X_CHUNKS), :],
                                  xbuf.at[p, pl.ds(j * X_CHUNKS, X_CHUNKS), :], gsem.at[p]).start()
            return c
        lax.fori_loop(0, TM, body, 0, unroll=DMA_UNROLL)

    def start_scatter(tile, p):
        def body(j, c):
            row = pl.multiple_of(dst[tile * TM + j] * Y_CHUNKS, Y_CHUNKS)
            pltpu.make_async_copy(ybuf.at[p, pl.ds(j * Y_CHUNKS, Y_CHUNKS), :],
                                  out_hbm.at[pl.ds(row, Y_CHUNKS), :], ssem.at[p]).start()
            return c
        lax.fori_loop(0, TM, body, 0, unroll=DMA_UNROLL)

    def wait_gather(p):
        pltpu.make_async_copy(h2_hbm.at[pl.ds(0, TM * X_CHUNKS), :], xbuf.at[p], gsem.at[p]).wait()

    def wait_scatter(p):
        pltpu.make_async_copy(ybuf.at[p], out_hbm.at[pl.ds(0, TM * Y_CHUNKS), :], ssem.at[p]).wait()

    @pl.when(t == 0)
    def _():
        def invert(i, c):
            s = slot_ref[i]
            src[s] = i
            dst[s] = i
            return c
        lax.fori_loop(0, N_TOK, invert, 0, unroll=DMA_UNROLL)

        def pad_tile(tile, c):
            def pad_slot(j, c2):
                src[tile * TM + j] = src[tile * TM]
                dst[tile * TM + j] = N_TOK + j
                return c2
            return lax.fori_loop(nv_ref[tile], TM, pad_slot, c)
        lax.fori_loop(0, total, pad_tile, 0)
        ybuf[0] = jnp.zeros((TM * Y_CHUNKS, LANE), F32)
        spare = pltpu.make_async_copy(ybuf.at[0], out_hbm.at[pl.ds(N_TOK * Y_CHUNKS, TM * Y_CHUNKS), :], ssem.at[0])
        spare.start()
        spare.wait()
        start_gather(0, 0)

    @pl.when(t + 1 < total)
    def _():
        start_gather(t + 1, 1 - par)

    @pl.when(t < total)
    def _():
        wait_gather(par)

        @pl.when(t >= 2)
        def _():
            wait_scatter(par)

        x = _unpack_bf16_pairs(_from_rows(xbuf, TM, X_CHUNKS, 0, GATE_CHUNK, lead=(par,)))
        gates = lax.bitcast_convert_type(_from_rows(xbuf, TM, X_CHUNKS, GATE_CHUNK, 1, lead=(par,)), F32)
        y = None
        for k, (wg, wu, wd) in enumerate(((wg1, wu1, wd1), (wg2, wu2, wd2))):
            a = _dot(x, wg[0, 0])
            hid = a * jax.nn.sigmoid(a) * _dot(x, wu[0, 0]) * gates[:, k:k + 1]
            part = _dot(hid.astype(BF16), wd[0, 0])
            y = part if y is None else y + part
        _to_rows(ybuf, y, Y_CHUNKS, lead=(par,))
        start_scatter(t, par)

        @pl.when(t == total - 1)
        def _():
            @pl.when(t >= 1)
            def _():
                wait_scatter(1 - par)
            wait_scatter(par)


def _moe(l, h2, e_lo, e_hi, n_valid, total, slot, wts):
    up = lambda which: pl.BlockSpec((1, 1, D, D_EXP), lambda t, *pf: (l, pf[which][t], 0, 0))
    down = lambda which: pl.BlockSpec((1, 1, D_EXP, D), lambda t, *pf: (l, pf[which][t], 0, 0))
    first, second = 0, 1
    return pl.pallas_call(
        _moe_kernel,
        grid_spec=pltpu.PrefetchScalarGridSpec(
            num_scalar_prefetch=5,
            grid=(MOE_TILES,),
            in_specs=[pl.BlockSpec(memory_space=pl.ANY),
                      up(first), up(first), down(first), up(second), up(second), down(second)],
            out_specs=pl.BlockSpec(memory_space=pl.ANY),
            scratch_shapes=[pltpu.VMEM((2, TM * X_CHUNKS, LANE), U32), pltpu.VMEM((2, TM * Y_CHUNKS, LANE), F32),
                            pltpu.SMEM((MOE_TILES * TM,), jnp.int32), pltpu.SMEM((MOE_TILES * TM,), jnp.int32),
                            pltpu.SemaphoreType.DMA((2,)), pltpu.SemaphoreType.DMA((2,))],
        ),
        out_shape=jax.ShapeDtypeStruct(((N_TOK + TM) * Y_CHUNKS, LANE), F32),
        compiler_params=_params(("arbitrary",)),
        name="moe",
    )(e_lo, e_hi, n_valid, total, slot, h2,
      wts["w_gate"], wts["w_up"], wts["w_down"], wts["w_gate"], wts["w_up"], wts["w_down"])


def _moe_schedule(route, counts):
    bucket = route[:, 0, :].reshape(N_TOK).astype(jnp.int32)
    rank = route[:, 1, :].reshape(N_TOK).astype(jnp.int32)
    counts = counts[:N_BUCKET, 0].astype(jnp.int32)
    tiles = (counts + TM - 1) // TM
    tile_end = jnp.cumsum(tiles)
    tile_start = tile_end - tiles
    buckets = jnp.arange(N_BUCKET, dtype=jnp.int32)
    first_slot = jnp.where(bucket[:, None] == buckets[None, :], (tile_start * TM)[None, :], 0).sum(axis=1)
    slot = first_slot + rank
    t = jnp.arange(MOE_TILES, dtype=jnp.int32)
    total = tile_end[-1]
    bucket_of = lambda tile: (tile_end[None, :] <= tile[:, None]).astype(jnp.int32).sum(axis=1)
    tb = bucket_of(jnp.minimum(t, total - 1))
    n_valid = jnp.where(t < total, jnp.clip(counts[tb] - (t - tile_start[tb]) * TM, 0, TM), 0).astype(jnp.int32)
    grp = tb // N_PAIR
    e_lo = grp * EPG + jnp.asarray(_PAIR_LO)[tb % N_PAIR]
    e_hi = grp * EPG + jnp.asarray(_PAIR_HI)[tb % N_PAIR]
    return e_lo, e_hi, n_valid, total.reshape(1), slot


def _final_kernel(x1_ref, moe_ref, mod_ref, g_ref, yc_ref, yl_ref):
    i = pl.program_id(0)
    moe = _from_rows(moe_ref, TT, Y_CHUNKS, 0, Y_CHUNKS)
    y = _rms(x1_ref[...] + mod_ref[0, 5:6, :] * moe, g_ref[...])

    @pl.when(i < CTX_TILES)
    def _():
        yc_ref[...] = y

    @pl.when(i >= CTX_TILES)
    def _():
        yl_ref[...] = y


def _final(x1, moe, mods, g):
    tok = pl.BlockSpec((TT, D), lambda i: (i, 0))
    l = DEPTH - 1
    return pl.pallas_call(
        _final_kernel,
        grid=(N_TILES,),
        in_specs=[tok, pl.BlockSpec((TT * Y_CHUNKS, LANE), lambda i: (i, 0)),
                  pl.BlockSpec((1, 6, D), lambda i: (l * N_MOD_ROWS + _mod_row(i), 0, 0)),
                  pl.BlockSpec((1, D), lambda i: (0, 0))],
        out_specs=[pl.BlockSpec((TT, D), lambda i: (_ctx_block(i), 0)),
                   pl.BlockSpec((TT, D), lambda i: (jnp.maximum(i - CTX_TILES, 0), 0))],
        out_shape=[jax.ShapeDtypeStruct((N_CTX, D), F32), jax.ShapeDtypeStruct((N_LAT, D), F32)],
        compiler_params=_params(("arbitrary",)),
        name="final",
    )(x1, moe, mods, g)


def _swap_halves(w):
    q = ROPE // 4
    return jnp.concatenate([-w[..., q:2 * q], w[..., :q], -w[..., 3 * q:], w[..., 2 * q:3 * q]], axis=-1)


def _pad_head(nope, rope):
    z = jnp.zeros(rope.shape[:-1] + (HP - NOPE - ROPE,), rope.dtype)
    return jnp.concatenate([nope, rope, z], axis=-1)


def _prepare_weights(norm1_g, norm2_g, w_in, q_norm_g, w_uq, kv_norm_g, w_ukv, w_out, w_router, b_router,
                     w_gate, w_up, w_down):
    o = 3 * D_NA + Q_LORA + KV_LORA
    w_kr = w_in[:, :, o:]
    zeros_nope = jnp.zeros((DEPTH, D, NOPE), F32)
    w_in_p = jnp.concatenate([w_in[:, :, :o], _pad_head(zeros_nope, w_kr),
                              _pad_head(zeros_nope, _swap_halves(w_kr))], axis=-1)
    wq = w_uq.reshape(DEPTH, Q_LORA, H_MLA, NOPE + ROPE)
    plain = _pad_head(wq[..., :NOPE], wq[..., NOPE:]).reshape(DEPTH, Q_LORA, D_QP)
    swapped = _pad_head(jnp.zeros_like(wq[..., :NOPE]), _swap_halves(wq[..., NOPE:])).reshape(DEPTH, Q_LORA, D_QP)
    wkv = w_ukv.reshape(DEPTH, KV_LORA, H_MLA, NOPE + VH)
    w_uk_p = jnp.concatenate([wkv[..., :NOPE], jnp.zeros((DEPTH, KV_LORA, H_MLA, HP - NOPE), F32)],
                             axis=-1).reshape(DEPTH, KV_LORA, D_QP)
    w_uv = wkv[..., NOPE:].reshape(DEPTH, KV_LORA, D_MLA)
    return {
        "n1": norm1_g.reshape(DEPTH, 1, D),
        "n2": norm2_g.reshape(DEPTH, 1, D),
        "w_in": w_in_p.astype(BF16),
        "qn": q_norm_g.reshape(DEPTH, 1, Q_LORA),
        "w_uq": jnp.concatenate([plain, swapped], axis=-1).astype(BF16),
        "kvn": kv_norm_g.reshape(DEPTH, 1, KV_LORA),
        "w_uk": w_uk_p.astype(BF16),
        "w_uv": w_uv.astype(BF16),
        "w_out": w_out.astype(BF16),
        "w_rt": w_router.T,
        "b_r": b_router.reshape(N_EXP, 1),
        "w_gate": w_gate.astype(BF16),
        "w_up": w_up.astype(BF16),
        "w_down": w_down.astype(BF16),
    }


def _rope_tables():
    half = ROPE // 2
    freqs = 1.0 / (ROPE_THETA ** (np.arange(0, half, 2, dtype=np.float32) / half))
    pos = np.arange(S_LAT)
    ang_r = (pos // GRID_W).astype(np.float32)[:, None] * freqs
    ang_c = (pos % GRID_W).astype(np.float32)[:, None] * freqs
    ang = np.concatenate([ang_r, ang_r, ang_c, ang_c], axis=1)
    cos = np.ones((TT + S_LAT, LANE), np.float32)
    sin = np.zeros((TT + S_LAT, LANE), np.float32)
    cos[TT:, NOPE:NOPE + ROPE] = np.cos(ang)
    sin[TT:, NOPE:NOPE + ROPE] = np.sin(ang)
    return jnp.asarray(cos), jnp.asarray(sin)


def kernel(x_prompt, x_sample, cache_nat_k, cache_nat_v, cache_mla_ckv, cache_mla_krope, c, c_ctx, norm1_g, norm2_g, w_ada, b_ada, w_in, q_norm_g, w_uq, kv_norm_g, w_ukv, rpb, w_out, w_router, b_router, w_gate, w_up, w_down, final_norm_g):
    wts = _prepare_weights(norm1_g, norm2_g, w_in, q_norm_g, w_uq, kv_norm_g, w_ukv, w_out, w_router, b_router,
                           w_gate, w_up, w_down)
    tabs = _rope_tables()
    cond = jnp.concatenate([c_ctx[None, :], c, jnp.zeros((N_MOD_ROWS - 1 - B_LAT, D), F32)], axis=0)
    mods = _ada(cond, w_ada, b_ada).reshape(DEPTH * N_MOD_ROWS, 6, D)
    place = jnp.asarray(np.eye(ROPE, HP, k=NOPE, dtype=np.float32)).astype(BF16)
    kc_all, vc_all = _ctxkv(cache_mla_ckv, cache_mla_krope, wts["w_uk"], wts["w_uv"], place)
    cache_k = cache_nat_k.reshape(B_LAT, DEPTH, PAST, D_NA)
    cache_v = cache_nat_v.reshape(B_LAT, DEPTH, PAST, D_NA)
    bias = _na_bias_tables(rpb)

    x1 = moe = None
    new_k, new_v, new_ckv, new_kr = [], [], [], []
    for l in range(DEPTH):
        xin = (x_prompt.reshape(N_CTX, D), x_sample.reshape(N_LAT, D)) if l == 0 else (x1, moe)
        x, qna, kna, vna, qp, kp, vm, k_new, v_new, ckv_new, kr_new = _pre(l, l == 0, xin, mods, wts, tabs)
        new_k.append(k_new)
        new_v.append(v_new)
        new_ckv.append(ckv_new)
        new_kr.append(kr_new)
        o_ctx = _ctx_attn(qna, kna, vna, qp, kp, vm)
        o_na = _na_attn(l, qna, kna, vna, cache_k, cache_v, bias)
        o_mla = _mla_attn(l, qp, kp, vm, kc_all, vc_all)
        x1, h2, route, counts = _post(l, x, o_ctx, o_na, o_mla, mods, wts)
        moe = _moe(l, h2, *_moe_schedule(route, counts), wts)
    y_ctx, y_lat = _final(x1, moe, mods, final_norm_g.reshape(1, D))

    stack = lambda parts, tail: jnp.stack([p.reshape((B_CTX, S_CTX) + tail) for p in parts], axis=1)
    return (y_ctx.reshape(B_CTX, S_CTX, D), y_lat.reshape(B_LAT, S_LAT, D),
            stack(new_k, (H_NA, DH_NA)), stack(new_v, (H_NA, DH_NA)),
            stack(new_ckv, (KV_LORA,)), stack(new_kr, (ROPE,)))
```

```python
import functools

import numpy as np
import jax
import jax.numpy as jnp
from jax import lax
from jax.experimental import pallas as pl
from jax.experimental.pallas import tpu as pltpu

F32 = jnp.float32
BF16 = jnp.bfloat16

D = 1024
B_CTX, S_CTX = 32, 256
B_LAT, S_LAT = 8, 2048
DEPTH = 4
PAST = 256
GRID_W, GRID_H = 64, 32
H_NA, DH_NA = 8, 64
WIN_H, WIN_W = 8, 16
H_MLA, NOPE, ROPE, VH = 8, 64, 32, 64
Q_LORA, KV_LORA = 256, 128
D_NA = H_NA * DH_NA
D_MLA = H_MLA * VH
N_EXP, N_GRP, EPG = 16, 4, 4
D_EXP = 512
ROPE_THETA = 10000.0
EPS = 1e-6
NA_SCALE = DH_NA ** -0.5
MLA_SCALE = (NOPE + ROPE) ** -0.5
LOG2E = 1.4426950408889634

N_CTX = B_CTX * S_CTX
N_LAT = B_LAT * S_LAT
N_TOK = N_CTX + N_LAT
TM = 256
TT = 512
CTX_TILES = N_CTX // TT
LAT_TILES = N_LAT // TT
N_TILES = CTX_TILES + LAT_TILES
LAT_TILES_PER_SEQ = S_LAT // TT
Q_TILES_PER_SEQ = S_LAT // TM
LANE = 128
HP = LANE
D_QP = H_MLA * HP
W_IN_COLS = 3 * D_NA + Q_LORA + KV_LORA + 2 * LANE
N_MOD_ROWS = 16
N_PAIR = 6
N_BUCKET = N_GRP * N_PAIR
MOE_TILES = N_TOK // TM + N_BUCKET
BUCKET_ROWS = 32
Y_CHUNKS = D // LANE
X_CHUNKS = D // LANE


def _to_rows(ref, value, chunks, lead=()):
    t = value.shape[0]
    for c in range(value.shape[1] // LANE):
        ref[lead + (pl.ds(c, t, stride=chunks), slice(None))] = value[:, c * LANE:(c + 1) * LANE]


def _from_rows(ref, t, chunks, first, n, lead=()):
    parts = [ref[lead + (pl.ds(first + c, t, stride=chunks), slice(None))] for c in range(n)]
    return parts[0] if n == 1 else jnp.concatenate(parts, axis=1)
NA_ROWS_PER_TILE = TM // GRID_W
NA_KEY_ROWS = 12
NA_KEYS = NA_KEY_ROWS * GRID_W
VMEM_LIMIT = 48 * 1024 * 1024
DMA_UNROLL = 8


def _dot(a, b):
    return jnp.dot(a, b, preferred_element_type=F32)


def _dot_nt(a, b):
    return lax.dot_general(a, b, (((1,), (1,)), ((), ())), preferred_element_type=F32)


def _rms(x, g):
    return x * lax.rsqrt(jnp.mean(x * x, axis=-1, keepdims=True) + EPS) * g


def _mod_row(i):
    return jnp.where(i < CTX_TILES, 0, 1 + (i - CTX_TILES) // LAT_TILES_PER_SEQ)


def _pos_block(i):
    return jnp.where(i < CTX_TILES, 0, 1 + (i - CTX_TILES) % LAT_TILES_PER_SEQ)


def _ctx_block(i):
    return jnp.minimum(i, CTX_TILES - 1)


def _params(sem):
    return pltpu.CompilerParams(dimension_semantics=sem, vmem_limit_bytes=VMEM_LIMIT)


def _ada_kernel(cond_ref, w_ref, b_ref, o_ref):
    cond = cond_ref[...]
    act = cond * jax.nn.sigmoid(cond)
    o_ref[0] = jnp.dot(act, w_ref[0], preferred_element_type=F32,
                       precision=lax.Precision.HIGHEST) + b_ref[0]


def _ada(cond, w_ada, b_ada):
    nt = 1536
    return pl.pallas_call(
        _ada_kernel,
        grid=(DEPTH, 6 * D // nt),
        in_specs=[pl.BlockSpec((N_MOD_ROWS, D), lambda l, j: (0, 0)),
                  pl.BlockSpec((1, D, nt), lambda l, j: (l, 0, j)),
                  pl.BlockSpec((1, 1, nt), lambda l, j: (l, 0, j))],
        out_specs=pl.BlockSpec((1, N_MOD_ROWS, nt), lambda l, j: (l, 0, j)),
        out_shape=jax.ShapeDtypeStruct((DEPTH, N_MOD_ROWS, 6 * D), F32),
        compiler_params=_params(("arbitrary", "arbitrary")),
        name="ada",
    )(cond, w_ada, b_ada.reshape(DEPTH, 1, 6 * D))


def _ctxkv_kernel(ckv_ref, kr_ref, wuk_ref, wuv_ref, place_ref, kc_ref, vc_ref):
    ckv = ckv_ref[0, 0].astype(BF16)
    kr = _dot(kr_ref[0, 0].astype(BF16), place_ref[...])
    kc_ref[0, 0] = (_dot(ckv, wuk_ref[0]) + jnp.tile(kr, (1, H_MLA))).astype(BF16)
    vc_ref[0, 0] = _dot(ckv, wuv_ref[0]).astype(BF16)


def _ctxkv(cache_ckv, cache_krope, w_uk_p, w_uv, place):
    return pl.pallas_call(
        _ctxkv_kernel,
        grid=(DEPTH, B_LAT),
        in_specs=[pl.BlockSpec((1, 1, PAST, KV_LORA), lambda l, b: (b, l, 0, 0)),
                  pl.BlockSpec((1, 1, PAST, ROPE), lambda l, b: (b, l, 0, 0)),
                  pl.BlockSpec((1, KV_LORA, D_QP), lambda l, b: (l, 0, 0)),
                  pl.BlockSpec((1, KV_LORA, D_MLA), lambda l, b: (l, 0, 0)),
                  pl.BlockSpec((ROPE, HP), lambda l, b: (0, 0))],
        out_specs=[pl.BlockSpec((1, 1, PAST, D_QP), lambda l, b: (l, b, 0, 0)),
                   pl.BlockSpec((1, 1, PAST, D_MLA), lambda l, b: (l, b, 0, 0))],
        out_shape=[jax.ShapeDtypeStruct((DEPTH, B_LAT, PAST, D_QP), BF16),
                   jax.ShapeDtypeStruct((DEPTH, B_LAT, PAST, D_MLA), BF16)],
        compiler_params=_params(("arbitrary", "arbitrary")),
        name="ctxkv",
    )(cache_ckv, cache_krope, w_uk_p, w_uv, place)


def _pre_kernel(first, *refs):
    if first:
        xc_ref, xl_ref, *rest = refs
    else:
        x1_ref, moe_ref, modp_ref, *rest = refs
    (mod_ref, n1_ref, win_ref, qn_ref, wuq_ref, kvn_ref, wuk_ref, wuv_ref, cos_ref, sin_ref,
     xo_ref, qna_ref, kna_ref, vna_ref, qp_ref, kp_ref, vm_ref, knew_ref, vnew_ref, ckvnew_ref, krnew_ref) = rest
    i = pl.program_id(0)

    if first:
        x = jnp.where(i < CTX_TILES, xc_ref[...], xl_ref[...])
    else:
        x = x1_ref[...] + modp_ref[0, 5:6, :] * _from_rows(moe_ref, TT, Y_CHUNKS, 0, Y_CHUNKS)
    xo_ref[...] = x
    m = mod_ref[0]
    h = _rms(x, n1_ref[0]) * (1.0 + m[1:2, :]) + m[0:1, :]
    proj = _dot(h.astype(BF16), win_ref[0])
    k_na = proj[:, D_NA:2 * D_NA]
    v_na = proj[:, 2 * D_NA:3 * D_NA]
    qna_ref[...] = (proj[:, :D_NA] * NA_SCALE).astype(BF16)
    kna_ref[...] = k_na.astype(BF16)
    vna_ref[...] = v_na.astype(BF16)
    o = 3 * D_NA
    cq = proj[:, o:o + Q_LORA]
    ckv = proj[:, o + Q_LORA:o + Q_LORA + KV_LORA]
    kr = proj[:, o + Q_LORA + KV_LORA:o + Q_LORA + KV_LORA + LANE]
    kr_sw = proj[:, o + Q_LORA + KV_LORA + LANE:]
    cos = cos_ref[...]
    sin = sin_ref[...]

    cqn = _rms(cq, qn_ref[0]).astype(BF16)
    q2 = _dot(cqn, wuq_ref[0])
    qp = q2[:, :D_QP] * jnp.tile(cos, (1, H_MLA)) + q2[:, D_QP:] * jnp.tile(sin, (1, H_MLA))
    qp_ref[...] = qp.astype(BF16)

    ckvn = _rms(ckv, kvn_ref[0])
    ckvb = ckvn.astype(BF16)
    kr_rot = kr * cos + kr_sw * sin
    kp_ref[...] = (_dot(ckvb, wuk_ref[0]) + jnp.tile(kr_rot, (1, H_MLA))).astype(BF16)
    vm_ref[...] = _dot(ckvb, wuv_ref[0]).astype(BF16)

    @pl.when(i < CTX_TILES)
    def _():
        knew_ref[...] = k_na
        vnew_ref[...] = v_na
        ckvnew_ref[...] = ckvn
        krnew_ref[...] = kr[:, NOPE:NOPE + ROPE]


def _pre(l, first, xin, mods, wts, tabs):
    tok = lambda n: pl.BlockSpec((TT, n), lambda i: (i, 0))
    ctx = lambda n: pl.BlockSpec((TT, n), lambda i: (_ctx_block(i), 0))
    lat = lambda n: pl.BlockSpec((TT, n), lambda i: (jnp.maximum(i - CTX_TILES, 0), 0))
    lay = lambda a: pl.BlockSpec((1,) + a.shape[1:], lambda i: (l,) + (0,) * (a.ndim - 1))
    in_specs = []
    args = []
    if first:
        in_specs += [ctx(D), lat(D)]
        args += list(xin)
    else:
        x1, moe = xin
        in_specs += [tok(D), pl.BlockSpec((TT * Y_CHUNKS, LANE), lambda i: (i, 0)),
                     pl.BlockSpec((1, 6, D), lambda i: ((l - 1) * N_MOD_ROWS + _mod_row(i), 0, 0))]
        args += [x1, moe, mods]
    in_specs.append(pl.BlockSpec((1, 6, D), lambda i: (l * N_MOD_ROWS + _mod_row(i), 0, 0)))
    args.append(mods)
    for name in ("n1", "w_in", "qn", "w_uq", "kvn", "w_uk", "w_uv"):
        in_specs.append(lay(wts[name]))
        args.append(wts[name])
    for t in tabs:
        in_specs.append(pl.BlockSpec((TT, LANE), lambda i: (_pos_block(i), 0)))
        args.append(t)
    out_specs = [tok(D), tok(D_NA), tok(D_NA), tok(D_NA), tok(D_QP), tok(D_QP), tok(D_MLA),
                 ctx(D_NA), ctx(D_NA), ctx(KV_LORA), ctx(ROPE)]
    out_shape = [jax.ShapeDtypeStruct((N_TOK, D), F32),
                 jax.ShapeDtypeStruct((N_TOK, D_NA), BF16),
                 jax.ShapeDtypeStruct((N_TOK, D_NA), BF16),
                 jax.ShapeDtypeStruct((N_TOK, D_NA), BF16),
                 jax.ShapeDtypeStruct((N_TOK, D_QP), BF16),
                 jax.ShapeDtypeStruct((N_TOK, D_QP), BF16),
                 jax.ShapeDtypeStruct((N_TOK, D_MLA), BF16),
                 jax.ShapeDtypeStruct((N_CTX, D_NA), F32),
                 jax.ShapeDtypeStruct((N_CTX, D_NA), F32),
                 jax.ShapeDtypeStruct((N_CTX, KV_LORA), F32),
                 jax.ShapeDtypeStruct((N_CTX, ROPE), F32)]
    return pl.pallas_call(
        functools.partial(_pre_kernel, first),
        grid=(N_TILES,),
        in_specs=in_specs,
        out_specs=out_specs,
        out_shape=out_shape,
        compiler_params=_params(("arbitrary",)),
        name="pre",
    )(*args)


def _softmax_parts(parts, scale=None):
    m = parts[0].max(axis=-1, keepdims=True)
    for s in parts[1:]:
        m = jnp.maximum(m, s.max(axis=-1, keepdims=True))
    if scale is None:
        ps = [jnp.exp(s - m) for s in parts]
    else:
        ps = [jnp.exp2((s - m) * (scale * LOG2E)) for s in parts]
    den = ps[0].sum(axis=-1, keepdims=True)
    for p in ps[1:]:
        den = den + p.sum(axis=-1, keepdims=True)
    return ps, den


def _low_lanes():
    return lax.broadcasted_iota(jnp.int32, (TM, LANE), 1) < DH_NA


def _split_pair(qpair, low):
    zero = jnp.zeros_like(qpair)
    return jnp.where(low, qpair, zero), jnp.where(low, zero, qpair)


def _ctx_attn_kernel(qna_ref, kna_ref, vna_ref, qp_ref, kp_ref, vm_ref, o_ref):
    low = _low_lanes()
    outs = []
    for hp in range(H_NA // 2):
        sl = slice(hp * LANE, (hp + 1) * LANE)
        kpair = kna_ref[:, sl]
        vpair = vna_ref[:, sl]
        res = []
        for qh in _split_pair(qna_ref[:, sl], low):
            (p,), den = _softmax_parts([_dot_nt(qh, kpair)])
            res.append(_dot(p.astype(BF16), vpair) / den)
        outs.append(jnp.where(low, res[0], res[1]))
    for hp in range(H_MLA // 2):
        vpair = vm_ref[:, hp * LANE:(hp + 1) * LANE]
        res = []
        for h in (2 * hp, 2 * hp + 1):
            sl = slice(h * HP, (h + 1) * HP)
            (p,), den = _softmax_parts([_dot_nt(qp_ref[:, sl], kp_ref[:, sl])], MLA_SCALE)
            res.append(_dot(p.astype(BF16), vpair) / den)
        outs.append(jnp.where(low, res[0], res[1]))
    o_ref[...] = jnp.concatenate(outs, axis=1).astype(BF16)


def _ctx_attn(qna, kna, vna, qp, kp, vm):
    tok = lambda n: pl.BlockSpec((S_CTX, n), lambda i: (i, 0))
    return pl.pallas_call(
        _ctx_attn_kernel,
        grid=(B_CTX,),
        in_specs=[tok(D_NA), tok(D_NA), tok(D_NA), tok(D_QP), tok(D_QP), tok(D_MLA)],
        out_specs=tok(D_NA + D_MLA),
        out_shape=jax.ShapeDtypeStruct((N_CTX, D_NA + D_MLA), BF16),
        compiler_params=_params(("arbitrary",)),
        name="ctx_attn",
    )(qna, kna, vna, qp, kp, vm)


def _na_kernel(q_ref, k_ref, v_ref, kc_ref, vc_ref, bias_ref, o_ref):
    t = pl.program_id(1)
    row0 = jnp.clip(t * NA_ROWS_PER_TILE - WIN_H // 2, 0, GRID_H - NA_KEY_ROWS)
    ks = pl.multiple_of(row0 * GRID_W, GRID_W)
    low = _low_lanes()
    outs = []
    for hp in range(H_NA // 2):
        sl = slice(hp * LANE, (hp + 1) * LANE)
        kwin = k_ref[pl.ds(ks, NA_KEYS), sl]
        vwin = v_ref[pl.ds(ks, NA_KEYS), sl]
        kctx = kc_ref[0, 0, :, sl].astype(BF16)
        vctx = vc_ref[0, 0, :, sl].astype(BF16)
        res = []
        for half, qh in enumerate(_split_pair(q_ref[:, sl], low)):
            s_win = _dot_nt(qh, kwin) + bias_ref[0, 0, 2 * hp + half]
            s_ctx = _dot_nt(qh, kctx)
            (p_win, p_ctx), den = _softmax_parts([s_win, s_ctx])
            o = _dot(p_win.astype(BF16), vwin) + _dot(p_ctx.astype(BF16), vctx)
            res.append(o / den)
        outs.append(jnp.where(low, res[0], res[1]))
    o_ref[...] = jnp.concatenate(outs, axis=1).astype(BF16)


def _na_attn(l, qna, kna, vna, cache_k, cache_v, bias):
    nt = Q_TILES_PER_SEQ
    seq = lambda n: pl.BlockSpec((S_LAT, n), lambda b, t: (N_CTX // S_LAT + b, 0))
    cache = pl.BlockSpec((1, 1, PAST, D_NA), lambda b, t: (b, l, 0, 0))
    kind = lambda t: jnp.where(t == 0, 0, jnp.where(t == nt - 1, 2, 1))
    return pl.pallas_call(
        _na_kernel,
        grid=(B_LAT, nt),
        in_specs=[pl.BlockSpec((TM, D_NA), lambda b, t: (N_CTX // TM + b * nt + t, 0)),
                  seq(D_NA), seq(D_NA), cache, cache,
                  pl.BlockSpec((1, 1, H_NA, TM, NA_KEYS), lambda b, t: (l, kind(t), 0, 0, 0))],
        out_specs=pl.BlockSpec((TM, D_NA), lambda b, t: (b * nt + t, 0)),
        out_shape=jax.ShapeDtypeStruct((N_LAT, D_NA), BF16),
        compiler_params=_params(("arbitrary", "arbitrary")),
        name="na_attn",
    )(qna, kna, vna, cache_k, cache_v, bias)


def _na_bias_tables(rpb):
    n_dr, n_dc = 2 * WIN_H - 1, 2 * WIN_W - 1
    qc = np.arange(GRID_W)[:, None]
    kc = np.arange(GRID_W)[None, :]
    ws = np.clip(qc - WIN_W // 2, 0, GRID_W - WIN_W)
    col_ok = (kc >= ws) & (kc < ws + WIN_W)
    col_hot = ((kc - qc + WIN_W - 1)[..., None] == np.arange(n_dc)) & col_ok[..., None]
    col_hot = col_hot.reshape(GRID_W * GRID_W, n_dc).astype(np.float32)
    by_col = jnp.einsum("pb,lhab->lhap", jnp.asarray(col_hot), rpb, precision=lax.Precision.HIGHEST)
    by_col = jnp.where(jnp.asarray(col_ok), by_col.reshape(DEPTH, H_NA, n_dr, GRID_W, GRID_W), -jnp.inf)
    pad = jnp.full((DEPTH, H_NA, _BIAS_PAD, GRID_W, GRID_W), -jnp.inf, F32)
    blocks = jnp.concatenate([pad, by_col, pad], axis=2)
    pairs = jnp.concatenate([blocks[:, :, :-1], blocks[:, :, 1:]], axis=-1)
    n_pairs = n_dr + 2 * _BIAS_PAD - 1
    return pl.pallas_call(
        _na_bias_kernel,
        grid=(DEPTH, H_NA),
        in_specs=[pl.BlockSpec((1, 1, n_pairs, GRID_W, LANE), lambda l, h: (l, h, 0, 0, 0))],
        out_specs=pl.BlockSpec((1, len(_NA_KINDS), 1, TM, NA_KEYS), lambda l, h: (l, 0, h, 0, 0)),
        out_shape=jax.ShapeDtypeStruct((DEPTH, len(_NA_KINDS), H_NA, TM, NA_KEYS), F32),
        compiler_params=_params(("arbitrary", "arbitrary")),
        name="na_bias",
    )(pairs)


_NA_KINDS = ((0, 0), (NA_ROWS_PER_TILE, 0), (GRID_H - NA_ROWS_PER_TILE, GRID_H - NA_KEY_ROWS))
_BIAS_PAD = 4


def _na_bias_kernel(pairs_ref, o_ref):
    low = lax.broadcasted_iota(jnp.int32, (GRID_W, LANE), 1) < GRID_W
    ninf = jnp.full((GRID_W, LANE), -jnp.inf, F32)
    for kind, (r0, k0) in enumerate(_NA_KINDS):
        for i in range(NA_ROWS_PER_TILE):
            r = r0 + i
            rs = min(max(r - WIN_H // 2, 0), GRID_H - WIN_H)
            for m in range(NA_KEY_ROWS // 2):
                kr = k0 + 2 * m
                ok0 = rs <= kr < rs + WIN_H
                ok1 = rs <= kr + 1 < rs + WIN_H
                blk = pairs_ref[0, 0, kr - r + WIN_H - 1 + _BIAS_PAD]
                if ok0 and not ok1:
                    blk = jnp.where(low, blk, ninf)
                elif ok1 and not ok0:
                    blk = jnp.where(low, ninf, blk)
                elif not ok0:
                    blk = ninf
                o_ref[0, kind, 0, i * GRID_W:(i + 1) * GRID_W, m * LANE:(m + 1) * LANE] = blk


def _mla_kernel(q_ref, k_ref, v_ref, kc_ref, vc_ref, o_ref):
    low = _low_lanes()
    outs = []
    for hp in range(H_MLA // 2):
        vsl = slice(hp * LANE, (hp + 1) * LANE)
        vlat = v_ref[:, vsl]
        vctx = vc_ref[0, 0, :, vsl]
        res = []
        for h in (2 * hp, 2 * hp + 1):
            sl = slice(h * HP, (h + 1) * HP)
            q = q_ref[:, sl]
            s_ctx = _dot_nt(q, kc_ref[0, 0, :, sl])
            s_lat = _dot_nt(q, k_ref[:, sl])
            (p_ctx, p_lat), den = _softmax_parts([s_ctx, s_lat], MLA_SCALE)
            o = _dot(p_ctx.astype(BF16), vctx) + _dot(p_lat.astype(BF16), vlat)
            res.append(o / den)
        outs.append(jnp.where(low, res[0], res[1]))
    o_ref[...] = jnp.concatenate(outs, axis=1).astype(BF16)


def _mla_attn(l, qp, kp, vm, kc, vc):
    nt = Q_TILES_PER_SEQ
    seq = lambda n: pl.BlockSpec((S_LAT, n), lambda b, t: (N_CTX // S_LAT + b, 0))
    return pl.pallas_call(
        _mla_kernel,
        grid=(B_LAT, nt),
        in_specs=[pl.BlockSpec((TM, D_QP), lambda b, t: (N_CTX // TM + b * nt + t, 0)),
                  seq(D_QP), seq(D_MLA),
                  pl.BlockSpec((1, 1, PAST, D_QP), lambda b, t: (l, b, 0, 0)),
                  pl.BlockSpec((1, 1, PAST, D_MLA), lambda b, t: (l, b, 0, 0))],
        out_specs=pl.BlockSpec((TM, D_MLA), lambda b, t: (b * nt + t, 0)),
        out_shape=jax.ShapeDtypeStruct((N_LAT, D_MLA), BF16),
        compiler_params=_params(("arbitrary", "arbitrary")),
        name="mla_attn",
    )(qp, kp, vm, kc, vc)


def _route(sel, aff):
    rows = lambda a, g: [a[EPG * g + e:EPG * g + e + 1, :] for e in range(EPG)]
    best = None
    for g in range(N_GRP):
        v = rows(sel, g)
        score = v[0] + v[1]
        for a, b in ((0, 2), (0, 3), (1, 2), (1, 3), (2, 3)):
            score = jnp.maximum(score, v[a] + v[b])
        if best is None:
            best, grp = score, jnp.zeros_like(score)
        else:
            better = score > best
            best = jnp.where(better, score, best)
            grp = jnp.where(better, float(g), grp)
    zero = jnp.zeros_like(best)
    sv = [zero] * EPG
    av = [zero] * EPG
    for g in range(N_GRP):
        hit = grp == float(g)
        sg, ag = rows(sel, g), rows(aff, g)
        sv = [jnp.where(hit, sg[e], sv[e]) for e in range(EPG)]
        av = [jnp.where(hit, ag[e], av[e]) for e in range(EPG)]

    def argmax_first(vals, skip=None):
        bv, bi = None, None
        for e in range(EPG):
            v = vals[e] if skip is None else jnp.where(skip == float(e), -jnp.inf, vals[e])
            if bv is None:
                bv, bi = v, jnp.zeros_like(v)
            else:
                better = v > bv
                bv = jnp.where(better, v, bv)
                bi = jnp.where(better, float(e), bi)
        return bi

    i1 = argmax_first(sv)
    i2 = argmax_first(sv, skip=i1)
    lo = jnp.minimum(i1, i2)
    hi = jnp.maximum(i1, i2)
    pick = lambda idx: sum(jnp.where(idx == float(e), av[e], 0.0) for e in range(EPG))
    a_lo, a_hi = pick(lo), pick(hi)
    den = a_lo + a_hi
    pair = jnp.where(lo == 0.0, 0.0, jnp.where(lo == 1.0, 3.0, 5.0)) + hi - lo - 1.0
    return grp * float(N_PAIR) + pair, a_lo / den, a_hi / den


def _post_kernel(x_ref, octx_ref, ona_ref, omla_ref, mod_ref, n2_ref, wout_ref, wr_ref, br_ref,
                 x1_ref, h2_ref, route_ref, gate_ref, counts_ref, carry):
    i = pl.program_id(0)

    @pl.when(i == 0)
    def _():
        carry[...] = jnp.zeros_like(carry)

    is_ctx = i < CTX_TILES
    o_lat = jnp.concatenate([ona_ref[...], omla_ref[...]], axis=1)
    o = jnp.where(is_ctx, octx_ref[...], o_lat)
    m = mod_ref[0]
    x1 = x_ref[...] + m[2:3, :] * _dot(o, wout_ref[0])
    x1_ref[...] = x1
    h2 = _rms(x1, n2_ref[0]) * (1.0 + m[4:5, :]) + m[3:4, :]
    logits = lax.dot_general(wr_ref[...], h2, (((1,), (1,)), ((), ())),
                             preferred_element_type=F32, precision=lax.Precision.HIGHEST)
    aff = jax.nn.sigmoid(logits)
    bucket, g_lo, g_hi = _route(aff + br_ref[...], aff)

    hit = lax.broadcasted_iota(jnp.int32, (BUCKET_ROWS, TT), 0).astype(F32) == bucket
    earlier = lax.broadcasted_iota(jnp.int32, (TT, TT), 0) <= lax.broadcasted_iota(jnp.int32, (TT, TT), 1)
    incl = _dot(jnp.where(hit, 1.0, 0.0).astype(BF16), jnp.where(earlier, 1.0, 0.0).astype(BF16))
    base = carry[...]
    rank = jnp.sum(jnp.where(hit, incl + base[:, 0:1], 0.0), axis=0, keepdims=True) - 1.0
    carry[...] = base + incl[:, TT - 1:TT]
    counts_ref[...] = carry[...].astype(jnp.int32)
    route = jnp.concatenate([bucket, rank, jnp.zeros((6, TT), F32)], axis=0)
    route_ref[0] = route.astype(jnp.int32)
    gate_ref[0] = jnp.concatenate([g_lo, g_hi, jnp.zeros((6, TT), F32)], axis=0)
    _to_rows(h2_ref, h2, X_CHUNKS)


def _post(l, x, o_ctx, o_na, o_mla, mods, wts):
    tok = lambda n: pl.BlockSpec((TT, n), lambda i: (i, 0))
    lat = lambda n: pl.BlockSpec((TT, n), lambda i: (jnp.maximum(i - CTX_TILES, 0), 0))
    lay = lambda a: pl.BlockSpec((1,) + a.shape[1:], lambda i: (l,) + (0,) * (a.ndim - 1))
    full = lambda a: pl.BlockSpec(a.shape, lambda i: (0,) * a.ndim)
    return pl.pallas_call(
        _post_kernel,
        grid=(N_TILES,),
        in_specs=[tok(D), pl.BlockSpec((TT, D), lambda i: (_ctx_block(i), 0)), lat(D_NA), lat(D_MLA),
                  pl.BlockSpec((1, 6, D), lambda i: (l * N_MOD_ROWS + _mod_row(i), 0, 0)),
                  lay(wts["n2"]), lay(wts["w_out"]), full(wts["w_rt"]), full(wts["b_r"])],
        out_specs=[tok(D), pl.BlockSpec((TT * X_CHUNKS, LANE), lambda i: (i, 0)),
                   pl.BlockSpec((1, 8, TT), lambda i: (i, 0, 0)),
                   pl.BlockSpec((1, 8, TT), lambda i: (i, 0, 0)),
                   pl.BlockSpec((BUCKET_ROWS, LANE), lambda i: (0, 0))],
        out_shape=[jax.ShapeDtypeStruct((N_TOK, D), F32),
                   jax.ShapeDtypeStruct((N_TOK * X_CHUNKS, LANE), F32),
                   jax.ShapeDtypeStruct((N_TILES, 8, TT), jnp.int32),
                   jax.ShapeDtypeStruct((N_TILES, 8, TT), F32),
                   jax.ShapeDtypeStruct((BUCKET_ROWS, LANE), jnp.int32)],
        scratch_shapes=[pltpu.VMEM((BUCKET_ROWS, LANE), F32)],
        compiler_params=_params(("arbitrary",)),
        name="post",
    )(x, o_ctx, o_na, o_mla, mods, wts["n2"], wts["w_out"], wts["w_rt"], wts["b_r"])


def _moe_index_kernel(counts, elo, ehi, valid, total, first_tile):
    def per_bucket(b, cursor):
        cnt = counts[b]
        n_tiles = (cnt + (TM - 1)) // TM
        first_tile[b] = cursor
        grp = b // N_PAIR
        pair = b - grp * N_PAIR
        lo = jnp.where(pair < 3, 0, jnp.where(pair < 5, 1, 2))
        hi = jnp.where(pair < 3, pair + 1, jnp.where(pair < 5, pair - 1, 3))

        def per_tile(k, c):
            elo[cursor + k] = grp * EPG + lo
            ehi[cursor + k] = grp * EPG + hi
            valid[cursor + k] = jnp.minimum(cnt - k * TM, TM)
            return c
        lax.fori_loop(0, n_tiles, per_tile, 0)
        return cursor + n_tiles
    used = lax.fori_loop(0, N_BUCKET, per_bucket, 0)
    total[0] = used

    def unused_tile(k, c):
        elo[k] = elo[used - 1]
        ehi[k] = ehi[used - 1]
        valid[k] = 0
        return c
    lax.fori_loop(used, MOE_TILES, unused_tile, 0)

    def unused_bucket_row(b, c):
        first_tile[b] = used
        return c
    lax.fori_loop(N_BUCKET, BUCKET_ROWS, unused_bucket_row, 0)


def _moe_index(counts):
    smem = pl.BlockSpec(memory_space=pltpu.SMEM)
    tiles = jax.ShapeDtypeStruct((MOE_TILES,), jnp.int32)
    return pl.pallas_call(
        _moe_index_kernel,
        in_specs=[smem],
        out_specs=[smem] * 5,
        out_shape=[tiles, tiles, tiles, jax.ShapeDtypeStruct((1,), jnp.int32),
                   jax.ShapeDtypeStruct((BUCKET_ROWS,), jnp.int32)],
        name="moe_index",
    )(counts)


def _moe_kernel(elo_ref, ehi_ref, valid_ref, total_ref, slot_ref, glo_tok,
                h2_hbm, wg1, wu1, wd1, wg2, wu2, wd2, out_hbm,
                xbuf, ybuf, gvec, src, dst, glo, gsem, ssem, vsem):
    del elo_ref, ehi_ref
    t = pl.program_id(0)
    total = total_ref[0]
    par = t % 2

    def gate_copy(tile, p):
        return pltpu.make_async_copy(glo.at[pl.ds(pl.multiple_of(tile * TM, TM), TM)], gvec.at[p], vsem.at[p])

    def start_gather(tile, p):
        gate_copy(tile, p).start()
        def body(j, c):
            row = pl.multiple_of(src[tile * TM + j] * X_CHUNKS, X_CHUNKS)
            pltpu.make_async_copy(h2_hbm.at[pl.ds(row, X_CHUNKS), :],
                                  xbuf.at[p, pl.ds(j * X_CHUNKS, X_CHUNKS), :], gsem.at[p]).start()
            return c
        lax.fori_loop(0, TM, body, 0, unroll=DMA_UNROLL)

    def start_scatter(tile, p):
        def body(j, c):
            row = pl.multiple_of(dst[tile * TM + j] * Y_CHUNKS, Y_CHUNKS)
            pltpu.make_async_copy(ybuf.at[p, pl.ds(j * Y_CHUNKS, Y_CHUNKS), :],
                                  out_hbm.at[pl.ds(row, Y_CHUNKS), :], ssem.at[p]).start()
            return c
        lax.fori_loop(0, TM, body, 0, unroll=DMA_UNROLL)

    def wait_gather(p):
        pltpu.make_async_copy(h2_hbm.at[pl.ds(0, TM * X_CHUNKS), :], xbuf.at[p], gsem.at[p]).wait()
        gate_copy(0, p).wait()

    def wait_scatter(p):
        pltpu.make_async_copy(ybuf.at[p], out_hbm.at[pl.ds(0, TM * Y_CHUNKS), :], ssem.at[p]).wait()

    @pl.when(t == 0)
    def _():
        def place(i, c):
            s = slot_ref[i]
            src[s] = i
            dst[s] = i
            glo[s] = glo_tok[i]
            return c
        lax.fori_loop(0, N_TOK, place, 0, unroll=DMA_UNROLL)

        def pad_tile(tile, c):
            def pad_slot(j, c2):
                src[tile * TM + j] = src[tile * TM]
                dst[tile * TM + j] = N_TOK + j
                glo[tile * TM + j] = 0.0
                return c2
            return lax.fori_loop(valid_ref[tile], TM, pad_slot, c)
        lax.fori_loop(0, total, pad_tile, 0)
        ybuf[0] = jnp.zeros((TM * Y_CHUNKS, LANE), F32)
        spare = pltpu.make_async_copy(ybuf.at[0], out_hbm.at[pl.ds(N_TOK * Y_CHUNKS, TM * Y_CHUNKS), :], ssem.at[0])
        spare.start()
        spare.wait()
        start_gather(0, 0)

    @pl.when(t + 1 < total)
    def _():
        start_gather(t + 1, 1 - par)

    @pl.when(t < total)
    def _():
        wait_gather(par)

        @pl.when(t >= 2)
        def _():
            wait_scatter(par)

        x = _from_rows(xbuf, TM, X_CHUNKS, 0, X_CHUNKS, lead=(par,)).astype(BF16)
        g_row = gvec[pl.ds(par, 1), :]
        g_lo = jnp.transpose(jnp.concatenate([g_row, jnp.zeros((7, TM), F32)], axis=0))[:, 0:1]
        y = None
        for gate, (wg, wu, wd) in ((g_lo, (wg1, wu1, wd1)), (1.0 - g_lo, (wg2, wu2, wd2))):
            a = _dot(x, wg[0, 0])
            hid = a * jax.nn.sigmoid(a) * _dot(x, wu[0, 0]) * gate
            part = _dot(hid.astype(BF16), wd[0, 0])
            y = part if y is None else y + part
        _to_rows(ybuf, y, Y_CHUNKS, lead=(par,))
        start_scatter(t, par)

        @pl.when(t == total - 1)
        def _():
            @pl.when(t >= 1)
            def _():
                wait_scatter(1 - par)
            wait_scatter(par)


def _moe(l, h2, route, gate, counts, wts):
    bucket = route[:, 0, :].reshape(N_TOK)
    rank = route[:, 1, :].reshape(N_TOK)
    g_lo = gate[:, 0, :].reshape(N_TOK)
    e_lo, e_hi, valid, total, first_tile = _moe_index(counts[:, 0])
    buckets = jnp.arange(N_BUCKET, dtype=jnp.int32)
    slot = jnp.where(bucket[:, None] == buckets[None, :], (first_tile[:N_BUCKET] * TM)[None, :], 0).sum(axis=1) + rank
    up = lambda which: pl.BlockSpec((1, 1, D, D_EXP), lambda t, *pf: (l, pf[which][t], 0, 0))
    down = lambda which: pl.BlockSpec((1, 1, D_EXP, D), lambda t, *pf: (l, pf[which][t], 0, 0))
    first, second = 0, 1
    slots = MOE_TILES * TM
    return pl.pallas_call(
        _moe_kernel,
        grid_spec=pltpu.PrefetchScalarGridSpec(
            num_scalar_prefetch=6,
            grid=(MOE_TILES,),
            in_specs=[pl.BlockSpec(memory_space=pl.ANY),
                      up(first), up(first), down(first), up(second), up(second), down(second)],
            out_specs=pl.BlockSpec(memory_space=pl.ANY),
            scratch_shapes=[pltpu.VMEM((2, TM * X_CHUNKS, LANE), F32), pltpu.VMEM((2, TM * Y_CHUNKS, LANE), F32),
                            pltpu.VMEM((2, TM), F32),
                            pltpu.SMEM((slots,), jnp.int32), pltpu.SMEM((slots,), jnp.int32),
                            pltpu.SMEM((slots,), F32),
                            pltpu.SemaphoreType.DMA((2,)), pltpu.SemaphoreType.DMA((2,)),
                            pltpu.SemaphoreType.DMA((2,))],
        ),
        out_shape=jax.ShapeDtypeStruct(((N_TOK + TM) * Y_CHUNKS, LANE), F32),
        compiler_params=_params(("arbitrary",)),
        name="moe",
    )(e_lo, e_hi, valid, total, slot, g_lo, h2,
      wts["w_gate"], wts["w_up"], wts["w_down"], wts["w_gate"], wts["w_up"], wts["w_down"])


def _final_kernel(x1_ref, moe_ref, mod_ref, g_ref, yc_ref, yl_ref):
    i = pl.program_id(0)
    moe = _from_rows(moe_ref, TT, Y_CHUNKS, 0, Y_CHUNKS)
    y = _rms(x1_ref[...] + mod_ref[0, 5:6, :] * moe, g_ref[...])

    @pl.when(i < CTX_TILES)
    def _():
        yc_ref[...] = y

    @pl.when(i >= CTX_TILES)
    def _():
        yl_ref[...] = y


def _final(x1, moe, mods, g):
    tok = pl.BlockSpec((TT, D), lambda i: (i, 0))
    l = DEPTH - 1
    return pl.pallas_call(
        _final_kernel,
        grid=(N_TILES,),
        in_specs=[tok, pl.BlockSpec((TT * Y_CHUNKS, LANE), lambda i: (i, 0)),
                  pl.BlockSpec((1, 6, D), lambda i: (l * N_MOD_ROWS + _mod_row(i), 0, 0)),
                  pl.BlockSpec((1, D), lambda i: (0, 0))],
        out_specs=[pl.BlockSpec((TT, D), lambda i: (_ctx_block(i), 0)),
                   pl.BlockSpec((TT, D), lambda i: (jnp.maximum(i - CTX_TILES, 0), 0))],
        out_shape=[jax.ShapeDtypeStruct((N_CTX, D), F32), jax.ShapeDtypeStruct((N_LAT, D), F32)],
        compiler_params=_params(("arbitrary",)),
        name="final",
    )(x1, moe, mods, g)


def _swap_halves(w):
    q = ROPE // 4
    return jnp.concatenate([-w[..., q:2 * q], w[..., :q], -w[..., 3 * q:], w[..., 2 * q:3 * q]], axis=-1)


def _pad_head(nope, rope):
    z = jnp.zeros(rope.shape[:-1] + (HP - NOPE - ROPE,), rope.dtype)
    return jnp.concatenate([nope, rope, z], axis=-1)


def _prepare_weights(norm1_g, norm2_g, w_in, q_norm_g, w_uq, kv_norm_g, w_ukv, w_out, w_router, b_router,
                     w_gate, w_up, w_down):
    o = 3 * D_NA + Q_LORA + KV_LORA
    w_kr = w_in[:, :, o:]
    zeros_nope = jnp.zeros((DEPTH, D, NOPE), F32)
    w_in_p = jnp.concatenate([w_in[:, :, :o], _pad_head(zeros_nope, w_kr),
                              _pad_head(zeros_nope, _swap_halves(w_kr))], axis=-1)
    wq = w_uq.reshape(DEPTH, Q_LORA, H_MLA, NOPE + ROPE)
    plain = _pad_head(wq[..., :NOPE], wq[..., NOPE:]).reshape(DEPTH, Q_LORA, D_QP)
    swapped = _pad_head(jnp.zeros_like(wq[..., :NOPE]), _swap_halves(wq[..., NOPE:])).reshape(DEPTH, Q_LORA, D_QP)
    wkv = w_ukv.reshape(DEPTH, KV_LORA, H_MLA, NOPE + VH)
    w_uk_p = jnp.concatenate([wkv[..., :NOPE], jnp.zeros((DEPTH, KV_LORA, H_MLA, HP - NOPE), F32)],
                             axis=-1).reshape(DEPTH, KV_LORA, D_QP)
    w_uv = wkv[..., NOPE:].reshape(DEPTH, KV_LORA, D_MLA)
    return {
        "n1": norm1_g.reshape(DEPTH, 1, D),
        "n2": norm2_g.reshape(DEPTH, 1, D),
        "w_in": w_in_p.astype(BF16),
        "qn": q_norm_g.reshape(DEPTH, 1, Q_LORA),
        "w_uq": jnp.concatenate([plain, swapped], axis=-1).astype(BF16),
        "kvn": kv_norm_g.reshape(DEPTH, 1, KV_LORA),
        "w_uk": w_uk_p.astype(BF16),
        "w_uv": w_uv.astype(BF16),
        "w_out": w_out.astype(BF16),
        "w_rt": w_router.T,
        "b_r": b_router.reshape(N_EXP, 1),
        "w_gate": w_gate.astype(BF16),
        "w_up": w_up.astype(BF16),
        "w_down": w_down.astype(BF16),
    }


def _rope_tables():
    half = ROPE // 2
    freqs = 1.0 / (ROPE_THETA ** (np.arange(0, half, 2, dtype=np.float32) / half))
    pos = np.arange(S_LAT)
    ang_r = (pos // GRID_W).astype(np.float32)[:, None] * freqs
    ang_c = (pos % GRID_W).astype(np.float32)[:, None] * freqs
    ang = np.concatenate([ang_r, ang_r, ang_c, ang_c], axis=1)
    cos = np.ones((TT + S_LAT, LANE), np.float32)
    sin = np.zeros((TT + S_LAT, LANE), np.float32)
    cos[TT:, NOPE:NOPE + ROPE] = np.cos(ang)
    sin[TT:, NOPE:NOPE + ROPE] = np.sin(ang)
    return jnp.asarray(cos), jnp.asarray(sin)


def kernel(x_prompt, x_sample, cache_nat_k, cache_nat_v, cache_mla_ckv, cache_mla_krope, c, c_ctx, norm1_g, norm2_g, w_ada, b_ada, w_in, q_norm_g, w_uq, kv_norm_g, w_ukv, rpb, w_out, w_router, b_router, w_gate, w_up, w_down, final_norm_g):
    wts = _prepare_weights(norm1_g, norm2_g, w_in, q_norm_g, w_uq, kv_norm_g, w_ukv, w_out, w_router, b_router,
                           w_gate, w_up, w_down)
    tabs = _rope_tables()
    cond = jnp.concatenate([c_ctx[None, :], c, jnp.zeros((N_MOD_ROWS - 1 - B_LAT, D), F32)], axis=0)
    mods = _ada(cond, w_ada, b_ada).reshape(DEPTH * N_MOD_ROWS, 6, D)
    place = jnp.asarray(np.eye(ROPE, HP, k=NOPE, dtype=np.float32)).astype(BF16)
    kc_all, vc_all = _ctxkv(cache_mla_ckv, cache_mla_krope, wts["w_uk"], wts["w_uv"], place)
    cache_k = cache_nat_k.reshape(B_LAT, DEPTH, PAST, D_NA)
    cache_v = cache_nat_v.reshape(B_LAT, DEPTH, PAST, D_NA)
    bias = _na_bias_tables(rpb)

    x1 = moe = None
    new_k, new_v, new_ckv, new_kr = [], [], [], []
    for l in range(DEPTH):
        xin = (x_prompt.reshape(N_CTX, D), x_sample.reshape(N_LAT, D)) if l == 0 else (x1, moe)
        x, qna, kna, vna, qp, kp, vm, k_new, v_new, ckv_new, kr_new = _pre(l, l == 0, xin, mods, wts, tabs)
        new_k.append(k_new)
        new_v.append(v_new)
        new_ckv.append(ckv_new)
        new_kr.append(kr_new)
        o_ctx = _ctx_attn(qna, kna, vna, qp, kp, vm)
        o_na = _na_attn(l, qna, kna, vna, cache_k, cache_v, bias)
        o_mla = _mla_attn(l, qp, kp, vm, kc_all, vc_all)
        x1, h2, route, gate, counts = _post(l, x, o_ctx, o_na, o_mla, mods, wts)
        moe = _moe(l, h2, route, gate, counts, wts)
    y_ctx, y_lat = _final(x1, moe, mods, final_norm_g.reshape(1, D))

    stack = lambda parts, tail: jnp.stack([p.reshape((B_CTX, S_CTX) + tail) for p in parts], axis=1)
    return (y_ctx.reshape(B_CTX, S_CTX, D), y_lat.reshape(B_LAT, S_LAT, D),
            stack(new_k, (H_NA, DH_NA)), stack(new_v, (H_NA, DH_NA)),
            stack(new_ckv, (KV_LORA,)), stack(new_kr, (ROPE,)))
```

```python
import functools

import numpy as np
import jax
import jax.numpy as jnp
from jax import lax
from jax.experimental import pallas as pl
from jax.experimental.pallas import tpu as pltpu

F32 = jnp.float32
BF16 = jnp.bfloat16

D = 1024
B_CTX, S_CTX = 32, 256
B_LAT, S_LAT = 8, 2048
DEPTH = 4
PAST = 256
GRID_W, GRID_H = 64, 32
H_NA, DH_NA = 8, 64
WIN_H, WIN_W = 8, 16
H_MLA, NOPE, ROPE, VH = 8, 64, 32, 64
Q_LORA, KV_LORA = 256, 128
D_NA = H_NA * DH_NA
D_MLA = H_MLA * VH
N_EXP, N_GRP, EPG = 16, 4, 4
D_EXP = 512
ROPE_THETA = 10000.0
EPS = 1e-6
NA_SCALE = DH_NA ** -0.5
MLA_SCALE = (NOPE + ROPE) ** -0.5
LOG2E = 1.4426950408889634

N_CTX = B_CTX * S_CTX
N_LAT = B_LAT * S_LAT
N_TOK = N_CTX + N_LAT
TM = 256
TT = 512
CTX_TILES = N_CTX // TT
LAT_TILES = N_LAT // TT
N_TILES = CTX_TILES + LAT_TILES
LAT_TILES_PER_SEQ = S_LAT // TT
Q_TILES_PER_SEQ = S_LAT // TM
LANE = 128
HP = LANE
D_QP = H_MLA * HP
W_IN_COLS = 3 * D_NA + Q_LORA + KV_LORA + 2 * LANE
N_MOD_ROWS = 16
N_PAIR = 6
N_BUCKET = N_GRP * N_PAIR
MOE_TILES = N_TOK // TM + N_BUCKET
BUCKET_ROWS = 32
Y_CHUNKS = D // LANE
X_CHUNKS = D // LANE


def _to_rows(ref, value, chunks, lead=()):
    t = value.shape[0]
    for c in range(value.shape[1] // LANE):
        ref[lead + (pl.ds(c, t, stride=chunks), slice(None))] = value[:, c * LANE:(c + 1) * LANE]


def _from_rows(ref, t, chunks, first, n, lead=()):
    parts = [ref[lead + (pl.ds(first + c, t, stride=chunks), slice(None))] for c in range(n)]
    return parts[0] if n == 1 else jnp.concatenate(parts, axis=1)
NA_ROWS_PER_TILE = TM // GRID_W
NA_KEY_ROWS = 12
NA_KEYS = NA_KEY_ROWS * GRID_W
VMEM_LIMIT = 48 * 1024 * 1024
DMA_UNROLL = 8


def _dot(a, b):
    return jnp.dot(a, b, preferred_element_type=F32)


def _dot_nt(a, b):
    return lax.dot_general(a, b, (((1,), (1,)), ((), ())), preferred_element_type=F32)


def _rms(x, g):
    return x * lax.rsqrt(jnp.mean(x * x, axis=-1, keepdims=True) + EPS) * g


def _mod_row(i):
    return jnp.where(i < CTX_TILES, 0, 1 + (i - CTX_TILES) // LAT_TILES_PER_SEQ)


def _pos_block(i):
    return jnp.where(i < CTX_TILES, 0, 1 + (i - CTX_TILES) % LAT_TILES_PER_SEQ)


def _ctx_block(i):
    return jnp.minimum(i, CTX_TILES - 1)


def _params(sem):
    return pltpu.CompilerParams(dimension_semantics=sem, vmem_limit_bytes=VMEM_LIMIT)


def _ada_kernel(cond_ref, w_ref, b_ref, o_ref):
    cond = cond_ref[...]
    act = cond * jax.nn.sigmoid(cond)
    o_ref[0] = jnp.dot(act, w_ref[0], preferred_element_type=F32,
                       precision=lax.Precision.HIGHEST) + b_ref[0]


def _ada(cond, w_ada, b_ada):
    nt = 1536
    return pl.pallas_call(
        _ada_kernel,
        grid=(DEPTH, 6 * D // nt),
        in_specs=[pl.BlockSpec((N_MOD_ROWS, D), lambda l, j: (0, 0)),
                  pl.BlockSpec((1, D, nt), lambda l, j: (l, 0, j)),
                  pl.BlockSpec((1, 1, nt), lambda l, j: (l, 0, j))],
        out_specs=pl.BlockSpec((1, N_MOD_ROWS, nt), lambda l, j: (l, 0, j)),
        out_shape=jax.ShapeDtypeStruct((DEPTH, N_MOD_ROWS, 6 * D), F32),
        compiler_params=_params(("arbitrary", "arbitrary")),
        name="ada",
    )(cond, w_ada, b_ada.reshape(DEPTH, 1, 6 * D))


def _ctxkv_kernel(ckv_ref, kr_ref, wuk_ref, wuv_ref, place_ref, kc_ref, vc_ref):
    ckv = ckv_ref[0, 0].astype(BF16)
    kr = _dot(kr_ref[0, 0].astype(BF16), place_ref[...])
    kc_ref[0, 0] = (_dot(ckv, wuk_ref[0]) + jnp.tile(kr, (1, H_MLA))).astype(BF16)
    vc_ref[0, 0] = _dot(ckv, wuv_ref[0]).astype(BF16)


def _ctxkv(cache_ckv, cache_krope, w_uk_p, w_uv, place):
    return pl.pallas_call(
        _ctxkv_kernel,
        grid=(DEPTH, B_LAT),
        in_specs=[pl.BlockSpec((1, 1, PAST, KV_LORA), lambda l, b: (b, l, 0, 0)),
                  pl.BlockSpec((1, 1, PAST, ROPE), lambda l, b: (b, l, 0, 0)),
                  pl.BlockSpec((1, KV_LORA, D_QP), lambda l, b: (l, 0, 0)),
                  pl.BlockSpec((1, KV_LORA, D_MLA), lambda l, b: (l, 0, 0)),
                  pl.BlockSpec((ROPE, HP), lambda l, b: (0, 0))],
        out_specs=[pl.BlockSpec((1, 1, PAST, D_QP), lambda l, b: (l, b, 0, 0)),
                   pl.BlockSpec((1, 1, PAST, D_MLA), lambda l, b: (l, b, 0, 0))],
        out_shape=[jax.ShapeDtypeStruct((DEPTH, B_LAT, PAST, D_QP), BF16),
                   jax.ShapeDtypeStruct((DEPTH, B_LAT, PAST, D_MLA), BF16)],
        compiler_params=_params(("arbitrary", "arbitrary")),
        name="ctxkv",
    )(cache_ckv, cache_krope, w_uk_p, w_uv, place)


def _pre_kernel(first, *refs):
    if first:
        xc_ref, xl_ref, *rest = refs
    else:
        x1_ref, moe_ref, modp_ref, *rest = refs
    (mod_ref, n1_ref, win_ref, qn_ref, wuq_ref, kvn_ref, wuk_ref, wuv_ref, cos_ref, sin_ref,
     xo_ref, qna_ref, kna_ref, vna_ref, qp_ref, kp_ref, vm_ref, knew_ref, vnew_ref, ckvnew_ref, krnew_ref) = rest
    i = pl.program_id(0)

    if first:
        x = jnp.where(i < CTX_TILES, xc_ref[...], xl_ref[...])
    else:
        x = x1_ref[...] + modp_ref[0, 5:6, :] * _from_rows(moe_ref, TT, Y_CHUNKS, 0, Y_CHUNKS)
    xo_ref[...] = x
    m = mod_ref[0]
    h = _rms(x, n1_ref[0]) * (1.0 + m[1:2, :]) + m[0:1, :]
    proj = _dot(h.astype(BF16), win_ref[0])
    k_na = proj[:, D_NA:2 * D_NA]
    v_na = proj[:, 2 * D_NA:3 * D_NA]
    qna_ref[...] = (proj[:, :D_NA] * NA_SCALE).astype(BF16)
    kna_ref[...] = k_na.astype(BF16)
    vna_ref[...] = v_na.astype(BF16)
    o = 3 * D_NA
    cq = proj[:, o:o + Q_LORA]
    ckv = proj[:, o + Q_LORA:o + Q_LORA + KV_LORA]
    kr = proj[:, o + Q_LORA + KV_LORA:o + Q_LORA + KV_LORA + LANE]
    kr_sw = proj[:, o + Q_LORA + KV_LORA + LANE:]
    cos = cos_ref[...]
    sin = sin_ref[...]

    cqn = _rms(cq, qn_ref[0]).astype(BF16)
    q2 = _dot(cqn, wuq_ref[0])
    qp = q2[:, :D_QP] * jnp.tile(cos, (1, H_MLA)) + q2[:, D_QP:] * jnp.tile(sin, (1, H_MLA))
    qp_ref[...] = qp.astype(BF16)

    ckvn = _rms(ckv, kvn_ref[0])
    ckvb = ckvn.astype(BF16)
    kr_rot = kr * cos + kr_sw * sin
    kp_ref[...] = (_dot(ckvb, wuk_ref[0]) + jnp.tile(kr_rot, (1, H_MLA))).astype(BF16)
    vm_ref[...] = _dot(ckvb, wuv_ref[0]).astype(BF16)

    @pl.when(i < CTX_TILES)
    def _():
        knew_ref[...] = k_na
        vnew_ref[...] = v_na
        ckvnew_ref[...] = ckvn
        krnew_ref[...] = kr[:, NOPE:NOPE + ROPE]


def _pre(l, first, xin, mods, wts, tabs):
    tok = lambda n: pl.BlockSpec((TT, n), lambda i: (i, 0))
    ctx = lambda n: pl.BlockSpec((TT, n), lambda i: (_ctx_block(i), 0))
    lat = lambda n: pl.BlockSpec((TT, n), lambda i: (jnp.maximum(i - CTX_TILES, 0), 0))
    lay = lambda a: pl.BlockSpec((1,) + a.shape[1:], lambda i: (l,) + (0,) * (a.ndim - 1))
    in_specs = []
    args = []
    if first:
        in_specs += [ctx(D), lat(D)]
        args += list(xin)
    else:
        x1, moe = xin
        in_specs += [tok(D), pl.BlockSpec((TT * Y_CHUNKS, LANE), lambda i: (i, 0)),
                     pl.BlockSpec((1, 6, D), lambda i: ((l - 1) * N_MOD_ROWS + _mod_row(i), 0, 0))]
        args += [x1, moe, mods]
    in_specs.append(pl.BlockSpec((1, 6, D), lambda i: (l * N_MOD_ROWS + _mod_row(i), 0, 0)))
    args.append(mods)
    for name in ("n1", "w_in", "qn", "w_uq", "kvn", "w_uk", "w_uv"):
        in_specs.append(lay(wts[name]))
        args.append(wts[name])
    for t in tabs:
        in_specs.append(pl.BlockSpec((TT, LANE), lambda i: (_pos_block(i), 0)))
        args.append(t)
    out_specs = [tok(D), tok(D_NA), tok(D_NA), tok(D_NA), tok(D_QP), tok(D_QP), tok(D_MLA),
                 ctx(D_NA), ctx(D_NA), ctx(KV_LORA), ctx(ROPE)]
    out_shape = [jax.ShapeDtypeStruct((N_TOK, D), F32),
                 jax.ShapeDtypeStruct((N_TOK, D_NA), BF16),
                 jax.ShapeDtypeStruct((N_TOK, D_NA), BF16),
                 jax.ShapeDtypeStruct((N_TOK, D_NA), BF16),
                 jax.ShapeDtypeStruct((N_TOK, D_QP), BF16),
                 jax.ShapeDtypeStruct((N_TOK, D_QP), BF16),
                 jax.ShapeDtypeStruct((N_TOK, D_MLA), BF16),
                 jax.ShapeDtypeStruct((N_CTX, D_NA), F32),
                 jax.ShapeDtypeStruct((N_CTX, D_NA), F32),
                 jax.ShapeDtypeStruct((N_CTX, KV_LORA), F32),
                 jax.ShapeDtypeStruct((N_CTX, ROPE), F32)]
    return pl.pallas_call(
        functools.partial(_pre_kernel, first),
        grid=(N_TILES,),
        in_specs=in_specs,
        out_specs=out_specs,
        out_shape=out_shape,
        compiler_params=_params(("arbitrary",)),
        name="pre",
    )(*args)


def _softmax_parts(parts, scale=None):
    m = parts[0].max(axis=-1, keepdims=True)
    for s in parts[1:]:
        m = jnp.maximum(m, s.max(axis=-1, keepdims=True))
    if scale is None:
        ps = [jnp.exp(s - m) for s in parts]
    else:
        ps = [jnp.exp2((s - m) * (scale * LOG2E)) for s in parts]
    den = ps[0].sum(axis=-1, keepdims=True)
    for p in ps[1:]:
        den = den + p.sum(axis=-1, keepdims=True)
    return ps, den


def _low_lanes():
    return lax.broadcasted_iota(jnp.int32, (TM, LANE), 1) < DH_NA


def _split_pair(qpair, low):
    zero = jnp.zeros_like(qpair)
    return jnp.where(low, qpair, zero), jnp.where(low, zero, qpair)


def _ctx_attn_kernel(qna_ref, kna_ref, vna_ref, qp_ref, kp_ref, vm_ref, o_ref):
    low = _low_lanes()
    outs = []
    for hp in range(H_NA // 2):
        sl = slice(hp * LANE, (hp + 1) * LANE)
        kpair = kna_ref[:, sl]
        vpair = vna_ref[:, sl]
        res = []
        for qh in _split_pair(qna_ref[:, sl], low):
            (p,), den = _softmax_parts([_dot_nt(qh, kpair)])
            res.append(_dot(p.astype(BF16), vpair) / den)
        outs.append(jnp.where(low, res[0], res[1]))
    for hp in range(H_MLA // 2):
        vpair = vm_ref[:, hp * LANE:(hp + 1) * LANE]
        res = []
        for h in (2 * hp, 2 * hp + 1):
            sl = slice(h * HP, (h + 1) * HP)
            (p,), den = _softmax_parts([_dot_nt(qp_ref[:, sl], kp_ref[:, sl])], MLA_SCALE)
            res.append(_dot(p.astype(BF16), vpair) / den)
        outs.append(jnp.where(low, res[0], res[1]))
    o_ref[...] = jnp.concatenate(outs, axis=1).astype(BF16)


def _ctx_attn(qna, kna, vna, qp, kp, vm):
    tok = lambda n: pl.BlockSpec((S_CTX, n), lambda i: (i, 0))
    return pl.pallas_call(
        _ctx_attn_kernel,
        grid=(B_CTX,),
        in_specs=[tok(D_NA), tok(D_NA), tok(D_NA), tok(D_QP), tok(D_QP), tok(D_MLA)],
        out_specs=tok(D_NA + D_MLA),
        out_shape=jax.ShapeDtypeStruct((N_CTX, D_NA + D_MLA), BF16),
        compiler_params=_params(("arbitrary",)),
        name="ctx_attn",
    )(qna, kna, vna, qp, kp, vm)


def _na_kernel(q_ref, k_ref, v_ref, kc_ref, vc_ref, bias_ref, o_ref):
    t = pl.program_id(1)
    row0 = jnp.clip(t * NA_ROWS_PER_TILE - WIN_H // 2, 0, GRID_H - NA_KEY_ROWS)
    ks = pl.multiple_of(row0 * GRID_W, GRID_W)
    low = _low_lanes()
    outs = []
    for hp in range(H_NA // 2):
        sl = slice(hp * LANE, (hp + 1) * LANE)
        kwin = k_ref[pl.ds(ks, NA_KEYS), sl]
        vwin = v_ref[pl.ds(ks, NA_KEYS), sl]
        kctx = kc_ref[0, 0, :, sl].astype(BF16)
        vctx = vc_ref[0, 0, :, sl].astype(BF16)
        res = []
        for half, qh in enumerate(_split_pair(q_ref[:, sl], low)):
            s_win = _dot_nt(qh, kwin) + bias_ref[0, 0, 2 * hp + half]
            s_ctx = _dot_nt(qh, kctx)
            (p_win, p_ctx), den = _softmax_parts([s_win, s_ctx])
            o = _dot(p_win.astype(BF16), vwin) + _dot(p_ctx.astype(BF16), vctx)
            res.append(o / den)
        outs.append(jnp.where(low, res[0], res[1]))
    o_ref[...] = jnp.concatenate(outs, axis=1).astype(BF16)


def _na_attn(l, qna, kna, vna, cache_k, cache_v, bias):
    nt = Q_TILES_PER_SEQ
    seq = lambda n: pl.BlockSpec((S_LAT, n), lambda b, t: (N_CTX // S_LAT + b, 0))
    cache = pl.BlockSpec((1, 1, PAST, D_NA), lambda b, t: (b, l, 0, 0))
    kind = lambda t: jnp.where(t == 0, 0, jnp.where(t == nt - 1, 2, 1))
    return pl.pallas_call(
        _na_kernel,
        grid=(B_LAT, nt),
        in_specs=[pl.BlockSpec((TM, D_NA), lambda b, t: (N_CTX // TM + b * nt + t, 0)),
                  seq(D_NA), seq(D_NA), cache, cache,
                  pl.BlockSpec((1, 1, H_NA, TM, NA_KEYS), lambda b, t: (l, kind(t), 0, 0, 0))],
        out_specs=pl.BlockSpec((TM, D_NA), lambda b, t: (b * nt + t, 0)),
        out_shape=jax.ShapeDtypeStruct((N_LAT, D_NA), BF16),
        compiler_params=_params(("arbitrary", "arbitrary")),
        name="na_attn",
    )(qna, kna, vna, cache_k, cache_v, bias)


def _na_bias_tables(rpb):
    n_dr, n_dc = 2 * WIN_H - 1, 2 * WIN_W - 1
    qc = np.arange(GRID_W)[:, None]
    kc = np.arange(GRID_W)[None, :]
    ws = np.clip(qc - WIN_W // 2, 0, GRID_W - WIN_W)
    col_ok = (kc >= ws) & (kc < ws + WIN_W)
    col_hot = ((kc - qc + WIN_W - 1)[..., None] == np.arange(n_dc)) & col_ok[..., None]
    col_hot = col_hot.reshape(GRID_W * GRID_W, n_dc).astype(np.float32)
    by_col = jnp.einsum("pb,lhab->lhap", jnp.asarray(col_hot), rpb, precision=lax.Precision.HIGHEST)
    by_col = jnp.where(jnp.asarray(col_ok), by_col.reshape(DEPTH, H_NA, n_dr, GRID_W, GRID_W), -jnp.inf)
    pad = jnp.full((DEPTH, H_NA, _BIAS_PAD, GRID_W, GRID_W), -jnp.inf, F32)
    blocks = jnp.concatenate([pad, by_col, pad], axis=2)
    pairs = jnp.concatenate([blocks[:, :, :-1], blocks[:, :, 1:]], axis=-1)
    n_pairs = n_dr + 2 * _BIAS_PAD - 1
    return pl.pallas_call(
        _na_bias_kernel,
        grid=(DEPTH, H_NA),
        in_specs=[pl.BlockSpec((1, 1, n_pairs, GRID_W, LANE), lambda l, h: (l, h, 0, 0, 0))],
        out_specs=pl.BlockSpec((1, len(_NA_KINDS), 1, TM, NA_KEYS), lambda l, h: (l, 0, h, 0, 0)),
        out_shape=jax.ShapeDtypeStruct((DEPTH, len(_NA_KINDS), H_NA, TM, NA_KEYS), F32),
        compiler_params=_params(("arbitrary", "arbitrary")),
        name="na_bias",
    )(pairs)


_NA_KINDS = ((0, 0), (NA_ROWS_PER_TILE, 0), (GRID_H - NA_ROWS_PER_TILE, GRID_H - NA_KEY_ROWS))
_BIAS_PAD = 4


def _na_bias_kernel(pairs_ref, o_ref):
    low = lax.broadcasted_iota(jnp.int32, (GRID_W, LANE), 1) < GRID_W
    ninf = jnp.full((GRID_W, LANE), -jnp.inf, F32)
    for kind, (r0, k0) in enumerate(_NA_KINDS):
        for i in range(NA_ROWS_PER_TILE):
            r = r0 + i
            rs = min(max(r - WIN_H // 2, 0), GRID_H - WIN_H)
            for m in range(NA_KEY_ROWS // 2):
                kr = k0 + 2 * m
                ok0 = rs <= kr < rs + WIN_H
                ok1 = rs <= kr + 1 < rs + WIN_H
                blk = pairs_ref[0, 0, kr - r + WIN_H - 1 + _BIAS_PAD]
                if ok0 and not ok1:
                    blk = jnp.where(low, blk, ninf)
                elif ok1 and not ok0:
                    blk = jnp.where(low, ninf, blk)
                elif not ok0:
                    blk = ninf
                o_ref[0, kind, 0, i * GRID_W:(i + 1) * GRID_W, m * LANE:(m + 1) * LANE] = blk


def _mla_kernel(q_ref, k_ref, v_ref, kc_ref, vc_ref, o_ref):
    low = _low_lanes()
    outs = []
    for hp in range(H_MLA // 2):
        vsl = slice(hp * LANE, (hp + 1) * LANE)
        vlat = v_ref[:, vsl]
        vctx = vc_ref[0, 0, :, vsl]
        res = []
        for h in (2 * hp, 2 * hp + 1):
            sl = slice(h * HP, (h + 1) * HP)
            q = q_ref[:, sl]
            s_ctx = _dot_nt(q, kc_ref[0, 0, :, sl])
            s_lat = _dot_nt(q, k_ref[:, sl])
            (p_ctx, p_lat), den = _softmax_parts([s_ctx, s_lat], MLA_SCALE)
            o = _dot(p_ctx.astype(BF16), vctx) + _dot(p_lat.astype(BF16), vlat)
            res.append(o / den)
        outs.append(jnp.where(low, res[0], res[1]))
    o_ref[...] = jnp.concatenate(outs, axis=1).astype(BF16)


def _mla_attn(l, qp, kp, vm, kc, vc):
    nt = Q_TILES_PER_SEQ
    seq = lambda n: pl.BlockSpec((S_LAT, n), lambda b, t: (N_CTX // S_LAT + b, 0))
    return pl.pallas_call(
        _mla_kernel,
        grid=(B_LAT, nt),
        in_specs=[pl.BlockSpec((TM, D_QP), lambda b, t: (N_CTX // TM + b * nt + t, 0)),
                  seq(D_QP), seq(D_MLA),
                  pl.BlockSpec((1, 1, PAST, D_QP), lambda b, t: (l, b, 0, 0)),
                  pl.BlockSpec((1, 1, PAST, D_MLA), lambda b, t: (l, b, 0, 0))],
        out_specs=pl.BlockSpec((TM, D_MLA), lambda b, t: (b * nt + t, 0)),
        out_shape=jax.ShapeDtypeStruct((N_LAT, D_MLA), BF16),
        compiler_params=_params(("arbitrary", "arbitrary")),
        name="mla_attn",
    )(qp, kp, vm, kc, vc)


def _route(sel, aff):
    rows = lambda a, g: [a[EPG * g + e:EPG * g + e + 1, :] for e in range(EPG)]
    best = None
    for g in range(N_GRP):
        v = rows(sel, g)
        score = v[0] + v[1]
        for a, b in ((0, 2), (0, 3), (1, 2), (1, 3), (2, 3)):
            score = jnp.maximum(score, v[a] + v[b])
        if best is None:
            best, grp = score, jnp.zeros_like(score)
        else:
            better = score > best
            best = jnp.where(better, score, best)
            grp = jnp.where(better, float(g), grp)
    zero = jnp.zeros_like(best)
    sv = [zero] * EPG
    av = [zero] * EPG
    for g in range(N_GRP):
        hit = grp == float(g)
        sg, ag = rows(sel, g), rows(aff, g)
        sv = [jnp.where(hit, sg[e], sv[e]) for e in range(EPG)]
        av = [jnp.where(hit, ag[e], av[e]) for e in range(EPG)]

    def argmax_first(vals, skip=None):
        bv, bi = None, None
        for e in range(EPG):
            v = vals[e] if skip is None else jnp.where(skip == float(e), -jnp.inf, vals[e])
            if bv is None:
                bv, bi = v, jnp.zeros_like(v)
            else:
                better = v > bv
                bv = jnp.where(better, v, bv)
                bi = jnp.where(better, float(e), bi)
        return bi

    i1 = argmax_first(sv)
    i2 = argmax_first(sv, skip=i1)
    lo = jnp.minimum(i1, i2)
    hi = jnp.maximum(i1, i2)
    pick = lambda idx: sum(jnp.where(idx == float(e), av[e], 0.0) for e in range(EPG))
    a_lo, a_hi = pick(lo), pick(hi)
    den = a_lo + a_hi
    pair = jnp.where(lo == 0.0, 0.0, jnp.where(lo == 1.0, 3.0, 5.0)) + hi - lo - 1.0
    return grp * float(N_PAIR) + pair, a_lo / den, a_hi / den


def _post_kernel(x_ref, octx_ref, ona_ref, omla_ref, mod_ref, n2_ref, wout_ref, wr_ref, br_ref,
                 x1_ref, h2_ref, route_ref, gate_ref, counts_ref, carry):
    i = pl.program_id(0)

    @pl.when(i == 0)
    def _():
        carry[...] = jnp.zeros_like(carry)

    is_ctx = i < CTX_TILES
    o_lat = jnp.concatenate([ona_ref[...], omla_ref[...]], axis=1)
    o = jnp.where(is_ctx, octx_ref[...], o_lat)
    m = mod_ref[0]
    x1 = x_ref[...] + m[2:3, :] * _dot(o, wout_ref[0])
    x1_ref[...] = x1
    h2 = _rms(x1, n2_ref[0]) * (1.0 + m[4:5, :]) + m[3:4, :]
    logits = lax.dot_general(wr_ref[...], h2, (((1,), (1,)), ((), ())),
                             preferred_element_type=F32, precision=lax.Precision.HIGHEST)
    aff = jax.nn.sigmoid(logits)
    bucket, g_lo, g_hi = _route(aff + br_ref[...], aff)

    hit = lax.broadcasted_iota(jnp.int32, (BUCKET_ROWS, TT), 0).astype(F32) == bucket
    earlier = lax.broadcasted_iota(jnp.int32, (TT, TT), 0) <= lax.broadcasted_iota(jnp.int32, (TT, TT), 1)
    incl = _dot(jnp.where(hit, 1.0, 0.0).astype(BF16), jnp.where(earlier, 1.0, 0.0).astype(BF16))
    base = carry[...]
    rank = jnp.sum(jnp.where(hit, incl + base[:, 0:1], 0.0), axis=0, keepdims=True) - 1.0
    carry[...] = base + incl[:, TT - 1:TT]
    counts_ref[...] = carry[...].astype(jnp.int32)
    route = jnp.concatenate([bucket, rank, jnp.zeros((6, TT), F32)], axis=0)
    route_ref[0] = route.astype(jnp.int32)
    gate_ref[0] = jnp.concatenate([g_lo, g_hi, jnp.zeros((6, TT), F32)], axis=0)
    _to_rows(h2_ref, h2, X_CHUNKS)


def _post(l, x, o_ctx, o_na, o_mla, mods, wts):
    tok = lambda n: pl.BlockSpec((TT, n), lambda i: (i, 0))
    lat = lambda n: pl.BlockSpec((TT, n), lambda i: (jnp.maximum(i - CTX_TILES, 0), 0))
    lay = lambda a: pl.BlockSpec((1,) + a.shape[1:], lambda i: (l,) + (0,) * (a.ndim - 1))
    full = lambda a: pl.BlockSpec(a.shape, lambda i: (0,) * a.ndim)
    return pl.pallas_call(
        _post_kernel,
        grid=(N_TILES,),
        in_specs=[tok(D), pl.BlockSpec((TT, D), lambda i: (_ctx_block(i), 0)), lat(D_NA), lat(D_MLA),
                  pl.BlockSpec((1, 6, D), lambda i: (l * N_MOD_ROWS + _mod_row(i), 0, 0)),
                  lay(wts["n2"]), lay(wts["w_out"]), full(wts["w_rt"]), full(wts["b_r"])],
        out_specs=[tok(D), pl.BlockSpec((TT * X_CHUNKS, LANE), lambda i: (i, 0)),
                   pl.BlockSpec((1, 8, TT), lambda i: (i, 0, 0)),
                   pl.BlockSpec((1, 8, TT), lambda i: (i, 0, 0)),
                   pl.BlockSpec((BUCKET_ROWS, LANE), lambda i: (0, 0))],
        out_shape=[jax.ShapeDtypeStruct((N_TOK, D), F32),
                   jax.ShapeDtypeStruct((N_TOK * X_CHUNKS, LANE), F32),
                   jax.ShapeDtypeStruct((N_TILES, 8, TT), jnp.int32),
                   jax.ShapeDtypeStruct((N_TILES, 8, TT), F32),
                   jax.ShapeDtypeStruct((BUCKET_ROWS, LANE), jnp.int32)],
        scratch_shapes=[pltpu.VMEM((BUCKET_ROWS, LANE), F32)],
        compiler_params=_params(("arbitrary",)),
        name="post",
    )(x, o_ctx, o_na, o_mla, mods, wts["n2"], wts["w_out"], wts["w_rt"], wts["b_r"])


def _moe_index_kernel(counts, elo, ehi, valid, total, first_tile):
    def per_bucket(b, cursor):
        cnt = counts[b]
        n_tiles = (cnt + (TM - 1)) // TM
        first_tile[b] = cursor
        grp = b // N_PAIR
        pair = b - grp * N_PAIR
        lo = jnp.where(pair < 3, 0, jnp.where(pair < 5, 1, 2))
        hi = jnp.where(pair < 3, pair + 1, jnp.where(pair < 5, pair - 1, 3))

        def per_tile(k, c):
            elo[cursor + k] = grp * EPG + lo
            ehi[cursor + k] = grp * EPG + hi
            valid[cursor + k] = jnp.minimum(cnt - k * TM, TM)
            return c
        lax.fori_loop(0, n_tiles, per_tile, 0)
        return cursor + n_tiles
    used = lax.fori_loop(0, N_BUCKET, per_bucket, 0)
    total[0] = used

    def unused_tile(k, c):
        elo[k] = elo[used - 1]
        ehi[k] = ehi[used - 1]
        valid[k] = 0
        return c
    lax.fori_loop(used, MOE_TILES, unused_tile, 0)

    def unused_bucket_row(b, c):
        first_tile[b] = used
        return c
    lax.fori_loop(N_BUCKET, BUCKET_ROWS, unused_bucket_row, 0)


def _moe_index(counts):
    smem = pl.BlockSpec(memory_space=pltpu.SMEM)
    tiles = jax.ShapeDtypeStruct((MOE_TILES,), jnp.int32)
    return pl.pallas_call(
        _moe_index_kernel,
        in_specs=[smem],
        out_specs=[smem] * 5,
        out_shape=[tiles, tiles, tiles, jax.ShapeDtypeStruct((1,), jnp.int32),
                   jax.ShapeDtypeStruct((BUCKET_ROWS,), jnp.int32)],
        name="moe_index",
    )(counts)


def _moe_kernel(elo_ref, ehi_ref, valid_ref, total_ref, slot_ref, glo_tok,
                h2_hbm, wg1, wu1, wd1, wg2, wu2, wd2, out_hbm,
                xbuf, ybuf, gvec, src, dst, glo, gsem, ssem, vsem):
    del elo_ref, ehi_ref
    t = pl.program_id(0)
    total = total_ref[0]
    par = t % 2

    def gate_copy(tile, p):
        return pltpu.make_async_copy(glo.at[pl.ds(pl.multiple_of(tile * TM, TM), TM)], gvec.at[p], vsem.at[p])

    def gather_row(tile, p, j):
        row = pl.multiple_of(src[tile * TM + j] * X_CHUNKS, X_CHUNKS)
        pltpu.make_async_copy(h2_hbm.at[pl.ds(row, X_CHUNKS), :],
                              xbuf.at[p, pl.ds(j * X_CHUNKS, X_CHUNKS), :], gsem.at[p]).start()

    def scatter_row(tile, p, j):
        row = pl.multiple_of(dst[(tile + 1) * TM + j] * Y_CHUNKS, Y_CHUNKS)
        pltpu.make_async_copy(ybuf.at[p, pl.ds(j * Y_CHUNKS, Y_CHUNKS), :],
                              out_hbm.at[pl.ds(row, Y_CHUNKS), :], ssem.at[p]).start()

    def looped(row_fn, tile, p):
        def body(j, c):
            row_fn(tile, p, j)
            return c
        lax.fori_loop(0, TM, body, 0, unroll=DMA_UNROLL)

    def wait_gather(p):
        pltpu.make_async_copy(h2_hbm.at[pl.ds(0, TM * X_CHUNKS), :], xbuf.at[p], gsem.at[p]).wait()
        gate_copy(0, p).wait()

    def wait_scatter(p):
        pltpu.make_async_copy(ybuf.at[p], out_hbm.at[pl.ds(0, TM * Y_CHUNKS), :], ssem.at[p]).wait()

    @pl.when(t == 0)
    def _():
        def place(i, c):
            s = slot_ref[i]
            src[s] = i
            dst[s + TM] = i
            glo[s] = glo_tok[i]
            return c
        lax.fori_loop(0, N_TOK, place, 0, unroll=DMA_UNROLL)

        def spare_slot(j, c):
            dst[j] = N_TOK + j
            return c
        lax.fori_loop(0, TM, spare_slot, 0)

        def pad_tile(tile, c):
            def pad_slot(j, c2):
                src[tile * TM + j] = src[tile * TM]
                dst[(tile + 1) * TM + j] = N_TOK + j
                glo[tile * TM + j] = 0.0
                return c2
            return lax.fori_loop(valid_ref[tile], TM, pad_slot, c)
        lax.fori_loop(0, total, pad_tile, 0)
        ybuf[1] = jnp.zeros((TM * Y_CHUNKS, LANE), F32)
        gate_copy(0, 0).start()
        looped(gather_row, 0, 0)

    @pl.when(t < total)
    def _():
        wait_gather(par)

        @pl.when(t >= 1)
        def _():
            wait_scatter(par)

        x = _from_rows(xbuf, TM, X_CHUNKS, 0, X_CHUNKS, lead=(par,)).astype(BF16)
        g_row = gvec[pl.ds(par, 1), :]
        g_lo = jnp.transpose(jnp.concatenate([g_row, jnp.zeros((7, TM), F32)], axis=0))[:, 0:1]

        nxt = jnp.minimum(t + 1, total - 1)
        gate_copy(nxt, 1 - par).start()
        for j in range(TM):
            gather_row(nxt, 1 - par, j)
            scatter_row(t - 1, 1 - par, j)

        y = None
        for gate, (wg, wu, wd) in ((g_lo, (wg1, wu1, wd1)), (1.0 - g_lo, (wg2, wu2, wd2))):
            a = _dot(x, wg[0, 0])
            hid = a * jax.nn.sigmoid(a) * _dot(x, wu[0, 0]) * gate
            part = _dot(hid.astype(BF16), wd[0, 0])
            y = part if y is None else y + part
        _to_rows(ybuf, y, Y_CHUNKS, lead=(par,))

        @pl.when(t == total - 1)
        def _():
            looped(scatter_row, t, par)
            wait_gather(1 - par)
            wait_scatter(1 - par)
            wait_scatter(par)


def _moe(l, h2, route, gate, counts, wts):
    bucket = route[:, 0, :].reshape(N_TOK)
    rank = route[:, 1, :].reshape(N_TOK)
    g_lo = gate[:, 0, :].reshape(N_TOK)
    e_lo, e_hi, valid, total, first_tile = _moe_index(counts[:, 0])
    buckets = jnp.arange(N_BUCKET, dtype=jnp.int32)
    slot = jnp.where(bucket[:, None] == buckets[None, :], (first_tile[:N_BUCKET] * TM)[None, :], 0).sum(axis=1) + rank
    up = lambda which: pl.BlockSpec((1, 1, D, D_EXP), lambda t, *pf: (l, pf[which][t], 0, 0))
    down = lambda which: pl.BlockSpec((1, 1, D_EXP, D), lambda t, *pf: (l, pf[which][t], 0, 0))
    first, second = 0, 1
    slots = MOE_TILES * TM
    return pl.pallas_call(
        _moe_kernel,
        grid_spec=pltpu.PrefetchScalarGridSpec(
            num_scalar_prefetch=6,
            grid=(MOE_TILES,),
            in_specs=[pl.BlockSpec(memory_space=pl.ANY),
                      up(first), up(first), down(first), up(second), up(second), down(second)],
            out_specs=pl.BlockSpec(memory_space=pl.ANY),
            scratch_shapes=[pltpu.VMEM((2, TM * X_CHUNKS, LANE), F32), pltpu.VMEM((2, TM * Y_CHUNKS, LANE), F32),
                            pltpu.VMEM((2, TM), F32),
                            pltpu.SMEM((slots,), jnp.int32), pltpu.SMEM((slots + TM,), jnp.int32),
                            pltpu.SMEM((slots,), F32),
                            pltpu.SemaphoreType.DMA((2,)), pltpu.SemaphoreType.DMA((2,)),
                            pltpu.SemaphoreType.DMA((2,))],
        ),
        out_shape=jax.ShapeDtypeStruct(((N_TOK + TM) * Y_CHUNKS, LANE), F32),
        compiler_params=_params(("arbitrary",)),
        name="moe",
    )(e_lo, e_hi, valid, total, slot, g_lo, h2,
      wts["w_gate"], wts["w_up"], wts["w_down"], wts["w_gate"], wts["w_up"], wts["w_down"])


def _final_kernel(x1_ref, moe_ref, mod_ref, g_ref, yc_ref, yl_ref):
    i = pl.program_id(0)
    moe = _from_rows(moe_ref, TT, Y_CHUNKS, 0, Y_CHUNKS)
    y = _rms(x1_ref[...] + mod_ref[0, 5:6, :] * moe, g_ref[...])

    @pl.when(i < CTX_TILES)
    def _():
        yc_ref[...] = y

    @pl.when(i >= CTX_TILES)
    def _():
        yl_ref[...] = y


def _final(x1, moe, mods, g):
    tok = pl.BlockSpec((TT, D), lambda i: (i, 0))
    l = DEPTH - 1
    return pl.pallas_call(
        _final_kernel,
        grid=(N_TILES,),
        in_specs=[tok, pl.BlockSpec((TT * Y_CHUNKS, LANE), lambda i: (i, 0)),
                  pl.BlockSpec((1, 6, D), lambda i: (l * N_MOD_ROWS + _mod_row(i), 0, 0)),
                  pl.BlockSpec((1, D), lambda i: (0, 0))],
        out_specs=[pl.BlockSpec((TT, D), lambda i: (_ctx_block(i), 0)),
                   pl.BlockSpec((TT, D), lambda i: (jnp.maximum(i - CTX_TILES, 0), 0))],
        out_shape=[jax.ShapeDtypeStruct((N_CTX, D), F32), jax.ShapeDtypeStruct((N_LAT, D), F32)],
        compiler_params=_params(("arbitrary",)),
        name="final",
    )(x1, moe, mods, g)


def _swap_halves(w):
    q = ROPE // 4
    return jnp.concatenate([-w[..., q:2 * q], w[..., :q], -w[..., 3 * q:], w[..., 2 * q:3 * q]], axis=-1)


def _pad_head(nope, rope):
    z = jnp.zeros(rope.shape[:-1] + (HP - NOPE - ROPE,), rope.dtype)
    return jnp.concatenate([nope, rope, z], axis=-1)


def _prepare_weights(norm1_g, norm2_g, w_in, q_norm_g, w_uq, kv_norm_g, w_ukv, w_out, w_router, b_router,
                     w_gate, w_up, w_down):
    o = 3 * D_NA + Q_LORA + KV_LORA
    w_kr = w_in[:, :, o:]
    zeros_nope = jnp.zeros((DEPTH, D, NOPE), F32)
    w_in_p = jnp.concatenate([w_in[:, :, :o], _pad_head(zeros_nope, w_kr),
                              _pad_head(zeros_nope, _swap_halves(w_kr))], axis=-1)
    wq = w_uq.reshape(DEPTH, Q_LORA, H_MLA, NOPE + ROPE)
    plain = _pad_head(wq[..., :NOPE], wq[..., NOPE:]).reshape(DEPTH, Q_LORA, D_QP)
    swapped = _pad_head(jnp.zeros_like(wq[..., :NOPE]), _swap_halves(wq[..., NOPE:])).reshape(DEPTH, Q_LORA, D_QP)
    wkv = w_ukv.reshape(DEPTH, KV_LORA, H_MLA, NOPE + VH)
    w_uk_p = jnp.concatenate([wkv[..., :NOPE], jnp.zeros((DEPTH, KV_LORA, H_MLA, HP - NOPE), F32)],
                             axis=-1).reshape(DEPTH, KV_LORA, D_QP)
    w_uv = wkv[..., NOPE:].reshape(DEPTH, KV_LORA, D_MLA)
    return {
        "n1": norm1_g.reshape(DEPTH, 1, D),
        "n2": norm2_g.reshape(DEPTH, 1, D),
        "w_in": w_in_p.astype(BF16),
        "qn": q_norm_g.reshape(DEPTH, 1, Q_LORA),
        "w_uq": jnp.concatenate([plain, swapped], axis=-1).astype(BF16),
        "kvn": kv_norm_g.reshape(DEPTH, 1, KV_LORA),
        "w_uk": w_uk_p.astype(BF16),
        "w_uv": w_uv.astype(BF16),
        "w_out": w_out.astype(BF16),
        "w_rt": w_router.T,
        "b_r": b_router.reshape(N_EXP, 1),
        "w_gate": w_gate.astype(BF16),
        "w_up": w_up.astype(BF16),
        "w_down": w_down.astype(BF16),
    }


def _rope_tables():
    half = ROPE // 2
    freqs = 1.0 / (ROPE_THETA ** (np.arange(0, half, 2, dtype=np.float32) / half))
    pos = np.arange(S_LAT)
    ang_r = (pos // GRID_W).astype(np.float32)[:, None] * freqs
    ang_c = (pos % GRID_W).astype(np.float32)[:, None] * freqs
    ang = np.concatenate([ang_r, ang_r, ang_c, ang_c], axis=1)
    cos = np.ones((TT + S_LAT, LANE), np.float32)
    sin = np.zeros((TT + S_LAT, LANE), np.float32)
    cos[TT:, NOPE:NOPE + ROPE] = np.cos(ang)
    sin[TT:, NOPE:NOPE + ROPE] = np.sin(ang)
    return jnp.asarray(cos), jnp.asarray(sin)


def kernel(x_prompt, x_sample, cache_nat_k, cache_nat_v, cache_mla_ckv, cache_mla_krope, c, c_ctx, norm1_g, norm2_g, w_ada, b_ada, w_in, q_norm_g, w_uq, kv_norm_g, w_ukv, rpb, w_out, w_router, b_router, w_gate, w_up, w_down, final_norm_g):
    wts = _prepare_weights(norm1_g, norm2_g, w_in, q_norm_g, w_uq, kv_norm_g, w_ukv, w_out, w_router, b_router,
                           w_gate, w_up, w_down)
    tabs = _rope_tables()
    cond = jnp.concatenate([c_ctx[None, :], c, jnp.zeros((N_MOD_ROWS - 1 - B_LAT, D), F32)], axis=0)
    mods = _ada(cond, w_ada, b_ada).reshape(DEPTH * N_MOD_ROWS, 6, D)
    place = jnp.asarray(np.eye(ROPE, HP, k=NOPE, dtype=np.float32)).astype(BF16)
    kc_all, vc_all = _ctxkv(cache_mla_ckv, cache_mla_krope, wts["w_uk"], wts["w_uv"], place)
    cache_k = cache_nat_k.reshape(B_LAT, DEPTH, PAST, D_NA)
    cache_v = cache_nat_v.reshape(B_LAT, DEPTH, PAST, D_NA)
    bias = _na_bias_tables(rpb)

    x1 = moe = None
    new_k, new_v, new_ckv, new_kr = [], [], [], []
    for l in range(DEPTH):
        xin = (x_prompt.reshape(N_CTX, D), x_sample.reshape(N_LAT, D)) if l == 0 else (x1, moe)
        x, qna, kna, vna, qp, kp, vm, k_new, v_new, ckv_new, kr_new = _pre(l, l == 0, xin, mods, wts, tabs)
        new_k.append(k_new)
        new_v.append(v_new)
        new_ckv.append(ckv_new)
        new_kr.append(kr_new)
        o_ctx = _ctx_attn(qna, kna, vna, qp, kp, vm)
        o_na = _na_attn(l, qna, kna, vna, cache_k, cache_v, bias)
        o_mla = _mla_attn(l, qp, kp, vm, kc_all, vc_all)
        x1, h2, route, gate, counts = _post(l, x, o_ctx, o_na, o_mla, mods, wts)
        moe = _moe(l, h2, route, gate, counts, wts)
    y_ctx, y_lat = _final(x1, moe, mods, final_norm_g.reshape(1, D))

    stack = lambda parts, tail: jnp.stack([p.reshape((B_CTX, S_CTX) + tail) for p in parts], axis=1)
    return (y_ctx.reshape(B_CTX, S_CTX, D), y_lat.reshape(B_LAT, S_LAT, D),
            stack(new_k, (H_NA, DH_NA)), stack(new_v, (H_NA, DH_NA)),
            stack(new_ckv, (KV_LORA,)), stack(new_kr, (ROPE,)))
```

```python
import functools

import numpy as np
import jax
import jax.numpy as jnp
from jax import lax
from jax.experimental import pallas as pl
from jax.experimental.pallas import tpu as pltpu

F32 = jnp.float32
BF16 = jnp.bfloat16

D = 1024
B_CTX, S_CTX = 32, 256
B_LAT, S_LAT = 8, 2048
DEPTH = 4
PAST = 256
GRID_W, GRID_H = 64, 32
H_NA, DH_NA = 8, 64
WIN_H, WIN_W = 8, 16
H_MLA, NOPE, ROPE, VH = 8, 64, 32, 64
Q_LORA, KV_LORA = 256, 128
D_NA = H_NA * DH_NA
D_MLA = H_MLA * VH
N_EXP, N_GRP, EPG = 16, 4, 4
D_EXP = 512
ROPE_THETA = 10000.0
EPS = 1e-6
NA_SCALE = DH_NA ** -0.5
MLA_SCALE = (NOPE + ROPE) ** -0.5
LOG2E = 1.4426950408889634

N_CTX = B_CTX * S_CTX
N_LAT = B_LAT * S_LAT
N_TOK = N_CTX + N_LAT
TM = 256
TT = 512
CTX_TILES = N_CTX // TT
LAT_TILES = N_LAT // TT
N_TILES = CTX_TILES + LAT_TILES
LAT_TILES_PER_SEQ = S_LAT // TT
Q_TILES_PER_SEQ = S_LAT // TM
TQ_MLA = 512
LANE = 128
HP = LANE
D_QP = H_MLA * HP
W_IN_COLS = 3 * D_NA + Q_LORA + KV_LORA + 2 * LANE
N_MOD_ROWS = 16
N_PAIR = 6
N_BUCKET = N_GRP * N_PAIR
MOE_TILES = N_TOK // TM + N_BUCKET
BUCKET_ROWS = 32
Y_CHUNKS = D // LANE
X_CHUNKS = D // LANE


def _to_rows(ref, value, chunks, lead=()):
    t = value.shape[0]
    for c in range(value.shape[1] // LANE):
        ref[lead + (pl.ds(c, t, stride=chunks), slice(None))] = value[:, c * LANE:(c + 1) * LANE]


def _from_rows(ref, t, chunks, first, n, lead=()):
    parts = [ref[lead + (pl.ds(first + c, t, stride=chunks), slice(None))] for c in range(n)]
    return parts[0] if n == 1 else jnp.concatenate(parts, axis=1)
NA_ROWS_PER_TILE = TM // GRID_W
NA_KEY_ROWS = 12
NA_KEYS = NA_KEY_ROWS * GRID_W
VMEM_LIMIT = 48 * 1024 * 1024
DMA_UNROLL = 8


def _dot(a, b):
    return jnp.dot(a, b, preferred_element_type=F32)


def _dot_nt(a, b):
    return lax.dot_general(a, b, (((1,), (1,)), ((), ())), preferred_element_type=F32)


def _rms(x, g):
    return x * lax.rsqrt(jnp.mean(x * x, axis=-1, keepdims=True) + EPS) * g


def _mod_row(i):
    return jnp.where(i < CTX_TILES, 0, 1 + (i - CTX_TILES) // LAT_TILES_PER_SEQ)


def _pos_block(i):
    return jnp.where(i < CTX_TILES, 0, 1 + (i - CTX_TILES) % LAT_TILES_PER_SEQ)


def _ctx_block(i):
    return jnp.minimum(i, CTX_TILES - 1)


def _params(sem):
    return pltpu.CompilerParams(dimension_semantics=sem, vmem_limit_bytes=VMEM_LIMIT)


def _ada_kernel(cond_ref, w_ref, b_ref, o_ref):
    cond = cond_ref[...]
    act = cond * jax.nn.sigmoid(cond)
    o_ref[0] = jnp.dot(act, w_ref[0], preferred_element_type=F32,
                       precision=lax.Precision.HIGHEST) + b_ref[0]


def _ada(cond, w_ada, b_ada):
    nt = 1536
    return pl.pallas_call(
        _ada_kernel,
        grid=(DEPTH, 6 * D // nt),
        in_specs=[pl.BlockSpec((N_MOD_ROWS, D), lambda l, j: (0, 0)),
                  pl.BlockSpec((1, D, nt), lambda l, j: (l, 0, j)),
                  pl.BlockSpec((1, 1, nt), lambda l, j: (l, 0, j))],
        out_specs=pl.BlockSpec((1, N_MOD_ROWS, nt), lambda l, j: (l, 0, j)),
        out_shape=jax.ShapeDtypeStruct((DEPTH, N_MOD_ROWS, 6 * D), F32),
        compiler_params=_params(("arbitrary", "arbitrary")),
        name="ada",
    )(cond, w_ada, b_ada.reshape(DEPTH, 1, 6 * D))


def _ctxkv_kernel(ckv_ref, kr_ref, wuk_ref, wuv_ref, place_ref, kc_ref, vc_ref):
    ckv = ckv_ref[0, 0].astype(BF16)
    kr = _dot(kr_ref[0, 0].astype(BF16), place_ref[...])
    kc_ref[0, 0] = (_dot(ckv, wuk_ref[0]) + jnp.tile(kr, (1, H_MLA))).astype(BF16)
    vc_ref[0, 0] = _dot(ckv, wuv_ref[0]).astype(BF16)


def _ctxkv(cache_ckv, cache_krope, w_uk_p, w_uv, place):
    return pl.pallas_call(
        _ctxkv_kernel,
        grid=(DEPTH, B_LAT),
        in_specs=[pl.BlockSpec((1, 1, PAST, KV_LORA), lambda l, b: (b, l, 0, 0)),
                  pl.BlockSpec((1, 1, PAST, ROPE), lambda l, b: (b, l, 0, 0)),
                  pl.BlockSpec((1, KV_LORA, D_QP), lambda l, b: (l, 0, 0)),
                  pl.BlockSpec((1, KV_LORA, D_MLA), lambda l, b: (l, 0, 0)),
                  pl.BlockSpec((ROPE, HP), lambda l, b: (0, 0))],
        out_specs=[pl.BlockSpec((1, 1, PAST, D_QP), lambda l, b: (l, b, 0, 0)),
                   pl.BlockSpec((1, 1, PAST, D_MLA), lambda l, b: (l, b, 0, 0))],
        out_shape=[jax.ShapeDtypeStruct((DEPTH, B_LAT, PAST, D_QP), BF16),
                   jax.ShapeDtypeStruct((DEPTH, B_LAT, PAST, D_MLA), BF16)],
        compiler_params=_params(("arbitrary", "arbitrary")),
        name="ctxkv",
    )(cache_ckv, cache_krope, w_uk_p, w_uv, place)


def _pre_kernel(first, *refs):
    if first:
        xc_ref, xl_ref, *rest = refs
    else:
        x1_ref, moe_ref, modp_ref, *rest = refs
    (mod_ref, n1_ref, win_ref, qn_ref, wuq_ref, kvn_ref, wuk_ref, wuv_ref, cos_ref, sin_ref,
     xo_ref, qna_ref, kna_ref, vna_ref, qp_ref, kp_ref, vm_ref, knew_ref, vnew_ref, ckvnew_ref, krnew_ref) = rest
    i = pl.program_id(0)

    if first:
        x = jnp.where(i < CTX_TILES, xc_ref[...], xl_ref[...])
    else:
        x = x1_ref[...] + modp_ref[0, 5:6, :] * _from_rows(moe_ref, TT, Y_CHUNKS, 0, Y_CHUNKS)
    xo_ref[...] = x
    m = mod_ref[0]
    h = _rms(x, n1_ref[0]) * (1.0 + m[1:2, :]) + m[0:1, :]
    proj = _dot(h.astype(BF16), win_ref[0])
    k_na = proj[:, D_NA:2 * D_NA]
    v_na = proj[:, 2 * D_NA:3 * D_NA]
    qna_ref[...] = (proj[:, :D_NA] * NA_SCALE).astype(BF16)
    kna_ref[...] = k_na.astype(BF16)
    vna_ref[...] = v_na.astype(BF16)
    o = 3 * D_NA
    cq = proj[:, o:o + Q_LORA]
    ckv = proj[:, o + Q_LORA:o + Q_LORA + KV_LORA]
    kr = proj[:, o + Q_LORA + KV_LORA:o + Q_LORA + KV_LORA + LANE]
    kr_sw = proj[:, o + Q_LORA + KV_LORA + LANE:]
    cos = cos_ref[...]
    sin = sin_ref[...]

    cqn = _rms(cq, qn_ref[0]).astype(BF16)
    q2 = _dot(cqn, wuq_ref[0])
    qp = q2[:, :D_QP] * jnp.tile(cos, (1, H_MLA)) + q2[:, D_QP:] * jnp.tile(sin, (1, H_MLA))
    qp_ref[...] = qp.astype(BF16)

    ckvn = _rms(ckv, kvn_ref[0])
    ckvb = ckvn.astype(BF16)
    kr_rot = kr * cos + kr_sw * sin
    kp_ref[...] = (_dot(ckvb, wuk_ref[0]) + jnp.tile(kr_rot, (1, H_MLA))).astype(BF16)
    vm_ref[...] = _dot(ckvb, wuv_ref[0]).astype(BF16)

    @pl.when(i < CTX_TILES)
    def _():
        knew_ref[...] = k_na
        vnew_ref[...] = v_na
        ckvnew_ref[...] = ckvn
        krnew_ref[...] = kr[:, NOPE:NOPE + ROPE]


def _pre(l, first, xin, mods, wts, tabs):
    tok = lambda n: pl.BlockSpec((TT, n), lambda i: (i, 0))
    ctx = lambda n: pl.BlockSpec((TT, n), lambda i: (_ctx_block(i), 0))
    lat = lambda n: pl.BlockSpec((TT, n), lambda i: (jnp.maximum(i - CTX_TILES, 0), 0))
    lay = lambda a: pl.BlockSpec((1,) + a.shape[1:], lambda i: (l,) + (0,) * (a.ndim - 1))
    in_specs = []
    args = []
    if first:
        in_specs += [ctx(D), lat(D)]
        args += list(xin)
    else:
        x1, moe = xin
        in_specs += [tok(D), pl.BlockSpec((TT * Y_CHUNKS, LANE), lambda i: (i, 0)),
                     pl.BlockSpec((1, 6, D), lambda i: ((l - 1) * N_MOD_ROWS + _mod_row(i), 0, 0))]
        args += [x1, moe, mods]
    in_specs.append(pl.BlockSpec((1, 6, D), lambda i: (l * N_MOD_ROWS + _mod_row(i), 0, 0)))
    args.append(mods)
    for name in ("n1", "w_in", "qn", "w_uq", "kvn", "w_uk", "w_uv"):
        in_specs.append(lay(wts[name]))
        args.append(wts[name])
    for t in tabs:
        in_specs.append(pl.BlockSpec((TT, LANE), lambda i: (_pos_block(i), 0)))
        args.append(t)
    out_specs = [tok(D), tok(D_NA), tok(D_NA), tok(D_NA), tok(D_QP), tok(D_QP), tok(D_MLA),
                 ctx(D_NA), ctx(D_NA), ctx(KV_LORA), ctx(ROPE)]
    out_shape = [jax.ShapeDtypeStruct((N_TOK, D), F32),
                 jax.ShapeDtypeStruct((N_TOK, D_NA), BF16),
                 jax.ShapeDtypeStruct((N_TOK, D_NA), BF16),
                 jax.ShapeDtypeStruct((N_TOK, D_NA), BF16),
                 jax.ShapeDtypeStruct((N_TOK, D_QP), BF16),
                 jax.ShapeDtypeStruct((N_TOK, D_QP), BF16),
                 jax.ShapeDtypeStruct((N_TOK, D_MLA), BF16),
                 jax.ShapeDtypeStruct((N_CTX, D_NA), F32),
                 jax.ShapeDtypeStruct((N_CTX, D_NA), F32),
                 jax.ShapeDtypeStruct((N_CTX, KV_LORA), F32),
                 jax.ShapeDtypeStruct((N_CTX, ROPE), F32)]
    return pl.pallas_call(
        functools.partial(_pre_kernel, first),
        grid=(N_TILES,),
        in_specs=in_specs,
        out_specs=out_specs,
        out_shape=out_shape,
        compiler_params=_params(("arbitrary",)),
        name="pre",
    )(*args)


def _softmax_parts(parts, scale=None):
    m = parts[0].max(axis=-1, keepdims=True)
    for s in parts[1:]:
        m = jnp.maximum(m, s.max(axis=-1, keepdims=True))
    if scale is None:
        ps = [jnp.exp(s - m) for s in parts]
    else:
        ps = [jnp.exp2((s - m) * (scale * LOG2E)) for s in parts]
    den = ps[0].sum(axis=-1, keepdims=True)
    for p in ps[1:]:
        den = den + p.sum(axis=-1, keepdims=True)
    return ps, den


def _low_lanes(rows=TM):
    return lax.broadcasted_iota(jnp.int32, (rows, LANE), 1) < DH_NA


def _split_pair(qpair, low):
    zero = jnp.zeros_like(qpair)
    return jnp.where(low, qpair, zero), jnp.where(low, zero, qpair)


def _ctx_attn_kernel(qna_ref, kna_ref, vna_ref, qp_ref, kp_ref, vm_ref, o_ref):
    low = _low_lanes()
    outs = []
    for hp in range(H_NA // 2):
        sl = slice(hp * LANE, (hp + 1) * LANE)
        kpair = kna_ref[:, sl]
        vpair = vna_ref[:, sl]
        res = []
        for qh in _split_pair(qna_ref[:, sl], low):
            (p,), den = _softmax_parts([_dot_nt(qh, kpair)])
            res.append(_dot(p.astype(BF16), vpair) / den)
        outs.append(jnp.where(low, res[0], res[1]))
    for hp in range(H_MLA // 2):
        vpair = vm_ref[:, hp * LANE:(hp + 1) * LANE]
        res = []
        for h in (2 * hp, 2 * hp + 1):
            sl = slice(h * HP, (h + 1) * HP)
            (p,), den = _softmax_parts([_dot_nt(qp_ref[:, sl], kp_ref[:, sl])], MLA_SCALE)
            res.append(_dot(p.astype(BF16), vpair) / den)
        outs.append(jnp.where(low, res[0], res[1]))
    o_ref[...] = jnp.concatenate(outs, axis=1).astype(BF16)


def _ctx_attn(qna, kna, vna, qp, kp, vm):
    tok = lambda n: pl.BlockSpec((S_CTX, n), lambda i: (i, 0))
    return pl.pallas_call(
        _ctx_attn_kernel,
        grid=(B_CTX,),
        in_specs=[tok(D_NA), tok(D_NA), tok(D_NA), tok(D_QP), tok(D_QP), tok(D_MLA)],
        out_specs=tok(D_NA + D_MLA),
        out_shape=jax.ShapeDtypeStruct((N_CTX, D_NA + D_MLA), BF16),
        compiler_params=_params(("arbitrary",)),
        name="ctx_attn",
    )(qna, kna, vna, qp, kp, vm)


def _na_kernel(q_ref, k_ref, v_ref, kc_ref, vc_ref, bias_ref, o_ref):
    t = pl.program_id(1)
    row0 = jnp.clip(t * NA_ROWS_PER_TILE - WIN_H // 2, 0, GRID_H - NA_KEY_ROWS)
    ks = pl.multiple_of(row0 * GRID_W, GRID_W)
    low = _low_lanes()
    outs = []
    for hp in range(H_NA // 2):
        sl = slice(hp * LANE, (hp + 1) * LANE)
        kwin = k_ref[pl.ds(ks, NA_KEYS), sl]
        vwin = v_ref[pl.ds(ks, NA_KEYS), sl]
        kctx = kc_ref[0, 0, :, sl].astype(BF16)
        vctx = vc_ref[0, 0, :, sl].astype(BF16)
        res = []
        for half, qh in enumerate(_split_pair(q_ref[:, sl], low)):
            s_win = _dot_nt(qh, kwin) + bias_ref[0, 0, 2 * hp + half]
            s_ctx = _dot_nt(qh, kctx)
            (p_win, p_ctx), den = _softmax_parts([s_win, s_ctx])
            o = _dot(p_win.astype(BF16), vwin) + _dot(p_ctx.astype(BF16), vctx)
            res.append(o / den)
        outs.append(jnp.where(low, res[0], res[1]))
    o_ref[...] = jnp.concatenate(outs, axis=1).astype(BF16)


def _na_attn(l, qna, kna, vna, cache_k, cache_v, bias):
    nt = Q_TILES_PER_SEQ
    seq = lambda n: pl.BlockSpec((S_LAT, n), lambda b, t: (N_CTX // S_LAT + b, 0))
    cache = pl.BlockSpec((1, 1, PAST, D_NA), lambda b, t: (b, l, 0, 0))
    kind = lambda t: jnp.where(t == 0, 0, jnp.where(t == nt - 1, 2, 1))
    return pl.pallas_call(
        _na_kernel,
        grid=(B_LAT, nt),
        in_specs=[pl.BlockSpec((TM, D_NA), lambda b, t: (N_CTX // TM + b * nt + t, 0)),
                  seq(D_NA), seq(D_NA), cache, cache,
                  pl.BlockSpec((1, 1, H_NA, TM, NA_KEYS), lambda b, t: (l, kind(t), 0, 0, 0))],
        out_specs=pl.BlockSpec((TM, D_NA), lambda b, t: (b * nt + t, 0)),
        out_shape=jax.ShapeDtypeStruct((N_LAT, D_NA), BF16),
        compiler_params=_params(("arbitrary", "arbitrary")),
        name="na_attn",
    )(qna, kna, vna, cache_k, cache_v, bias)


def _na_bias_tables(rpb):
    n_dr, n_dc = 2 * WIN_H - 1, 2 * WIN_W - 1
    qc = np.arange(GRID_W)[:, None]
    kc = np.arange(GRID_W)[None, :]
    ws = np.clip(qc - WIN_W // 2, 0, GRID_W - WIN_W)
    col_ok = (kc >= ws) & (kc < ws + WIN_W)
    col_hot = ((kc - qc + WIN_W - 1)[..., None] == np.arange(n_dc)) & col_ok[..., None]
    col_hot = col_hot.reshape(GRID_W * GRID_W, n_dc).astype(np.float32)
    by_col = jnp.einsum("pb,lhab->lhap", jnp.asarray(col_hot), rpb, precision=lax.Precision.HIGHEST)
    by_col = jnp.where(jnp.asarray(col_ok), by_col.reshape(DEPTH, H_NA, n_dr, GRID_W, GRID_W), -jnp.inf)
    pad = jnp.full((DEPTH, H_NA, _BIAS_PAD, GRID_W, GRID_W), -jnp.inf, F32)
    blocks = jnp.concatenate([pad, by_col, pad], axis=2)
    pairs = jnp.concatenate([blocks[:, :, :-1], blocks[:, :, 1:]], axis=-1)
    n_pairs = n_dr + 2 * _BIAS_PAD - 1
    return pl.pallas_call(
        _na_bias_kernel,
        grid=(DEPTH, H_NA),
        in_specs=[pl.BlockSpec((1, 1, n_pairs, GRID_W, LANE), lambda l, h: (l, h, 0, 0, 0))],
        out_specs=pl.BlockSpec((1, len(_NA_KINDS), 1, TM, NA_KEYS), lambda l, h: (l, 0, h, 0, 0)),
        out_shape=jax.ShapeDtypeStruct((DEPTH, len(_NA_KINDS), H_NA, TM, NA_KEYS), F32),
        compiler_params=_params(("arbitrary", "arbitrary")),
        name="na_bias",
    )(pairs)


_NA_KINDS = ((0, 0), (NA_ROWS_PER_TILE, 0), (GRID_H - NA_ROWS_PER_TILE, GRID_H - NA_KEY_ROWS))
_BIAS_PAD = 4


def _na_bias_kernel(pairs_ref, o_ref):
    low = lax.broadcasted_iota(jnp.int32, (GRID_W, LANE), 1) < GRID_W
    ninf = jnp.full((GRID_W, LANE), -jnp.inf, F32)
    for kind, (r0, k0) in enumerate(_NA_KINDS):
        for i in range(NA_ROWS_PER_TILE):
            r = r0 + i
            rs = min(max(r - WIN_H // 2, 0), GRID_H - WIN_H)
            for m in range(NA_KEY_ROWS // 2):
                kr = k0 + 2 * m
                ok0 = rs <= kr < rs + WIN_H
                ok1 = rs <= kr + 1 < rs + WIN_H
                blk = pairs_ref[0, 0, kr - r + WIN_H - 1 + _BIAS_PAD]
                if ok0 and not ok1:
                    blk = jnp.where(low, blk, ninf)
                elif ok1 and not ok0:
                    blk = jnp.where(low, ninf, blk)
                elif not ok0:
                    blk = ninf
                o_ref[0, kind, 0, i * GRID_W:(i + 1) * GRID_W, m * LANE:(m + 1) * LANE] = blk


def _mla_kernel(q_ref, k_ref, v_ref, kc_ref, vc_ref, o_ref):
    low = _low_lanes(TQ_MLA)
    outs = []
    for hp in range(H_MLA // 2):
        vsl = slice(hp * LANE, (hp + 1) * LANE)
        vlat = v_ref[:, vsl]
        vctx = vc_ref[0, 0, :, vsl]
        res = []
        for h in (2 * hp, 2 * hp + 1):
            sl = slice(h * HP, (h + 1) * HP)
            q = q_ref[:, sl]
            s_ctx = _dot_nt(q, kc_ref[0, 0, :, sl])
            s_lat = _dot_nt(q, k_ref[:, sl])
            (p_ctx, p_lat), den = _softmax_parts([s_ctx, s_lat], MLA_SCALE)
            o = _dot(p_ctx.astype(BF16), vctx) + _dot(p_lat.astype(BF16), vlat)
            res.append(o / den)
        outs.append(jnp.where(low, res[0], res[1]))
    o_ref[...] = jnp.concatenate(outs, axis=1).astype(BF16)


def _mla_attn(l, qp, kp, vm, kc, vc):
    nt = S_LAT // TQ_MLA
    seq = lambda n: pl.BlockSpec((S_LAT, n), lambda b, t: (N_CTX // S_LAT + b, 0))
    return pl.pallas_call(
        _mla_kernel,
        grid=(B_LAT, nt),
        in_specs=[pl.BlockSpec((TQ_MLA, D_QP), lambda b, t: (N_CTX // TQ_MLA + b * nt + t, 0)),
                  seq(D_QP), seq(D_MLA),
                  pl.BlockSpec((1, 1, PAST, D_QP), lambda b, t: (l, b, 0, 0)),
                  pl.BlockSpec((1, 1, PAST, D_MLA), lambda b, t: (l, b, 0, 0))],
        out_specs=pl.BlockSpec((TQ_MLA, D_MLA), lambda b, t: (b * nt + t, 0)),
        out_shape=jax.ShapeDtypeStruct((N_LAT, D_MLA), BF16),
        compiler_params=_params(("arbitrary", "arbitrary")),
        name="mla_attn",
    )(qp, kp, vm, kc, vc)


def _route(sel, aff):
    rows = lambda a, g: [a[EPG * g + e:EPG * g + e + 1, :] for e in range(EPG)]
    best = None
    for g in range(N_GRP):
        v = rows(sel, g)
        score = v[0] + v[1]
        for a, b in ((0, 2), (0, 3), (1, 2), (1, 3), (2, 3)):
            score = jnp.maximum(score, v[a] + v[b])
        if best is None:
            best, grp = score, jnp.zeros_like(score)
        else:
            better = score > best
            best = jnp.where(better, score, best)
            grp = jnp.where(better, float(g), grp)
    zero = jnp.zeros_like(best)
    sv = [zero] * EPG
    av = [zero] * EPG
    for g in range(N_GRP):
        hit = grp == float(g)
        sg, ag = rows(sel, g), rows(aff, g)
        sv = [jnp.where(hit, sg[e], sv[e]) for e in range(EPG)]
        av = [jnp.where(hit, ag[e], av[e]) for e in range(EPG)]

    def argmax_first(vals, skip=None):
        bv, bi = None, None
        for e in range(EPG):
            v = vals[e] if skip is None else jnp.where(skip == float(e), -jnp.inf, vals[e])
            if bv is None:
                bv, bi = v, jnp.zeros_like(v)
            else:
                better = v > bv
                bv = jnp.where(better, v, bv)
                bi = jnp.where(better, float(e), bi)
        return bi

    i1 = argmax_first(sv)
    i2 = argmax_first(sv, skip=i1)
    lo = jnp.minimum(i1, i2)
    hi = jnp.maximum(i1, i2)
    pick = lambda idx: sum(jnp.where(idx == float(e), av[e], 0.0) for e in range(EPG))
    a_lo, a_hi = pick(lo), pick(hi)
    den = a_lo + a_hi
    pair = jnp.where(lo == 0.0, 0.0, jnp.where(lo == 1.0, 3.0, 5.0)) + hi - lo - 1.0
    return grp * float(N_PAIR) + pair, a_lo / den, a_hi / den


def _post_kernel(x_ref, octx_ref, ona_ref, omla_ref, mod_ref, n2_ref, wout_ref, wr_ref, br_ref,
                 x1_ref, h2_ref, route_ref, gate_ref, counts_ref, carry):
    i = pl.program_id(0)

    @pl.when(i == 0)
    def _():
        carry[...] = jnp.zeros_like(carry)

    is_ctx = i < CTX_TILES
    o_lat = jnp.concatenate([ona_ref[...], omla_ref[...]], axis=1)
    o = jnp.where(is_ctx, octx_ref[...], o_lat)
    m = mod_ref[0]
    x1 = x_ref[...] + m[2:3, :] * _dot(o, wout_ref[0])
    x1_ref[...] = x1
    h2 = _rms(x1, n2_ref[0]) * (1.0 + m[4:5, :]) + m[3:4, :]
    logits = lax.dot_general(wr_ref[...], h2, (((1,), (1,)), ((), ())),
                             preferred_element_type=F32, precision=lax.Precision.HIGHEST)
    aff = jax.nn.sigmoid(logits)
    bucket, g_lo, g_hi = _route(aff + br_ref[...], aff)

    hit = lax.broadcasted_iota(jnp.int32, (BUCKET_ROWS, TT), 0).astype(F32) == bucket
    earlier = lax.broadcasted_iota(jnp.int32, (TT, TT), 0) <= lax.broadcasted_iota(jnp.int32, (TT, TT), 1)
    incl = _dot(jnp.where(hit, 1.0, 0.0).astype(BF16), jnp.where(earlier, 1.0, 0.0).astype(BF16))
    base = carry[...]
    rank = jnp.sum(jnp.where(hit, incl + base[:, 0:1], 0.0), axis=0, keepdims=True) - 1.0
    carry[...] = base + incl[:, TT - 1:TT]
    counts_ref[...] = carry[...].astype(jnp.int32)
    route = jnp.concatenate([bucket, rank, jnp.zeros((6, TT), F32)], axis=0)
    route_ref[0] = route.astype(jnp.int32)
    gate_ref[0] = jnp.concatenate([g_lo, g_hi, jnp.zeros((6, TT), F32)], axis=0)
    _to_rows(h2_ref, h2, X_CHUNKS)


def _post(l, x, o_ctx, o_na, o_mla, mods, wts):
    tok = lambda n: pl.BlockSpec((TT, n), lambda i: (i, 0))
    lat = lambda n: pl.BlockSpec((TT, n), lambda i: (jnp.maximum(i - CTX_TILES, 0), 0))
    lay = lambda a: pl.BlockSpec((1,) + a.shape[1:], lambda i: (l,) + (0,) * (a.ndim - 1))
    full = lambda a: pl.BlockSpec(a.shape, lambda i: (0,) * a.ndim)
    return pl.pallas_call(
        _post_kernel,
        grid=(N_TILES,),
        in_specs=[tok(D), pl.BlockSpec((TT, D), lambda i: (_ctx_block(i), 0)), lat(D_NA), lat(D_MLA),
                  pl.BlockSpec((1, 6, D), lambda i: (l * N_MOD_ROWS + _mod_row(i), 0, 0)),
                  lay(wts["n2"]), lay(wts["w_out"]), full(wts["w_rt"]), full(wts["b_r"])],
        out_specs=[tok(D), pl.BlockSpec((TT * X_CHUNKS, LANE), lambda i: (i, 0)),
                   pl.BlockSpec((1, 8, TT), lambda i: (i, 0, 0)),
                   pl.BlockSpec((1, 8, TT), lambda i: (i, 0, 0)),
                   pl.BlockSpec((BUCKET_ROWS, LANE), lambda i: (0, 0))],
        out_shape=[jax.ShapeDtypeStruct((N_TOK, D), F32),
                   jax.ShapeDtypeStruct((N_TOK * X_CHUNKS, LANE), F32),
                   jax.ShapeDtypeStruct((N_TILES, 8, TT), jnp.int32),
                   jax.ShapeDtypeStruct((N_TILES, 8, TT), F32),
                   jax.ShapeDtypeStruct((BUCKET_ROWS, LANE), jnp.int32)],
        scratch_shapes=[pltpu.VMEM((BUCKET_ROWS, LANE), F32)],
        compiler_params=_params(("arbitrary",)),
        name="post",
    )(x, o_ctx, o_na, o_mla, mods, wts["n2"], wts["w_out"], wts["w_rt"], wts["b_r"])


def _moe_index_kernel(counts, elo, ehi, valid, total, first_tile):
    def per_bucket(b, cursor):
        cnt = counts[b]
        n_tiles = (cnt + (TM - 1)) // TM
        first_tile[b] = cursor
        grp = b // N_PAIR
        pair = b - grp * N_PAIR
        lo = jnp.where(pair < 3, 0, jnp.where(pair < 5, 1, 2))
        hi = jnp.where(pair < 3, pair + 1, jnp.where(pair < 5, pair - 1, 3))

        def per_tile(k, c):
            elo[cursor + k] = grp * EPG + lo
            ehi[cursor + k] = grp * EPG + hi
            valid[cursor + k] = jnp.minimum(cnt - k * TM, TM)
            return c
        lax.fori_loop(0, n_tiles, per_tile, 0)
        return cursor + n_tiles
    used = lax.fori_loop(0, N_BUCKET, per_bucket, 0)
    total[0] = used

    def unused_tile(k, c):
        elo[k] = elo[used - 1]
        ehi[k] = ehi[used - 1]
        valid[k] = 0
        return c
    lax.fori_loop(used, MOE_TILES, unused_tile, 0)

    def unused_bucket_row(b, c):
        first_tile[b] = used
        return c
    lax.fori_loop(N_BUCKET, BUCKET_ROWS, unused_bucket_row, 0)


def _moe_index(counts):
    smem = pl.BlockSpec(memory_space=pltpu.SMEM)
    tiles = jax.ShapeDtypeStruct((MOE_TILES,), jnp.int32)
    return pl.pallas_call(
        _moe_index_kernel,
        in_specs=[smem],
        out_specs=[smem] * 5,
        out_shape=[tiles, tiles, tiles, jax.ShapeDtypeStruct((1,), jnp.int32),
                   jax.ShapeDtypeStruct((BUCKET_ROWS,), jnp.int32)],
        name="moe_index",
    )(counts)


def _moe_kernel(elo_ref, ehi_ref, valid_ref, total_ref, slot_ref, glo_tok,
                h2_hbm, wg1, wu1, wd1, wg2, wu2, wd2, out_hbm,
                xbuf, ybuf, gvec, src, dst, glo, gsem, ssem, vsem):
    del elo_ref, ehi_ref
    t = pl.program_id(0)
    total = total_ref[0]
    par = t % 2

    def gate_copy(tile, p):
        return pltpu.make_async_copy(glo.at[pl.ds(pl.multiple_of(tile * TM, TM), TM)], gvec.at[p], vsem.at[p])

    def start_gather(tile, p):
        gate_copy(tile, p).start()
        def body(j, c):
            row = pl.multiple_of(src[tile * TM + j] * X_CHUNKS, X_CHUNKS)
            pltpu.make_async_copy(h2_hbm.at[pl.ds(row, X_CHUNKS), :],
                                  xbuf.at[p, pl.ds(j * X_CHUNKS, X_CHUNKS), :], gsem.at[p]).start()
            return c
        lax.fori_loop(0, TM, body, 0, unroll=DMA_UNROLL)

    def start_scatter(tile, p):
        def body(j, c):
            row = pl.multiple_of(dst[tile * TM + j] * Y_CHUNKS, Y_CHUNKS)
            pltpu.make_async_copy(ybuf.at[p, pl.ds(j * Y_CHUNKS, Y_CHUNKS), :],
                                  out_hbm.at[pl.ds(row, Y_CHUNKS), :], ssem.at[p]).start()
            return c
        lax.fori_loop(0, TM, body, 0, unroll=DMA_UNROLL)

    def wait_gather(p):
        pltpu.make_async_copy(h2_hbm.at[pl.ds(0, TM * X_CHUNKS), :], xbuf.at[p], gsem.at[p]).wait()
        gate_copy(0, p).wait()

    def wait_scatter(p):
        pltpu.make_async_copy(ybuf.at[p], out_hbm.at[pl.ds(0, TM * Y_CHUNKS), :], ssem.at[p]).wait()

    @pl.when(t == 0)
    def _():
        def place(i, c):
            s = slot_ref[i]
            src[s] = i
            dst[s] = i
            glo[s] = glo_tok[i]
            return c
        lax.fori_loop(0, N_TOK, place, 0, unroll=DMA_UNROLL)

        def pad_tile(tile, c):
            def pad_slot(j, c2):
                src[tile * TM + j] = src[tile * TM]
                dst[tile * TM + j] = N_TOK + j
                glo[tile * TM + j] = 0.0
                return c2
            return lax.fori_loop(valid_ref[tile], TM, pad_slot, c)
        lax.fori_loop(0, total, pad_tile, 0)
        ybuf[0] = jnp.zeros((TM * Y_CHUNKS, LANE), F32)
        spare = pltpu.make_async_copy(ybuf.at[0], out_hbm.at[pl.ds(N_TOK * Y_CHUNKS, TM * Y_CHUNKS), :], ssem.at[0])
        spare.start()
        spare.wait()
        start_gather(0, 0)

    @pl.when(t + 1 < total)
    def _():
        start_gather(t + 1, 1 - par)

    @pl.when(t < total)
    def _():
        wait_gather(par)

        @pl.when(t >= 2)
        def _():
            wait_scatter(par)

        x = _from_rows(xbuf, TM, X_CHUNKS, 0, X_CHUNKS, lead=(par,)).astype(BF16)
        g_row = gvec[pl.ds(par, 1), :]
        g_lo = jnp.transpose(jnp.concatenate([g_row, jnp.zeros((7, TM), F32)], axis=0))[:, 0:1]
        y = None
        for gate, (wg, wu, wd) in ((g_lo, (wg1, wu1, wd1)), (1.0 - g_lo, (wg2, wu2, wd2))):
            a = _dot(x, wg[0, 0])
            hid = a * jax.nn.sigmoid(a) * _dot(x, wu[0, 0]) * gate
            part = _dot(hid.astype(BF16), wd[0, 0])
            y = part if y is None else y + part
        _to_rows(ybuf, y, Y_CHUNKS, lead=(par,))
        start_scatter(t, par)

        @pl.when(t == total - 1)
        def _():
            @pl.when(t >= 1)
            def _():
                wait_scatter(1 - par)
            wait_scatter(par)


def _moe(l, h2, route, gate, counts, wts):
    bucket = route[:, 0, :].reshape(N_TOK)
    rank = route[:, 1, :].reshape(N_TOK)
    g_lo = gate[:, 0, :].reshape(N_TOK)
    e_lo, e_hi, valid, total, first_tile = _moe_index(counts[:, 0])
    buckets = jnp.arange(N_BUCKET, dtype=jnp.int32)
    slot = jnp.where(bucket[:, None] == buckets[None, :], (first_tile[:N_BUCKET] * TM)[None, :], 0).sum(axis=1) + rank
    up = lambda which: pl.BlockSpec((1, 1, D, D_EXP), lambda t, *pf: (l, pf[which][t], 0, 0))
    down = lambda which: pl.BlockSpec((1, 1, D_EXP, D), lambda t, *pf: (l, pf[which][t], 0, 0))
    first, second = 0, 1
    slots = MOE_TILES * TM
    return pl.pallas_call(
        _moe_kernel,
        grid_spec=pltpu.PrefetchScalarGridSpec(
            num_scalar_prefetch=6,
            grid=(MOE_TILES,),
            in_specs=[pl.BlockSpec(memory_space=pl.ANY),
                      up(first), up(first), down(first), up(second), up(second), down(second)],
            out_specs=pl.BlockSpec(memory_space=pl.ANY),
            scratch_shapes=[pltpu.VMEM((2, TM * X_CHUNKS, LANE), F32), pltpu.VMEM((2, TM * Y_CHUNKS, LANE), F32),
                            pltpu.VMEM((2, TM), F32),
                            pltpu.SMEM((slots,), jnp.int32), pltpu.SMEM((slots,), jnp.int32),
                            pltpu.SMEM((slots,), F32),
                            pltpu.SemaphoreType.DMA((2,)), pltpu.SemaphoreType.DMA((2,)),
                            pltpu.SemaphoreType.DMA((2,))],
        ),
        out_shape=jax.ShapeDtypeStruct(((N_TOK + TM) * Y_CHUNKS, LANE), F32),
        compiler_params=_params(("arbitrary",)),
        name="moe",
    )(e_lo, e_hi, valid, total, slot, g_lo, h2,
      wts["w_gate"], wts["w_up"], wts["w_down"], wts["w_gate"], wts["w_up"], wts["w_down"])


def _final_kernel(x1_ref, moe_ref, mod_ref, g_ref, yc_ref, yl_ref):
    i = pl.program_id(0)
    moe = _from_rows(moe_ref, TT, Y_CHUNKS, 0, Y_CHUNKS)
    y = _rms(x1_ref[...] + mod_ref[0, 5:6, :] * moe, g_ref[...])

    @pl.when(i < CTX_TILES)
    def _():
        yc_ref[...] = y

    @pl.when(i >= CTX_TILES)
    def _():
        yl_ref[...] = y


def _final(x1, moe, mods, g):
    tok = pl.BlockSpec((TT, D), lambda i: (i, 0))
    l = DEPTH - 1
    return pl.pallas_call(
        _final_kernel,
        grid=(N_TILES,),
        in_specs=[tok, pl.BlockSpec((TT * Y_CHUNKS, LANE), lambda i: (i, 0)),
                  pl.BlockSpec((1, 6, D), lambda i: (l * N_MOD_ROWS + _mod_row(i), 0, 0)),
                  pl.BlockSpec((1, D), lambda i: (0, 0))],
        out_specs=[pl.BlockSpec((TT, D), lambda i: (_ctx_block(i), 0)),
                   pl.BlockSpec((TT, D), lambda i: (jnp.maximum(i - CTX_TILES, 0), 0))],
        out_shape=[jax.ShapeDtypeStruct((N_CTX, D), F32), jax.ShapeDtypeStruct((N_LAT, D), F32)],
        compiler_params=_params(("arbitrary",)),
        name="final",
    )(x1, moe, mods, g)


def _swap_halves(w):
    q = ROPE // 4
    return jnp.concatenate([-w[..., q:2 * q], w[..., :q], -w[..., 3 * q:], w[..., 2 * q:3 * q]], axis=-1)


def _pad_head(nope, rope):
    z = jnp.zeros(rope.shape[:-1] + (HP - NOPE - ROPE,), rope.dtype)
    return jnp.concatenate([nope, rope, z], axis=-1)


def _prepare_weights(norm1_g, norm2_g, w_in, q_norm_g, w_uq, kv_norm_g, w_ukv, w_out, w_router, b_router,
                     w_gate, w_up, w_down):
    o = 3 * D_NA + Q_LORA + KV_LORA
    w_kr = w_in[:, :, o:]
    zeros_nope = jnp.zeros((DEPTH, D, NOPE), F32)
    w_in_p = jnp.concatenate([w_in[:, :, :o], _pad_head(zeros_nope, w_kr),
                              _pad_head(zeros_nope, _swap_halves(w_kr))], axis=-1)
    wq = w_uq.reshape(DEPTH, Q_LORA, H_MLA, NOPE + ROPE)
    plain = _pad_head(wq[..., :NOPE], wq[..., NOPE:]).reshape(DEPTH, Q_LORA, D_QP)
    swapped = _pad_head(jnp.zeros_like(wq[..., :NOPE]), _swap_halves(wq[..., NOPE:])).reshape(DEPTH, Q_LORA, D_QP)
    wkv = w_ukv.reshape(DEPTH, KV_LORA, H_MLA, NOPE + VH)
    w_uk_p = jnp.concatenate([wkv[..., :NOPE], jnp.zeros((DEPTH, KV_LORA, H_MLA, HP - NOPE), F32)],
                             axis=-1).reshape(DEPTH, KV_LORA, D_QP)
    w_uv = wkv[..., NOPE:].reshape(DEPTH, KV_LORA, D_MLA)
    return {
        "n1": norm1_g.reshape(DEPTH, 1, D),
        "n2": norm2_g.reshape(DEPTH, 1, D),
        "w_in": w_in_p.astype(BF16),
        "qn": q_norm_g.reshape(DEPTH, 1, Q_LORA),
        "w_uq": jnp.concatenate([plain, swapped], axis=-1).astype(BF16),
        "kvn": kv_norm_g.reshape(DEPTH, 1, KV_LORA),
        "w_uk": w_uk_p.astype(BF16),
        "w_uv": w_uv.astype(BF16),
        "w_out": w_out.astype(BF16),
        "w_rt": w_router.T,
        "b_r": b_router.reshape(N_EXP, 1),
        "w_gate": w_gate.astype(BF16),
        "w_up": w_up.astype(BF16),
        "w_down": w_down.astype(BF16),
    }


def _rope_tables():
    half = ROPE // 2
    freqs = 1.0 / (ROPE_THETA ** (np.arange(0, half, 2, dtype=np.float32) / half))
    pos = np.arange(S_LAT)
    ang_r = (pos // GRID_W).astype(np.float32)[:, None] * freqs
    ang_c = (pos % GRID_W).astype(np.float32)[:, None] * freqs
    ang = np.concatenate([ang_r, ang_r, ang_c, ang_c], axis=1)
    cos = np.ones((TT + S_LAT, LANE), np.float32)
    sin = np.zeros((TT + S_LAT, LANE), np.float32)
    cos[TT:, NOPE:NOPE + ROPE] = np.cos(ang)
    sin[TT:, NOPE:NOPE + ROPE] = np.sin(ang)
    return jnp.asarray(cos), jnp.asarray(sin)


def kernel(x_prompt, x_sample, cache_nat_k, cache_nat_v, cache_mla_ckv, cache_mla_krope, c, c_ctx, norm1_g, norm2_g, w_ada, b_ada, w_in, q_norm_g, w_uq, kv_norm_g, w_ukv, rpb, w_out, w_router, b_router, w_gate, w_up, w_down, final_norm_g):
    wts = _prepare_weights(norm1_g, norm2_g, w_in, q_norm_g, w_uq, kv_norm_g, w_ukv, w_out, w_router, b_router,
                           w_gate, w_up, w_down)
    tabs = _rope_tables()
    cond = jnp.concatenate([c_ctx[None, :], c, jnp.zeros((N_MOD_ROWS - 1 - B_LAT, D), F32)], axis=0)
    mods = _ada(cond, w_ada, b_ada).reshape(DEPTH * N_MOD_ROWS, 6, D)
    place = jnp.asarray(np.eye(ROPE, HP, k=NOPE, dtype=np.float32)).astype(BF16)
    kc_all, vc_all = _ctxkv(cache_mla_ckv, cache_mla_krope, wts["w_uk"], wts["w_uv"], place)
    cache_k = cache_nat_k.reshape(B_LAT, DEPTH, PAST, D_NA)
    cache_v = cache_nat_v.reshape(B_LAT, DEPTH, PAST, D_NA)
    bias = _na_bias_tables(rpb)

    x1 = moe = None
    new_k, new_v, new_ckv, new_kr = [], [], [], []
    for l in range(DEPTH):
        xin = (x_prompt.reshape(N_CTX, D), x_sample.reshape(N_LAT, D)) if l == 0 else (x1, moe)
        x, qna, kna, vna, qp, kp, vm, k_new, v_new, ckv_new, kr_new = _pre(l, l == 0, xin, mods, wts, tabs)
        new_k.append(k_new)
        new_v.append(v_new)
        new_ckv.append(ckv_new)
        new_kr.append(kr_new)
        o_ctx = _ctx_attn(qna, kna, vna, qp, kp, vm)
        o_na = _na_attn(l, qna, kna, vna, cache_k, cache_v, bias)
        o_mla = _mla_attn(l, qp, kp, vm, kc_all, vc_all)
        x1, h2, route, gate, counts = _post(l, x, o_ctx, o_na, o_mla, mods, wts)
        moe = _moe(l, h2, route, gate, counts, wts)
    y_ctx, y_lat = _final(x1, moe, mods, final_norm_g.reshape(1, D))

    stack = lambda parts, tail: jnp.stack([p.reshape((B_CTX, S_CTX) + tail) for p in parts], axis=1)
    return (y_ctx.reshape(B_CTX, S_CTX, D), y_lat.reshape(B_LAT, S_LAT, D),
            stack(new_k, (H_NA, DH_NA)), stack(new_v, (H_NA, DH_NA)),
            stack(new_ckv, (KV_LORA,)), stack(new_kr, (ROPE,)))
```

```python
import functools

import numpy as np
import jax
import jax.numpy as jnp
from jax import lax
from jax.experimental import pallas as pl
from jax.experimental.pallas import tpu as pltpu

F32 = jnp.float32
BF16 = jnp.bfloat16

D = 1024
B_CTX, S_CTX = 32, 256
B_LAT, S_LAT = 8, 2048
DEPTH = 4
PAST = 256
GRID_W, GRID_H = 64, 32
H_NA, DH_NA = 8, 64
WIN_H, WIN_W = 8, 16
H_MLA, NOPE, ROPE, VH = 8, 64, 32, 64
Q_LORA, KV_LORA = 256, 128
D_NA = H_NA * DH_NA
D_MLA = H_MLA * VH
N_EXP, N_GRP, EPG = 16, 4, 4
D_EXP = 512
ROPE_THETA = 10000.0
EPS = 1e-6
NA_SCALE = DH_NA ** -0.5
MLA_SCALE = (NOPE + ROPE) ** -0.5
LOG2E = 1.4426950408889634

N_CTX = B_CTX * S_CTX
N_LAT = B_LAT * S_LAT
N_TOK = N_CTX + N_LAT
TM = 256
TT = 512
CTX_TILES = N_CTX // TT
LAT_TILES = N_LAT // TT
N_TILES = CTX_TILES + LAT_TILES
LAT_TILES_PER_SEQ = S_LAT // TT
Q_TILES_PER_SEQ = S_LAT // TM
TQ_MLA = 512
LANE = 128
HP = LANE
D_QP = H_MLA * HP
W_IN_COLS = 3 * D_NA + Q_LORA + KV_LORA + 2 * LANE
N_MOD_ROWS = 16
N_PAIR = 6
N_BUCKET = N_GRP * N_PAIR
MOE_TILES = N_TOK // TM + N_BUCKET
BUCKET_ROWS = 32
Y_CHUNKS = D // LANE
X_CHUNKS = D // LANE


def _to_rows(ref, value, chunks, lead=()):
    t = value.shape[0]
    for c in range(value.shape[1] // LANE):
        ref[lead + (pl.ds(c, t, stride=chunks), slice(None))] = value[:, c * LANE:(c + 1) * LANE]


def _from_rows(ref, t, chunks, first, n, lead=()):
    parts = [ref[lead + (pl.ds(first + c, t, stride=chunks), slice(None))] for c in range(n)]
    return parts[0] if n == 1 else jnp.concatenate(parts, axis=1)
NA_ROWS_PER_TILE = TM // GRID_W
NA_KEY_ROWS = 12
NA_KEYS = NA_KEY_ROWS * GRID_W
VMEM_LIMIT = 48 * 1024 * 1024
DMA_UNROLL = 16


def _dot(a, b):
    return jnp.dot(a, b, preferred_element_type=F32)


def _dot_nt(a, b):
    return lax.dot_general(a, b, (((1,), (1,)), ((), ())), preferred_element_type=F32)


def _rms(x, g):
    return x * lax.rsqrt(jnp.mean(x * x, axis=-1, keepdims=True) + EPS) * g


def _mod_row(i):
    return jnp.where(i < CTX_TILES, 0, 1 + (i - CTX_TILES) // LAT_TILES_PER_SEQ)


def _pos_block(i):
    return jnp.where(i < CTX_TILES, 0, 1 + (i - CTX_TILES) % LAT_TILES_PER_SEQ)


def _ctx_block(i):
    return jnp.minimum(i, CTX_TILES - 1)


def _params(sem):
    return pltpu.CompilerParams(dimension_semantics=sem, vmem_limit_bytes=VMEM_LIMIT)


def _ada_kernel(cond_ref, w_ref, b_ref, o_ref):
    cond = cond_ref[...]
    act = cond * jax.nn.sigmoid(cond)
    o_ref[0] = jnp.dot(act, w_ref[0], preferred_element_type=F32,
                       precision=lax.Precision.HIGHEST) + b_ref[0]


def _ada(cond, w_ada, b_ada):
    nt = 1536
    return pl.pallas_call(
        _ada_kernel,
        grid=(DEPTH, 6 * D // nt),
        in_specs=[pl.BlockSpec((N_MOD_ROWS, D), lambda l, j: (0, 0)),
                  pl.BlockSpec((1, D, nt), lambda l, j: (l, 0, j)),
                  pl.BlockSpec((1, 1, nt), lambda l, j: (l, 0, j))],
        out_specs=pl.BlockSpec((1, N_MOD_ROWS, nt), lambda l, j: (l, 0, j)),
        out_shape=jax.ShapeDtypeStruct((DEPTH, N_MOD_ROWS, 6 * D), F32),
        compiler_params=_params(("arbitrary", "arbitrary")),
        name="ada",
    )(cond, w_ada, b_ada.reshape(DEPTH, 1, 6 * D))


def _ctxkv_kernel(ckv_ref, kr_ref, wuk_ref, wuv_ref, place_ref, kc_ref, vc_ref):
    ckv = ckv_ref[0, 0].astype(BF16)
    kr = _dot(kr_ref[0, 0].astype(BF16), place_ref[...])
    kc_ref[0, 0] = (_dot(ckv, wuk_ref[0]) + jnp.tile(kr, (1, H_MLA))).astype(BF16)
    vc_ref[0, 0] = _dot(ckv, wuv_ref[0]).astype(BF16)


def _ctxkv(cache_ckv, cache_krope, w_uk_p, w_uv, place):
    return pl.pallas_call(
        _ctxkv_kernel,
        grid=(DEPTH, B_LAT),
        in_specs=[pl.BlockSpec((1, 1, PAST, KV_LORA), lambda l, b: (b, l, 0, 0)),
                  pl.BlockSpec((1, 1, PAST, ROPE), lambda l, b: (b, l, 0, 0)),
                  pl.BlockSpec((1, KV_LORA, D_QP), lambda l, b: (l, 0, 0)),
                  pl.BlockSpec((1, KV_LORA, D_MLA), lambda l, b: (l, 0, 0)),
                  pl.BlockSpec((ROPE, HP), lambda l, b: (0, 0))],
        out_specs=[pl.BlockSpec((1, 1, PAST, D_QP), lambda l, b: (l, b, 0, 0)),
                   pl.BlockSpec((1, 1, PAST, D_MLA), lambda l, b: (l, b, 0, 0))],
        out_shape=[jax.ShapeDtypeStruct((DEPTH, B_LAT, PAST, D_QP), BF16),
                   jax.ShapeDtypeStruct((DEPTH, B_LAT, PAST, D_MLA), BF16)],
        compiler_params=_params(("arbitrary", "arbitrary")),
        name="ctxkv",
    )(cache_ckv, cache_krope, w_uk_p, w_uv, place)


def _pre_kernel(first, *refs):
    if first:
        xc_ref, xl_ref, *rest = refs
    else:
        x1_ref, moe_ref, modp_ref, *rest = refs
    (mod_ref, n1_ref, win_ref, qn_ref, wuq_ref, kvn_ref, wuk_ref, wuv_ref, cos_ref, sin_ref,
     xo_ref, qna_ref, kna_ref, vna_ref, qp_ref, kp_ref, vm_ref, knew_ref, vnew_ref, ckvnew_ref, krnew_ref) = rest
    i = pl.program_id(0)

    if first:
        x = jnp.where(i < CTX_TILES, xc_ref[...], xl_ref[...])
    else:
        x = x1_ref[...] + modp_ref[0, 5:6, :] * _from_rows(moe_ref, TT, Y_CHUNKS, 0, Y_CHUNKS)
    xo_ref[...] = x
    m = mod_ref[0]
    h = _rms(x, n1_ref[0]) * (1.0 + m[1:2, :]) + m[0:1, :]
    proj = _dot(h.astype(BF16), win_ref[0])
    k_na = proj[:, D_NA:2 * D_NA]
    v_na = proj[:, 2 * D_NA:3 * D_NA]
    qna_ref[...] = (proj[:, :D_NA] * NA_SCALE).astype(BF16)
    kna_ref[...] = k_na.astype(BF16)
    vna_ref[...] = v_na.astype(BF16)
    o = 3 * D_NA
    cq = proj[:, o:o + Q_LORA]
    ckv = proj[:, o + Q_LORA:o + Q_LORA + KV_LORA]
    kr = proj[:, o + Q_LORA + KV_LORA:o + Q_LORA + KV_LORA + LANE]
    kr_sw = proj[:, o + Q_LORA + KV_LORA + LANE:]
    cos = cos_ref[...]
    sin = sin_ref[...]

    cqn = _rms(cq, qn_ref[0]).astype(BF16)
    q2 = _dot(cqn, wuq_ref[0])
    qp = q2[:, :D_QP] * jnp.tile(cos, (1, H_MLA)) + q2[:, D_QP:] * jnp.tile(sin, (1, H_MLA))
    qp_ref[...] = qp.astype(BF16)

    ckvn = _rms(ckv, kvn_ref[0])
    ckvb = ckvn.astype(BF16)
    kr_rot = kr * cos + kr_sw * sin
    kp_ref[...] = (_dot(ckvb, wuk_ref[0]) + jnp.tile(kr_rot, (1, H_MLA))).astype(BF16)
    vm_ref[...] = _dot(ckvb, wuv_ref[0]).astype(BF16)

    @pl.when(i < CTX_TILES)
    def _():
        knew_ref[...] = k_na
        vnew_ref[...] = v_na
        ckvnew_ref[...] = ckvn
        krnew_ref[...] = kr[:, NOPE:NOPE + ROPE]


def _pre(l, first, xin, mods, wts, tabs):
    tok = lambda n: pl.BlockSpec((TT, n), lambda i: (i, 0))
    ctx = lambda n: pl.BlockSpec((TT, n), lambda i: (_ctx_block(i), 0))
    lat = lambda n: pl.BlockSpec((TT, n), lambda i: (jnp.maximum(i - CTX_TILES, 0), 0))
    lay = lambda a: pl.BlockSpec((1,) + a.shape[1:], lambda i: (l,) + (0,) * (a.ndim - 1))
    in_specs = []
    args = []
    if first:
        in_specs += [ctx(D), lat(D)]
        args += list(xin)
    else:
        x1, moe = xin
        in_specs += [tok(D), pl.BlockSpec((TT * Y_CHUNKS, LANE), lambda i: (i, 0)),
                     pl.BlockSpec((1, 6, D), lambda i: ((l - 1) * N_MOD_ROWS + _mod_row(i), 0, 0))]
        args += [x1, moe, mods]
    in_specs.append(pl.BlockSpec((1, 6, D), lambda i: (l * N_MOD_ROWS + _mod_row(i), 0, 0)))
    args.append(mods)
    for name in ("n1", "w_in", "qn", "w_uq", "kvn", "w_uk", "w_uv"):
        in_specs.append(lay(wts[name]))
        args.append(wts[name])
    for t in tabs:
        in_specs.append(pl.BlockSpec((TT, LANE), lambda i: (_pos_block(i), 0)))
        args.append(t)
    out_specs = [tok(D), tok(D_NA), tok(D_NA), tok(D_NA), tok(D_QP), tok(D_QP), tok(D_MLA),
                 ctx(D_NA), ctx(D_NA), ctx(KV_LORA), ctx(ROPE)]
    out_shape = [jax.ShapeDtypeStruct((N_TOK, D), F32),
                 jax.ShapeDtypeStruct((N_TOK, D_NA), BF16),
                 jax.ShapeDtypeStruct((N_TOK, D_NA), BF16),
                 jax.ShapeDtypeStruct((N_TOK, D_NA), BF16),
                 jax.ShapeDtypeStruct((N_TOK, D_QP), BF16),
                 jax.ShapeDtypeStruct((N_TOK, D_QP), BF16),
                 jax.ShapeDtypeStruct((N_TOK, D_MLA), BF16),
                 jax.ShapeDtypeStruct((N_CTX, D_NA), F32),
                 jax.ShapeDtypeStruct((N_CTX, D_NA), F32),
                 jax.ShapeDtypeStruct((N_CTX, KV_LORA), F32),
                 jax.ShapeDtypeStruct((N_CTX, ROPE), F32)]
    return pl.pallas_call(
        functools.partial(_pre_kernel, first),
        grid=(N_TILES,),
        in_specs=in_specs,
        out_specs=out_specs,
        out_shape=out_shape,
        compiler_params=_params(("arbitrary",)),
        name="pre",
    )(*args)


def _softmax_parts(parts, scale=None):
    m = parts[0].max(axis=-1, keepdims=True)
    for s in parts[1:]:
        m = jnp.maximum(m, s.max(axis=-1, keepdims=True))
    if scale is None:
        ps = [jnp.exp(s - m) for s in parts]
    else:
        ps = [jnp.exp2((s - m) * (scale * LOG2E)) for s in parts]
    den = ps[0].sum(axis=-1, keepdims=True)
    for p in ps[1:]:
        den = den + p.sum(axis=-1, keepdims=True)
    return ps, den


def _low_lanes(rows=TM):
    return lax.broadcasted_iota(jnp.int32, (rows, LANE), 1) < DH_NA


def _split_pair(qpair, low):
    zero = jnp.zeros_like(qpair)
    return jnp.where(low, qpair, zero), jnp.where(low, zero, qpair)


def _ctx_attn_kernel(qna_ref, kna_ref, vna_ref, qp_ref, kp_ref, vm_ref, o_ref):
    low = _low_lanes()
    outs = []
    for hp in range(H_NA // 2):
        sl = slice(hp * LANE, (hp + 1) * LANE)
        kpair = kna_ref[:, sl]
        vpair = vna_ref[:, sl]
        res = []
        for qh in _split_pair(qna_ref[:, sl], low):
            (p,), den = _softmax_parts([_dot_nt(qh, kpair)])
            res.append(_dot(p.astype(BF16), vpair) / den)
        outs.append(jnp.where(low, res[0], res[1]))
    for hp in range(H_MLA // 2):
        vpair = vm_ref[:, hp * LANE:(hp + 1) * LANE]
        res = []
        for h in (2 * hp, 2 * hp + 1):
            sl = slice(h * HP, (h + 1) * HP)
            (p,), den = _softmax_parts([_dot_nt(qp_ref[:, sl], kp_ref[:, sl])], MLA_SCALE)
            res.append(_dot(p.astype(BF16), vpair) / den)
        outs.append(jnp.where(low, res[0], res[1]))
    o_ref[...] = jnp.concatenate(outs, axis=1).astype(BF16)


def _ctx_attn(qna, kna, vna, qp, kp, vm):
    tok = lambda n: pl.BlockSpec((S_CTX, n), lambda i: (i, 0))
    return pl.pallas_call(
        _ctx_attn_kernel,
        grid=(B_CTX,),
        in_specs=[tok(D_NA), tok(D_NA), tok(D_NA), tok(D_QP), tok(D_QP), tok(D_MLA)],
        out_specs=tok(D_NA + D_MLA),
        out_shape=jax.ShapeDtypeStruct((N_CTX, D_NA + D_MLA), BF16),
        compiler_params=_params(("arbitrary",)),
        name="ctx_attn",
    )(qna, kna, vna, qp, kp, vm)


def _na_kernel(q_ref, k_ref, v_ref, kc_ref, vc_ref, bias_ref, o_ref):
    t = pl.program_id(1)
    row0 = jnp.clip(t * NA_ROWS_PER_TILE - WIN_H // 2, 0, GRID_H - NA_KEY_ROWS)
    ks = pl.multiple_of(row0 * GRID_W, GRID_W)
    low = _low_lanes()
    outs = []
    for hp in range(H_NA // 2):
        sl = slice(hp * LANE, (hp + 1) * LANE)
        kwin = k_ref[pl.ds(ks, NA_KEYS), sl]
        vwin = v_ref[pl.ds(ks, NA_KEYS), sl]
        kctx = kc_ref[0, 0, :, sl].astype(BF16)
        vctx = vc_ref[0, 0, :, sl].astype(BF16)
        res = []
        for half, qh in enumerate(_split_pair(q_ref[:, sl], low)):
            s_win = _dot_nt(qh, kwin) + bias_ref[0, 0, 2 * hp + half]
            s_ctx = _dot_nt(qh, kctx)
            (p_win, p_ctx), den = _softmax_parts([s_win, s_ctx])
            o = _dot(p_win.astype(BF16), vwin) + _dot(p_ctx.astype(BF16), vctx)
            res.append(o / den)
        outs.append(jnp.where(low, res[0], res[1]))
    o_ref[...] = jnp.concatenate(outs, axis=1).astype(BF16)


def _na_attn(l, qna, kna, vna, cache_k, cache_v, bias):
    nt = Q_TILES_PER_SEQ
    seq = lambda n: pl.BlockSpec((S_LAT, n), lambda b, t: (N_CTX // S_LAT + b, 0))
    cache = pl.BlockSpec((1, 1, PAST, D_NA), lambda b, t: (b, l, 0, 0))
    kind = lambda t: jnp.where(t == 0, 0, jnp.where(t == nt - 1, 2, 1))
    return pl.pallas_call(
        _na_kernel,
        grid=(B_LAT, nt),
        in_specs=[pl.BlockSpec((TM, D_NA), lambda b, t: (N_CTX // TM + b * nt + t, 0)),
                  seq(D_NA), seq(D_NA), cache, cache,
                  pl.BlockSpec((1, 1, H_NA, TM, NA_KEYS), lambda b, t: (l, kind(t), 0, 0, 0))],
        out_specs=pl.BlockSpec((TM, D_NA), lambda b, t: (b * nt + t, 0)),
        out_shape=jax.ShapeDtypeStruct((N_LAT, D_NA), BF16),
        compiler_params=_params(("arbitrary", "arbitrary")),
        name="na_attn",
    )(qna, kna, vna, cache_k, cache_v, bias)


def _na_bias_tables(rpb):
    n_dr, n_dc = 2 * WIN_H - 1, 2 * WIN_W - 1
    qc = np.arange(GRID_W)[:, None]
    kc = np.arange(GRID_W)[None, :]
    ws = np.clip(qc - WIN_W // 2, 0, GRID_W - WIN_W)
    col_ok = (kc >= ws) & (kc < ws + WIN_W)
    col_hot = ((kc - qc + WIN_W - 1)[..., None] == np.arange(n_dc)) & col_ok[..., None]
    col_hot = col_hot.reshape(GRID_W * GRID_W, n_dc).astype(np.float32)
    by_col = jnp.einsum("pb,lhab->lhap", jnp.asarray(col_hot), rpb, precision=lax.Precision.HIGHEST)
    by_col = jnp.where(jnp.asarray(col_ok), by_col.reshape(DEPTH, H_NA, n_dr, GRID_W, GRID_W), -jnp.inf)
    pad = jnp.full((DEPTH, H_NA, _BIAS_PAD, GRID_W, GRID_W), -jnp.inf, F32)
    blocks = jnp.concatenate([pad, by_col, pad], axis=2)
    pairs = jnp.concatenate([blocks[:, :, :-1], blocks[:, :, 1:]], axis=-1)
    n_pairs = n_dr + 2 * _BIAS_PAD - 1
    return pl.pallas_call(
        _na_bias_kernel,
        grid=(DEPTH, H_NA),
        in_specs=[pl.BlockSpec((1, 1, n_pairs, GRID_W, LANE), lambda l, h: (l, h, 0, 0, 0))],
        out_specs=pl.BlockSpec((1, len(_NA_KINDS), 1, TM, NA_KEYS), lambda l, h: (l, 0, h, 0, 0)),
        out_shape=jax.ShapeDtypeStruct((DEPTH, len(_NA_KINDS), H_NA, TM, NA_KEYS), F32),
        compiler_params=_params(("arbitrary", "arbitrary")),
        name="na_bias",
    )(pairs)


_NA_KINDS = ((0, 0), (NA_ROWS_PER_TILE, 0), (GRID_H - NA_ROWS_PER_TILE, GRID_H - NA_KEY_ROWS))
_BIAS_PAD = 4


def _na_bias_kernel(pairs_ref, o_ref):
    low = lax.broadcasted_iota(jnp.int32, (GRID_W, LANE), 1) < GRID_W
    ninf = jnp.full((GRID_W, LANE), -jnp.inf, F32)
    for kind, (r0, k0) in enumerate(_NA_KINDS):
        for i in range(NA_ROWS_PER_TILE):
            r = r0 + i
            rs = min(max(r - WIN_H // 2, 0), GRID_H - WIN_H)
            for m in range(NA_KEY_ROWS // 2):
                kr = k0 + 2 * m
                ok0 = rs <= kr < rs + WIN_H
                ok1 = rs <= kr + 1 < rs + WIN_H
                blk = pairs_ref[0, 0, kr - r + WIN_H - 1 + _BIAS_PAD]
                if ok0 and not ok1:
                    blk = jnp.where(low, blk, ninf)
                elif ok1 and not ok0:
                    blk = jnp.where(low, ninf, blk)
                elif not ok0:
                    blk = ninf
                o_ref[0, kind, 0, i * GRID_W:(i + 1) * GRID_W, m * LANE:(m + 1) * LANE] = blk


def _mla_kernel(q_ref, k_ref, v_ref, kc_ref, vc_ref, o_ref):
    low = _low_lanes(TQ_MLA)
    outs = []
    for hp in range(H_MLA // 2):
        vsl = slice(hp * LANE, (hp + 1) * LANE)
        vlat = v_ref[:, vsl]
        vctx = vc_ref[0, 0, :, vsl]
        res = []
        for h in (2 * hp, 2 * hp + 1):
            sl = slice(h * HP, (h + 1) * HP)
            q = q_ref[:, sl]
            s_ctx = _dot_nt(q, kc_ref[0, 0, :, sl])
            s_lat = _dot_nt(q, k_ref[:, sl])
            (p_ctx, p_lat), den = _softmax_parts([s_ctx, s_lat], MLA_SCALE)
            o = _dot(p_ctx.astype(BF16), vctx) + _dot(p_lat.astype(BF16), vlat)
            res.append(o / den)
        outs.append(jnp.where(low, res[0], res[1]))
    o_ref[...] = jnp.concatenate(outs, axis=1).astype(BF16)


def _mla_attn(l, qp, kp, vm, kc, vc):
    nt = S_LAT // TQ_MLA
    seq = lambda n: pl.BlockSpec((S_LAT, n), lambda b, t: (N_CTX // S_LAT + b, 0))
    return pl.pallas_call(
        _mla_kernel,
        grid=(B_LAT, nt),
        in_specs=[pl.BlockSpec((TQ_MLA, D_QP), lambda b, t: (N_CTX // TQ_MLA + b * nt + t, 0)),
                  seq(D_QP), seq(D_MLA),
                  pl.BlockSpec((1, 1, PAST, D_QP), lambda b, t: (l, b, 0, 0)),
                  pl.BlockSpec((1, 1, PAST, D_MLA), lambda b, t: (l, b, 0, 0))],
        out_specs=pl.BlockSpec((TQ_MLA, D_MLA), lambda b, t: (b * nt + t, 0)),
        out_shape=jax.ShapeDtypeStruct((N_LAT, D_MLA), BF16),
        compiler_params=_params(("arbitrary", "arbitrary")),
        name="mla_attn",
    )(qp, kp, vm, kc, vc)


def _route(sel, aff):
    rows = lambda a, g: [a[EPG * g + e:EPG * g + e + 1, :] for e in range(EPG)]
    best = None
    for g in range(N_GRP):
        v = rows(sel, g)
        score = v[0] + v[1]
        for a, b in ((0, 2), (0, 3), (1, 2), (1, 3), (2, 3)):
            score = jnp.maximum(score, v[a] + v[b])
        if best is None:
            best, grp = score, jnp.zeros_like(score)
        else:
            better = score > best
            best = jnp.where(better, score, best)
            grp = jnp.where(better, float(g), grp)
    zero = jnp.zeros_like(best)
    sv = [zero] * EPG
    av = [zero] * EPG
    for g in range(N_GRP):
        hit = grp == float(g)
        sg, ag = rows(sel, g), rows(aff, g)
        sv = [jnp.where(hit, sg[e], sv[e]) for e in range(EPG)]
        av = [jnp.where(hit, ag[e], av[e]) for e in range(EPG)]

    def argmax_first(vals, skip=None):
        bv, bi = None, None
        for e in range(EPG):
            v = vals[e] if skip is None else jnp.where(skip == float(e), -jnp.inf, vals[e])
            if bv is None:
                bv, bi = v, jnp.zeros_like(v)
            else:
                better = v > bv
                bv = jnp.where(better, v, bv)
                bi = jnp.where(better, float(e), bi)
        return bi

    i1 = argmax_first(sv)
    i2 = argmax_first(sv, skip=i1)
    lo = jnp.minimum(i1, i2)
    hi = jnp.maximum(i1, i2)
    pick = lambda idx: sum(jnp.where(idx == float(e), av[e], 0.0) for e in range(EPG))
    a_lo, a_hi = pick(lo), pick(hi)
    den = a_lo + a_hi
    pair = jnp.where(lo == 0.0, 0.0, jnp.where(lo == 1.0, 3.0, 5.0)) + hi - lo - 1.0
    return grp * float(N_PAIR) + pair, a_lo / den, a_hi / den


def _post_kernel(x_ref, octx_ref, ona_ref, omla_ref, mod_ref, n2_ref, wout_ref, wr_ref, br_ref,
                 x1_ref, h2_ref, route_ref, gate_ref, counts_ref, carry):
    i = pl.program_id(0)

    @pl.when(i == 0)
    def _():
        carry[...] = jnp.zeros_like(carry)

    is_ctx = i < CTX_TILES
    o_lat = jnp.concatenate([ona_ref[...], omla_ref[...]], axis=1)
    o = jnp.where(is_ctx, octx_ref[...], o_lat)
    m = mod_ref[0]
    x1 = x_ref[...] + m[2:3, :] * _dot(o, wout_ref[0])
    x1_ref[...] = x1
    h2 = _rms(x1, n2_ref[0]) * (1.0 + m[4:5, :]) + m[3:4, :]
    logits = lax.dot_general(wr_ref[...], h2, (((1,), (1,)), ((), ())),
                             preferred_element_type=F32, precision=lax.Precision.HIGHEST)
    aff = jax.nn.sigmoid(logits)
    bucket, g_lo, g_hi = _route(aff + br_ref[...], aff)

    hit = lax.broadcasted_iota(jnp.int32, (BUCKET_ROWS, TT), 0).astype(F32) == bucket
    earlier = lax.broadcasted_iota(jnp.int32, (TT, TT), 0) <= lax.broadcasted_iota(jnp.int32, (TT, TT), 1)
    incl = _dot(jnp.where(hit, 1.0, 0.0).astype(BF16), jnp.where(earlier, 1.0, 0.0).astype(BF16))
    base = carry[...]
    rank = jnp.sum(jnp.where(hit, incl + base[:, 0:1], 0.0), axis=0, keepdims=True) - 1.0
    carry[...] = base + incl[:, TT - 1:TT]
    counts_ref[...] = carry[...].astype(jnp.int32)
    route = jnp.concatenate([bucket, rank, jnp.zeros((6, TT), F32)], axis=0)
    route_ref[0] = route.astype(jnp.int32)
    gate_ref[0] = jnp.concatenate([g_lo, g_hi, jnp.zeros((6, TT), F32)], axis=0)
    _to_rows(h2_ref, h2, X_CHUNKS)


def _post(l, x, o_ctx, o_na, o_mla, mods, wts):
    tok = lambda n: pl.BlockSpec((TT, n), lambda i: (i, 0))
    lat = lambda n: pl.BlockSpec((TT, n), lambda i: (jnp.maximum(i - CTX_TILES, 0), 0))
    lay = lambda a: pl.BlockSpec((1,) + a.shape[1:], lambda i: (l,) + (0,) * (a.ndim - 1))
    full = lambda a: pl.BlockSpec(a.shape, lambda i: (0,) * a.ndim)
    return pl.pallas_call(
        _post_kernel,
        grid=(N_TILES,),
        in_specs=[tok(D), pl.BlockSpec((TT, D), lambda i: (_ctx_block(i), 0)), lat(D_NA), lat(D_MLA),
                  pl.BlockSpec((1, 6, D), lambda i: (l * N_MOD_ROWS + _mod_row(i), 0, 0)),
                  lay(wts["n2"]), lay(wts["w_out"]), full(wts["w_rt"]), full(wts["b_r"])],
        out_specs=[tok(D), pl.BlockSpec((TT * X_CHUNKS, LANE), lambda i: (i, 0)),
                   pl.BlockSpec((1, 8, TT), lambda i: (i, 0, 0)),
                   pl.BlockSpec((1, 8, TT), lambda i: (i, 0, 0)),
                   pl.BlockSpec((BUCKET_ROWS, LANE), lambda i: (0, 0))],
        out_shape=[jax.ShapeDtypeStruct((N_TOK, D), F32),
                   jax.ShapeDtypeStruct((N_TOK * X_CHUNKS, LANE), F32),
                   jax.ShapeDtypeStruct((N_TILES, 8, TT), jnp.int32),
                   jax.ShapeDtypeStruct((N_TILES, 8, TT), F32),
                   jax.ShapeDtypeStruct((BUCKET_ROWS, LANE), jnp.int32)],
        scratch_shapes=[pltpu.VMEM((BUCKET_ROWS, LANE), F32)],
        compiler_params=_params(("arbitrary",)),
        name="post",
    )(x, o_ctx, o_na, o_mla, mods, wts["n2"], wts["w_out"], wts["w_rt"], wts["b_r"])


def _moe_index_kernel(counts, elo, ehi, valid, total, first_tile):
    def per_bucket(b, cursor):
        cnt = counts[b]
        n_tiles = (cnt + (TM - 1)) // TM
        first_tile[b] = cursor
        grp = b // N_PAIR
        pair = b - grp * N_PAIR
        lo = jnp.where(pair < 3, 0, jnp.where(pair < 5, 1, 2))
        hi = jnp.where(pair < 3, pair + 1, jnp.where(pair < 5, pair - 1, 3))

        def per_tile(k, c):
            elo[cursor + k] = grp * EPG + lo
            ehi[cursor + k] = grp * EPG + hi
            valid[cursor + k] = jnp.minimum(cnt - k * TM, TM)
            return c
        lax.fori_loop(0, n_tiles, per_tile, 0)
        return cursor + n_tiles
    used = lax.fori_loop(0, N_BUCKET, per_bucket, 0)
    total[0] = used

    def unused_tile(k, c):
        elo[k] = elo[used - 1]
        ehi[k] = ehi[used - 1]
        valid[k] = 0
        return c
    lax.fori_loop(used, MOE_TILES, unused_tile, 0)

    def unused_bucket_row(b, c):
        first_tile[b] = used
        return c
    lax.fori_loop(N_BUCKET, BUCKET_ROWS, unused_bucket_row, 0)


def _moe_index(counts):
    smem = pl.BlockSpec(memory_space=pltpu.SMEM)
    tiles = jax.ShapeDtypeStruct((MOE_TILES,), jnp.int32)
    return pl.pallas_call(
        _moe_index_kernel,
        in_specs=[smem],
        out_specs=[smem] * 5,
        out_shape=[tiles, tiles, tiles, jax.ShapeDtypeStruct((1,), jnp.int32),
                   jax.ShapeDtypeStruct((BUCKET_ROWS,), jnp.int32)],
        name="moe_index",
    )(counts)


def _moe_kernel(elo_ref, ehi_ref, valid_ref, total_ref, slot_ref, glo_tok,
                h2_hbm, wg1, wu1, wd1, wg2, wu2, wd2, out_hbm,
                xbuf, ybuf, gvec, src, dst, glo, gsem, ssem, vsem):
    del elo_ref, ehi_ref
    t = pl.program_id(0)
    total = total_ref[0]
    par = t % 2

    def gate_copy(tile, p):
        return pltpu.make_async_copy(glo.at[pl.ds(pl.multiple_of(tile * TM, TM), TM)], gvec.at[p], vsem.at[p])

    def start_gather(tile, p):
        gate_copy(tile, p).start()
        def body(j, c):
            row = pl.multiple_of(src[tile * TM + j] * X_CHUNKS, X_CHUNKS)
            pltpu.make_async_copy(h2_hbm.at[pl.ds(row, X_CHUNKS), :],
                                  xbuf.at[p, pl.ds(j * X_CHUNKS, X_CHUNKS), :], gsem.at[p]).start()
            return c
        lax.fori_loop(0, TM, body, 0, unroll=DMA_UNROLL)

    def start_scatter(tile, p):
        def body(j, c):
            row = pl.multiple_of(dst[tile * TM + j] * Y_CHUNKS, Y_CHUNKS)
            pltpu.make_async_copy(ybuf.at[p, pl.ds(j * Y_CHUNKS, Y_CHUNKS), :],
                                  out_hbm.at[pl.ds(row, Y_CHUNKS), :], ssem.at[p]).start()
            return c
        lax.fori_loop(0, TM, body, 0, unroll=DMA_UNROLL)

    def wait_gather(p):
        pltpu.make_async_copy(h2_hbm.at[pl.ds(0, TM * X_CHUNKS), :], xbuf.at[p], gsem.at[p]).wait()
        gate_copy(0, p).wait()

    def wait_scatter(p):
        pltpu.make_async_copy(ybuf.at[p], out_hbm.at[pl.ds(0, TM * Y_CHUNKS), :], ssem.at[p]).wait()

    @pl.when(t == 0)
    def _():
        def place(i, c):
            s = slot_ref[i]
            src[s] = i
            dst[s] = i
            glo[s] = glo_tok[i]
            return c
        lax.fori_loop(0, N_TOK, place, 0, unroll=DMA_UNROLL)

        def pad_tile(tile, c):
            def pad_slot(j, c2):
                src[tile * TM + j] = src[tile * TM]
                dst[tile * TM + j] = N_TOK + j
                glo[tile * TM + j] = 0.0
                return c2
            return lax.fori_loop(valid_ref[tile], TM, pad_slot, c)
        lax.fori_loop(0, total, pad_tile, 0)
        ybuf[0] = jnp.zeros((TM * Y_CHUNKS, LANE), F32)
        spare = pltpu.make_async_copy(ybuf.at[0], out_hbm.at[pl.ds(N_TOK * Y_CHUNKS, TM * Y_CHUNKS), :], ssem.at[0])
        spare.start()
        spare.wait()
        start_gather(0, 0)

    @pl.when(t + 1 < total)
    def _():
        start_gather(t + 1, 1 - par)

    @pl.when(t < total)
    def _():
        wait_gather(par)

        @pl.when(t >= 2)
        def _():
            wait_scatter(par)

        x = _from_rows(xbuf, TM, X_CHUNKS, 0, X_CHUNKS, lead=(par,)).astype(BF16)
        g_row = gvec[pl.ds(par, 1), :]
        g_lo = jnp.transpose(jnp.concatenate([g_row, jnp.zeros((7, TM), F32)], axis=0))[:, 0:1]
        y = None
        for gate, (wg, wu, wd) in ((g_lo, (wg1, wu1, wd1)), (1.0 - g_lo, (wg2, wu2, wd2))):
            a = _dot(x, wg[0, 0])
            hid = a * jax.nn.sigmoid(a) * _dot(x, wu[0, 0]) * gate
            part = _dot(hid.astype(BF16), wd[0, 0])
            y = part if y is None else y + part
        _to_rows(ybuf, y, Y_CHUNKS, lead=(par,))
        start_scatter(t, par)

        @pl.when(t == total - 1)
        def _():
            @pl.when(t >= 1)
            def _():
                wait_scatter(1 - par)
            wait_scatter(par)


def _moe(l, h2, route, gate, counts, wts):
    bucket = route[:, 0, :].reshape(N_TOK)
    rank = route[:, 1, :].reshape(N_TOK)
    g_lo = gate[:, 0, :].reshape(N_TOK)
    e_lo, e_hi, valid, total, first_tile = _moe_index(counts[:, 0])
    buckets = jnp.arange(N_BUCKET, dtype=jnp.int32)
    slot = jnp.where(bucket[:, None] == buckets[None, :], (first_tile[:N_BUCKET] * TM)[None, :], 0).sum(axis=1) + rank
    up = lambda which: pl.BlockSpec((1, 1, D, D_EXP), lambda t, *pf: (l, pf[which][t], 0, 0))
    down = lambda which: pl.BlockSpec((1, 1, D_EXP, D), lambda t, *pf: (l, pf[which][t], 0, 0))
    first, second = 0, 1
    slots = MOE_TILES * TM
    return pl.pallas_call(
        _moe_kernel,
        grid_spec=pltpu.PrefetchScalarGridSpec(
            num_scalar_prefetch=6,
            grid=(MOE_TILES,),
            in_specs=[pl.BlockSpec(memory_space=pl.ANY),
                      up(first), up(first), down(first), up(second), up(second), down(second)],
            out_specs=pl.BlockSpec(memory_space=pl.ANY),
            scratch_shapes=[pltpu.VMEM((2, TM * X_CHUNKS, LANE), F32), pltpu.VMEM((2, TM * Y_CHUNKS, LANE), F32),
                            pltpu.VMEM((2, TM), F32),
                            pltpu.SMEM((slots,), jnp.int32), pltpu.SMEM((slots,), jnp.int32),
                            pltpu.SMEM((slots,), F32),
                            pltpu.SemaphoreType.DMA((2,)), pltpu.SemaphoreType.DMA((2,)),
                            pltpu.SemaphoreType.DMA((2,))],
        ),
        out_shape=jax.ShapeDtypeStruct(((N_TOK + TM) * Y_CHUNKS, LANE), F32),
        compiler_params=_params(("arbitrary",)),
        name="moe",
    )(e_lo, e_hi, valid, total, slot, g_lo, h2,
      wts["w_gate"], wts["w_up"], wts["w_down"], wts["w_gate"], wts["w_up"], wts["w_down"])


def _final_kernel(x1_ref, moe_ref, mod_ref, g_ref, yc_ref, yl_ref):
    i = pl.program_id(0)
    moe = _from_rows(moe_ref, TT, Y_CHUNKS, 0, Y_CHUNKS)
    y = _rms(x1_ref[...] + mod_ref[0, 5:6, :] * moe, g_ref[...])

    @pl.when(i < CTX_TILES)
    def _():
        yc_ref[...] = y

    @pl.when(i >= CTX_TILES)
    def _():
        yl_ref[...] = y


def _final(x1, moe, mods, g):
    tok = pl.BlockSpec((TT, D), lambda i: (i, 0))
    l = DEPTH - 1
    return pl.pallas_call(
        _final_kernel,
        grid=(N_TILES,),
        in_specs=[tok, pl.BlockSpec((TT * Y_CHUNKS, LANE), lambda i: (i, 0)),
                  pl.BlockSpec((1, 6, D), lambda i: (l * N_MOD_ROWS + _mod_row(i), 0, 0)),
                  pl.BlockSpec((1, D), lambda i: (0, 0))],
        out_specs=[pl.BlockSpec((TT, D), lambda i: (_ctx_block(i), 0)),
                   pl.BlockSpec((TT, D), lambda i: (jnp.maximum(i - CTX_TILES, 0), 0))],
        out_shape=[jax.ShapeDtypeStruct((N_CTX, D), F32), jax.ShapeDtypeStruct((N_LAT, D), F32)],
        compiler_params=_params(("arbitrary",)),
        name="final",
    )(x1, moe, mods, g)


def _swap_halves(w):
    q = ROPE // 4
    return jnp.concatenate([-w[..., q:2 * q], w[..., :q], -w[..., 3 * q:], w[..., 2 * q:3 * q]], axis=-1)


def _pad_head(nope, rope):
    z = jnp.zeros(rope.shape[:-1] + (HP - NOPE - ROPE,), rope.dtype)
    return jnp.concatenate([nope, rope, z], axis=-1)


def _prepare_weights(norm1_g, norm2_g, w_in, q_norm_g, w_uq, kv_norm_g, w_ukv, w_out, w_router, b_router,
                     w_gate, w_up, w_down):
    o = 3 * D_NA + Q_LORA + KV_LORA
    w_kr = w_in[:, :, o:]
    zeros_nope = jnp.zeros((DEPTH, D, NOPE), F32)
    w_in_p = jnp.concatenate([w_in[:, :, :o], _pad_head(zeros_nope, w_kr),
                              _pad_head(zeros_nope, _swap_halves(w_kr))], axis=-1)
    wq = w_uq.reshape(DEPTH, Q_LORA, H_MLA, NOPE + ROPE)
    plain = _pad_head(wq[..., :NOPE], wq[..., NOPE:]).reshape(DEPTH, Q_LORA, D_QP)
    swapped = _pad_head(jnp.zeros_like(wq[..., :NOPE]), _swap_halves(wq[..., NOPE:])).reshape(DEPTH, Q_LORA, D_QP)
    wkv = w_ukv.reshape(DEPTH, KV_LORA, H_MLA, NOPE + VH)
    w_uk_p = jnp.concatenate([wkv[..., :NOPE], jnp.zeros((DEPTH, KV_LORA, H_MLA, HP - NOPE), F32)],
                             axis=-1).reshape(DEPTH, KV_LORA, D_QP)
    w_uv = wkv[..., NOPE:].reshape(DEPTH, KV_LORA, D_MLA)
    return {
        "n1": norm1_g.reshape(DEPTH, 1, D),
        "n2": norm2_g.reshape(DEPTH, 1, D),
        "w_in": w_in_p.astype(BF16),
        "qn": q_norm_g.reshape(DEPTH, 1, Q_LORA),
        "w_uq": jnp.concatenate([plain, swapped], axis=-1).astype(BF16),
        "kvn": kv_norm_g.reshape(DEPTH, 1, KV_LORA),
        "w_uk": w_uk_p.astype(BF16),
        "w_uv": w_uv.astype(BF16),
        "w_out": w_out.astype(BF16),
        "w_rt": w_router.T,
        "b_r": b_router.reshape(N_EXP, 1),
        "w_gate": w_gate.astype(BF16),
        "w_up": w_up.astype(BF16),
        "w_down": w_down.astype(BF16),
    }


def _rope_tables():
    half = ROPE // 2
    freqs = 1.0 / (ROPE_THETA ** (np.arange(0, half, 2, dtype=np.float32) / half))
    pos = np.arange(S_LAT)
    ang_r = (pos // GRID_W).astype(np.float32)[:, None] * freqs
    ang_c = (pos % GRID_W).astype(np.float32)[:, None] * freqs
    ang = np.concatenate([ang_r, ang_r, ang_c, ang_c], axis=1)
    cos = np.ones((TT + S_LAT, LANE), np.float32)
    sin = np.zeros((TT + S_LAT, LANE), np.float32)
    cos[TT:, NOPE:NOPE + ROPE] = np.cos(ang)
    sin[TT:, NOPE:NOPE + ROPE] = np.sin(ang)
    return jnp.asarray(cos), jnp.asarray(sin)


def kernel(x_prompt, x_sample, cache_nat_k, cache_nat_v, cache_mla_ckv, cache_mla_krope, c, c_ctx, norm1_g, norm2_g, w_ada, b_ada, w_in, q_norm_g, w_uq, kv_norm_g, w_ukv, rpb, w_out, w_router, b_router, w_gate, w_up, w_down, final_norm_g):
    wts = _prepare_weights(norm1_g, norm2_g, w_in, q_norm_g, w_uq, kv_norm_g, w_ukv, w_out, w_router, b_router,
                           w_gate, w_up, w_down)
    tabs = _rope_tables()
    cond = jnp.concatenate([c_ctx[None, :], c, jnp.zeros((N_MOD_ROWS - 1 - B_LAT, D), F32)], axis=0)
    mods = _ada(cond, w_ada, b_ada).reshape(DEPTH * N_MOD_ROWS, 6, D)
    place = jnp.asarray(np.eye(ROPE, HP, k=NOPE, dtype=np.float32)).astype(BF16)
    kc_all, vc_all = _ctxkv(cache_mla_ckv, cache_mla_krope, wts["w_uk"], wts["w_uv"], place)
    cache_k = cache_nat_k.reshape(B_LAT, DEPTH, PAST, D_NA)
    cache_v = cache_nat_v.reshape(B_LAT, DEPTH, PAST, D_NA)
    bias = _na_bias_tables(rpb)

    x1 = moe = None
    new_k, new_v, new_ckv, new_kr = [], [], [], []
    for l in range(DEPTH):
        xin = (x_prompt.reshape(N_CTX, D), x_sample.reshape(N_LAT, D)) if l == 0 else (x1, moe)
        x, qna, kna, vna, qp, kp, vm, k_new, v_new, ckv_new, kr_new = _pre(l, l == 0, xin, mods, wts, tabs)
        new_k.append(k_new)
        new_v.append(v_new)
        new_ckv.append(ckv_new)
        new_kr.append(kr_new)
        o_ctx = _ctx_attn(qna, kna, vna, qp, kp, vm)
        o_na = _na_attn(l, qna, kna, vna, cache_k, cache_v, bias)
        o_mla = _mla_attn(l, qp, kp, vm, kc_all, vc_all)
        x1, h2, route, gate, counts = _post(l, x, o_ctx, o_na, o_mla, mods, wts)
        moe = _moe(l, h2, route, gate, counts, wts)
    y_ctx, y_lat = _final(x1, moe, mods, final_norm_g.reshape(1, D))

    stack = lambda parts, tail: jnp.stack([p.reshape((B_CTX, S_CTX) + tail) for p in parts], axis=1)
    return (y_ctx.reshape(B_CTX, S_CTX, D), y_lat.reshape(B_LAT, S_LAT, D),
            stack(new_k, (H_NA, DH_NA)), stack(new_v, (H_NA, DH_NA)),
            stack(new_ckv, (KV_LORA,)), stack(new_kr, (ROPE,)))
```
